```python
import math
import jax
import jax.numpy as jnp
from jax import lax
import numpy as np

D_MODEL = 1024
BATCH = 8
SEQ = 2048
DEPTH = 2

GRID_W = 64
HEAD_DIM = 64
ATTN_HEADS = 8
ATTN_KV_HEADS = 2
ATTN_GROUP = ATTN_HEADS // ATTN_KV_HEADS
ATTN_WIDTH = ATTN_HEADS * HEAD_DIM
KV_WIDTH = ATTN_KV_HEADS * HEAD_DIM
ATTN_IN = ATTN_WIDTH + 2 * KV_WIDTH
Q_BLOCK = 128
ROPE_THETA = 10000.0
RWKV_HEADS = 8
RWKV_HEAD = 64
RWKV_WIDTH = RWKV_HEADS * RWKV_HEAD
DECAY_LORA = 64
AAA_LORA = 64
GATE_LORA = 160
RWKV_IN = 3 * RWKV_WIDTH + 2 * DECAY_LORA + 2 * AAA_LORA + GATE_LORA
MIX_IN = ATTN_IN + RWKV_IN
GN_EPS = 64e-5
HYENA_ORDER = 2
HYENA_BANDS = 16
HYENA_EMB = 2 * HYENA_BANDS + 1
HYENA_FILTER_HIDDEN = 64
HYENA_TARGET = 1e-2
HYENA_FAST_DECAY = 0.3
HYENA_SLOW_DECAY = 1.5
SHORT_CONV = 3
D_FF = -(-(8 * D_MODEL) // (3 * 256)) * 256
NORM_EPS = 1e-6

kernel_name = 'hybrid_attn_rwkv7_hyena_encoder'


def rms_norm(x, gain):
    xf = x.astype(jnp.float32)
    y = xf * lax.rsqrt(jnp.mean(xf * xf, axis=-1, keepdims=True) + NORM_EPS)
    return (y * gain.astype(jnp.float32)).astype(x.dtype)


def axial_rope_angles(n_tok):
    rows = n_tok // GRID_W
    row = jnp.repeat(jnp.arange(rows), GRID_W).astype(jnp.float32)
    col = jnp.tile(jnp.arange(GRID_W), rows).astype(jnp.float32)
    half = HEAD_DIM // 2
    inv_freq = ROPE_THETA ** (-jnp.arange(0, half, 2, dtype=jnp.float32) / half)
    ang = jnp.concatenate([row[:, None] * inv_freq, col[:, None] * inv_freq], axis=-1)
    return jnp.cos(ang), jnp.sin(ang)


def apply_rope(x, cos, sin):
    b, l, h, d = x.shape
    xf = x.astype(jnp.float32).reshape(b, l, h, d // 2, 2)
    x0, x1 = xf[..., 0], xf[..., 1]
    cb = cos[None, :, None, :]
    sb = sin[None, :, None, :]
    out = jnp.stack([x0 * cb - x1 * sb, x0 * sb + x1 * cb], axis=-1)
    return out.reshape(b, l, h, d).astype(x.dtype)


def blocked_gqa(q, k, v):
    b, l, _, _ = q.shape
    nb = l // Q_BLOCK
    qb = q.reshape(b, nb, Q_BLOCK, ATTN_KV_HEADS, ATTN_GROUP, HEAD_DIM).transpose(1, 0, 2, 3, 4, 5)
    scale = HEAD_DIM ** -0.5

    def one_block(q_blk):
        s = jnp.einsum('bqhgd,bkhd->bhgqk', q_blk, k, preferred_element_type=jnp.float32) * scale
        p = jax.nn.softmax(s, axis=-1)
        return jnp.einsum('bhgqk,bkhd->bqhgd', p.astype(v.dtype), v)

    out = lax.map(one_block, qb)
    return out.transpose(1, 0, 2, 3, 4, 5).reshape(b, l, ATTN_WIDTH)


def centred_shift(p):
    prev = jnp.pad(p[:, :-1], ((0, 0), (1, 0), (0, 0)))
    nxt = jnp.pad(p[:, 1:], ((0, 0), (0, 1), (0, 0)))
    return 0.5 * (prev + nxt)


def rwkv7_scan(r, w, k, v, z, bb, reverse):
    b, l, h, n = r.shape
    xs = tuple(t.transpose(1, 0, 2, 3) for t in (r, w, k, v, z, bb))

    def step(state, inp):
        r_t, w_t, k_t, v_t, z_t, b_t = inp
        sz = jnp.einsum('bhvk,bhk->bhv', state, z_t)
        state = (state * w_t[:, :, None, :] + sz[..., None] * b_t[:, :, None, :]
                 + v_t[..., None] * k_t[:, :, None, :])
        return state, jnp.einsum('bhvk,bhk->bhv', state, r_t)

    s0 = jnp.zeros((b, h, n, n), jnp.float32)
    _, ys = lax.scan(step, s0, xs, reverse=reverse)
    return ys.transpose(1, 0, 2, 3)


def rwkv7_bidirectional(p, mu, w0, w_up, a0, a_up, g_up, k_k, k_a, r_k, ln_g, ln_b):
    out_dtype = p.dtype
    p = p.astype(jnp.float32)
    b, l, _ = p.shape
    H, N, C = RWKV_HEADS, RWKV_HEAD, RWKV_WIDTH
    p = p + (centred_shift(p) - p) * mu
    o_w = 3 * C
    o_a = o_w + 2 * DECAY_LORA
    o_g = o_a + 2 * AAA_LORA
    r = p[..., :C]
    k = p[..., C:2 * C]
    v = p[..., 2 * C:o_w]
    w_lo = p[..., o_w:o_a].reshape(b, l, 2, DECAY_LORA)
    a_lo = p[..., o_a:o_g].reshape(b, l, 2, AAA_LORA)
    g_lo = p[..., o_g:]
    decay = jnp.exp(-math.exp(-0.5) * jax.nn.sigmoid(
        w0 + jnp.einsum('bldr,drc->bldc', jnp.tanh(w_lo), w_up)))
    a = jax.nn.sigmoid(a0 + jnp.einsum('bldr,drc->bldc', a_lo, a_up))
    kk = (k * k_k).reshape(b, l, H, N)
    kk = kk * lax.rsqrt(jnp.maximum(jnp.sum(kk * kk, axis=-1, keepdims=True), 1e-24))
    k_dir = k[:, :, None, :] * (1.0 + (a - 1.0) * k_a)

    def heads(t):
        return t.reshape(b, l, H, N)

    rh, vh = heads(r), heads(v)
    y_fwd = rwkv7_scan(rh, heads(decay[:, :, 0]), heads(k_dir[:, :, 0]), vh, -kk,
                       kk * heads(a[:, :, 0]), reverse=False)
    y_bwd = rwkv7_scan(rh, heads(decay[:, :, 1]), heads(k_dir[:, :, 1]), vh, -kk,
                       kk * heads(a[:, :, 1]), reverse=True)
    y = y_fwd + y_bwd
    mean = jnp.mean(y, axis=-1, keepdims=True)
    var = jnp.mean(jnp.square(y - mean), axis=-1, keepdims=True)
    y = ((y - mean) * lax.rsqrt(var + GN_EPS)).reshape(b, l, C) * ln_g + ln_b
    bonus = (jnp.sum(rh * heads(k) * r_k, axis=-1, keepdims=True) * vh).reshape(b, l, C)
    g = jax.nn.sigmoid(g_lo) @ g_up
    return ((y + bonus) * g).astype(out_dtype)


def attention_rwkv_mixer(h, cos, sin, w_in, w_out, q_norm, k_norm, mu, w0, w_up, a0, a_up,
                         g_up, k_k, k_a, r_k, ln_g, ln_b):
    b, l, _ = h.shape
    p = h @ w_in
    q = p[..., :ATTN_WIDTH].reshape(b, l, ATTN_HEADS, HEAD_DIM)
    k = p[..., ATTN_WIDTH:ATTN_WIDTH + KV_WIDTH].reshape(b, l, ATTN_KV_HEADS, HEAD_DIM)
    v = p[..., ATTN_WIDTH + KV_WIDTH:ATTN_IN].reshape(b, l, ATTN_KV_HEADS, HEAD_DIM)
    q = apply_rope(rms_norm(q, q_norm), cos, sin)
    k = apply_rope(rms_norm(k, k_norm), cos, sin)
    y_attn = blocked_gqa(q, k, v)
    y_rwkv = rwkv7_bidirectional(p[..., ATTN_IN:], mu, w0, w_up, a0, a_up, g_up,
                                 k_k, k_a, r_k, ln_g, ln_b)
    return jnp.concatenate([y_attn, y_rwkv.astype(y_attn.dtype)], axis=-1) @ w_out


def hyena_filters(n_tok, w1, b1, w2, b2, w3, b3, sin_freq, w_out):
    f32 = jnp.float32
    t = jnp.linspace(0.0, 1.0, n_tok, dtype=f32)[:, None]
    omega = (2.0 * math.pi / n_tok) * jnp.arange(n_tok, dtype=f32)[:, None]
    bands = jnp.linspace(1e-4, HYENA_BANDS - 1, HYENA_BANDS, dtype=f32)[None, :]
    feats = jnp.concatenate([t, jnp.cos(bands * omega), -jnp.sin(bands * omega)], axis=-1)
    fr = sin_freq.astype(f32)
    hid = jnp.sin(fr * (feats @ w1.astype(f32) + b1))
    hid = jnp.sin(fr * (hid @ w2.astype(f32) + b2))
    hid = jnp.sin(fr * (hid @ w3.astype(f32) + b3))
    filt = (hid @ w_out.astype(f32)).reshape(n_tok, HYENA_ORDER, 2, D_MODEL)
    min_decay = math.log(HYENA_TARGET) / HYENA_SLOW_DECAY
    max_decay = math.log(HYENA_TARGET) / HYENA_FAST_DECAY
    deltas = jnp.abs(jnp.linspace(min_decay, max_decay, D_MODEL, dtype=f32))
    window = jnp.exp(-t * deltas)
    return filt * window[:, None, None, :]


def bidirectional_fft_conv(u, h_fwd, h_bwd, skip):
    n = u.shape[1]
    uf = u.astype(jnp.float32)
    taps = jnp.concatenate([h_fwd, jnp.zeros_like(h_fwd[:1]), h_bwd[:0:-1]], axis=0)
    spec = jnp.fft.rfft(uf, n=2 * n, axis=1) * jnp.fft.rfft(taps, n=2 * n, axis=0)[None]
    y = jnp.fft.irfft(spec, n=2 * n, axis=1)[:, :n]
    return (y + uf * skip.astype(jnp.float32)).astype(u.dtype)


def hyena_mixer(h, w_in, conv_w, conv_b, f_w1, f_b1, f_w2, f_b2, f_w3, f_b3, sin_freq,
                f_out, skip, w_out):
    b, l, d = h.shape
    p = h @ w_in
    p = lax.conv_general_dilated(
        p, conv_w[:, None, :].astype(p.dtype), window_strides=(1,),
        padding=((SHORT_CONV // 2, SHORT_CONV // 2),),
        dimension_numbers=('NWC', 'WIO', 'NWC'), feature_group_count=p.shape[-1]) + conv_b
    v, x1, x2 = jnp.split(p, 3, axis=-1)
    filt = hyena_filters(l, f_w1, f_b1, f_w2, f_b2, f_w3, f_b3, sin_freq, f_out)
    z = v
    for o, gate in enumerate((x1, x2)):
        z = gate * bidirectional_fft_conv(z, filt[:, o, 0], filt[:, o, 1], skip[o])
    return z @ w_out


def swiglu(h, w1, w3, w2):
    return (jax.nn.silu(h @ w1) * (h @ w3)) @ w2


def setup_inputs(seed: int = 0) -> dict:
    key = jax.random.key(seed)
    keys = iter(jax.random.split(key, 64))
    f32 = jnp.float32
    D = D_MODEL
    HF = HYENA_FILTER_HIDDEN
    ne = (DEPTH + 1) // 2
    no = DEPTH // 2

    def normal(shape, std):
        return jax.random.normal(next(keys), shape, f32) * std

    def uniform(shape, lo, hi):
        return jax.random.uniform(next(keys), shape, f32, lo, hi)

    return {
        'x': normal((BATCH, SEQ, D), 1.0),
        'c': normal((BATCH, D), 1.0),
        'mix_w_in': normal((ne, D, MIX_IN), D ** -0.5),
        'mix_w_out': normal((ne, D, D), D ** -0.5),
        'attn_q_norm': 1.0 + normal((ne, HEAD_DIM), 0.02),
        'attn_k_norm': 1.0 + normal((ne, HEAD_DIM), 0.02),
        'rwkv_mu': uniform((ne, RWKV_IN), 0.1, 0.9),
        'rwkv_w0': normal((ne, 2, RWKV_WIDTH), 1.0) - 0.5,
        'rwkv_w_up': normal((ne, 2, DECAY_LORA, RWKV_WIDTH), 0.5 * DECAY_LORA ** -0.5),
        'rwkv_a0': normal((ne, 2, RWKV_WIDTH), 0.5),
        'rwkv_a_up': normal((ne, 2, AAA_LORA, RWKV_WIDTH), 0.5 * AAA_LORA ** -0.5),
        'rwkv_g_up': normal((ne, GATE_LORA, RWKV_WIDTH), GATE_LORA ** -0.5),
        'rwkv_k_k': 0.85 + normal((ne, RWKV_WIDTH), 0.02),
        'rwkv_k_a': 1.0 + normal((ne, RWKV_WIDTH), 0.02),
        'rwkv_r_k': normal((ne, RWKV_HEADS, RWKV_HEAD), 0.1),
        'rwkv_ln_g': 1.0 + normal((ne, RWKV_WIDTH), 0.02),
        'rwkv_ln_b': normal((ne, RWKV_WIDTH), 0.02),
        'hy_w_in': normal((no, D, 3 * D), D ** -0.5),
        'hy_conv_w': normal((no, SHORT_CONV, 3 * D), 0.6),
        'hy_conv_b': normal((no, 3 * D), 0.02),
        'hy_f_w1': normal((no, HYENA_EMB, HF), HYENA_EMB ** -0.5),
        'hy_f_b1': normal((no, HF), 0.1),
        'hy_f_w2': normal((no, HF, HF), HF ** -0.5),
        'hy_f_b2': normal((no, HF), 0.1),
        'hy_f_w3': normal((no, HF, HF), HF ** -0.5),
        'hy_f_b3': normal((no, HF), 0.1),
        'hy_sin_freq': 1.0 + normal((no, HF), 0.02),
        'hy_f_out': normal((no, HF, HYENA_ORDER * 2 * D), 0.05 * HF ** -0.5),
        'hy_skip': normal((no, HYENA_ORDER, D), 0.5),
        'hy_w_out': normal((no, D, D), D ** -0.5),
        'ada_w': normal((DEPTH, D, 6 * D), 0.5 * D ** -0.5),
        'ada_b': normal((DEPTH, 6 * D), 0.02),
        'norm_mix': 1.0 + normal((DEPTH, D), 0.02),
        'norm_ffn': 1.0 + normal((DEPTH, D), 0.02),
        'ffn_w1': normal((DEPTH, D, D_FF), D ** -0.5),
        'ffn_w3': normal((DEPTH, D, D_FF), D ** -0.5),
        'ffn_w2': normal((DEPTH, D_FF, D), D_FF ** -0.5),
        'final_norm': 1.0 + normal((D,), 0.02),
    }


def reference(x, c, mix_w_in, mix_w_out, attn_q_norm, attn_k_norm, rwkv_mu, rwkv_w0, rwkv_w_up,
              rwkv_a0, rwkv_a_up, rwkv_g_up, rwkv_k_k, rwkv_k_a, rwkv_r_k, rwkv_ln_g, rwkv_ln_b,
              hy_w_in, hy_conv_w, hy_conv_b, hy_f_w1, hy_f_b1, hy_f_w2, hy_f_b2, hy_f_w3, hy_f_b3,
              hy_sin_freq, hy_f_out, hy_skip, hy_w_out, ada_w, ada_b, norm_mix, norm_ffn,
              ffn_w1, ffn_w3, ffn_w2, final_norm):
    n_tok = x.shape[1]
    cos, sin = axial_rope_angles(n_tok)
    cond = jax.nn.silu(c)
    for layer in range(DEPTH):
        mod = (cond @ ada_w[layer] + ada_b[layer])[:, None, :]
        shift_m, scale_m, gate_m, shift_f, scale_f, gate_f = jnp.split(mod, 6, axis=-1)
        h = rms_norm(x, norm_mix[layer]) * (1.0 + scale_m) + shift_m
        i = layer // 2
        if layer % 2 == 0:
            y = attention_rwkv_mixer(h, cos, sin, mix_w_in[i], mix_w_out[i], attn_q_norm[i],
                                     attn_k_norm[i], rwkv_mu[i], rwkv_w0[i], rwkv_w_up[i],
                                     rwkv_a0[i], rwkv_a_up[i], rwkv_g_up[i], rwkv_k_k[i],
                                     rwkv_k_a[i], rwkv_r_k[i], rwkv_ln_g[i], rwkv_ln_b[i])
        else:
            y = hyena_mixer(h, hy_w_in[i], hy_conv_w[i], hy_conv_b[i], hy_f_w1[i], hy_f_b1[i],
                            hy_f_w2[i], hy_f_b2[i], hy_f_w3[i], hy_f_b3[i], hy_sin_freq[i],
                            hy_f_out[i], hy_skip[i], hy_w_out[i])
        x = x + gate_m * y
        h = rms_norm(x, norm_ffn[layer]) * (1.0 + scale_f) + shift_f
        x = x + gate_f * swiglu(h, ffn_w1[layer], ffn_w3[layer], ffn_w2[layer])
    return rms_norm(x, final_norm)
```

```python
import functools
import math

import jax
import jax.numpy as jnp
from jax import lax
from jax.experimental import pallas as pl
from jax.experimental.pallas import tpu as pltpu

F32 = jnp.float32
BF16 = jnp.bfloat16
HIGHEST = lax.Precision.HIGHEST

GRID_W = 64
HEAD_DIM = 64
ATTN_HEADS = 8
ATTN_KV_HEADS = 2
ATTN_WIDTH = ATTN_HEADS * HEAD_DIM
KV_WIDTH = ATTN_KV_HEADS * HEAD_DIM
ROPE_THETA = 10000.0
RWKV_HEADS = 8
RWKV_HEAD = 64
RWKV_WIDTH = RWKV_HEADS * RWKV_HEAD
DECAY_LORA = 64
AAA_LORA = 64
GATE_LORA = 160
RWKV_IN = 3 * RWKV_WIDTH + 2 * DECAY_LORA + 2 * AAA_LORA + GATE_LORA
RWKV_IN_PAD = 2048
GN_EPS = 64e-5
HYENA_BANDS = 16
HYENA_TARGET = 1e-2
HYENA_FAST_DECAY = 0.3
HYENA_SLOW_DECAY = 1.5
NORM_EPS = 1e-6

V7X_VMEM_BYTES = 64 * 1024 * 1024
VMEM_LIMIT = V7X_VMEM_BYTES - 8 * 1024 * 1024
FREQ_TILE = 256
SUBLANES = 8


def _params(*sem):
    return pltpu.CompilerParams(dimension_semantics=sem, vmem_limit_bytes=VMEM_LIMIT)


def _const_spec(shape):
    zeros = (0,) * len(shape)
    return pl.BlockSpec(shape, lambda *_: zeros, pipeline_mode=pl.Buffered(1))


def _dot(a, b):
    return jnp.dot(a, b, preferred_element_type=F32)


def _dot_hi(a, b):
    return jnp.dot(a, b, precision=HIGHEST, preferred_element_type=F32)


def _split_bf16(x):
    hi = x.astype(BF16)
    lo = (x - hi.astype(F32)).astype(BF16)
    return hi, lo


def _rms_mod(x, gain, scale, shift):
    ms = jnp.mean(x * x, axis=-1, keepdims=True)
    return (x * lax.rsqrt(ms + NORM_EPS) * gain) * (1.0 + scale) + shift


def _ada_kernel(c_ref, w_ref, b_ref, o_ref):
    c = c_ref[...]
    cond = c * jax.nn.sigmoid(c)
    o_ref[...] = _dot_hi(cond, w_ref[...]) + b_ref[...]


def ada_mod(c, ada_w, ada_b):
    depth, d, n = ada_w.shape
    b = c.shape[0]
    tn = 1536
    return pl.pallas_call(
        _ada_kernel,
        grid=(depth, n // tn),
        in_specs=[pl.BlockSpec((b, d), lambda l, j: (0, 0)),
                  pl.BlockSpec((None, d, tn), lambda l, j: (l, 0, j)),
                  pl.BlockSpec((None, 1, tn), lambda l, j: (l, 0, j))],
        out_specs=pl.BlockSpec((None, b, tn), lambda l, j: (l, 0, j)),
        out_shape=jax.ShapeDtypeStruct((depth, b, n), F32),
        compiler_params=_params("parallel", "parallel"),
        name="ada_mod",
    )(c, ada_w, ada_b.reshape(depth, 1, n))


def _mod_spec(d, layer, chunk):
    return pl.BlockSpec((None, None, 1, d), lambda b, *_: (layer, b, 0, chunk))


def _proj_kernel(n_out, x_ref, sh_ref, sc_ref, g_ref, *refs):
    h = _rms_mod(x_ref[0], g_ref[...], sc_ref[...], sh_ref[...]).astype(BF16)
    for w_ref, o_ref in zip(refs[:n_out], refs[n_out:]):
        o_ref[0] = _dot(h, w_ref[...])


def proj(x, mod4, layer, gain, weights, tm):
    b, l, d = x.shape
    n_out = len(weights)
    return pl.pallas_call(
        functools.partial(_proj_kernel, n_out),
        grid=(b, l // tm),
        in_specs=[pl.BlockSpec((1, tm, d), lambda bi, i: (bi, i, 0)),
                  _mod_spec(d, layer, 0), _mod_spec(d, layer, 1),
                  _const_spec((1, d))] + [_const_spec(w.shape) for w in weights],
        out_specs=[pl.BlockSpec((1, tm, w.shape[1]), lambda bi, i: (bi, i, 0)) for w in weights],
        out_shape=[jax.ShapeDtypeStruct((b, l, w.shape[1]), F32) for w in weights],
        compiler_params=_params("parallel", "parallel"),
        name="proj",
    )(x, mod4, mod4, gain.reshape(1, d), *weights)


def _headnorm_rope(x, gain, cos, sin):
    ms = jnp.mean(x * x, axis=-1, keepdims=True)
    y = x * lax.rsqrt(ms + NORM_EPS) * gain
    half = HEAD_DIM // 2
    rot = jnp.concatenate([-y[:, half:], y[:, :half]], axis=-1)
    return y * cos + rot * sin


def _attn_kernel(q_ref, kv_ref, cq_ref, sq_ref, ck_ref, sk_ref, qn_ref, kn_ref, o_ref, k_s, v_s):
    @pl.when(pl.program_id(1) == 0)
    def _():
        kv = kv_ref[0]
        for hk in range(ATTN_KV_HEADS):
            k = kv[:, HEAD_DIM * hk:HEAD_DIM * (hk + 1)]
            k_s[hk] = _headnorm_rope(k, kn_ref[...], ck_ref[...], sk_ref[...]).astype(BF16)
            v_s[hk] = kv[:, KV_WIDTH + HEAD_DIM * hk:KV_WIDTH + HEAD_DIM * (hk + 1)].astype(BF16)

    q = q_ref[0]
    group = ATTN_HEADS // ATTN_KV_HEADS
    outs = []
    for h in range(ATTN_HEADS):
        qh = _headnorm_rope(q[:, HEAD_DIM * h:HEAD_DIM * (h + 1)], qn_ref[...], cq_ref[...], sq_ref[...])
        qh = (qh * (HEAD_DIM ** -0.5)).astype(BF16)
        s = lax.dot_general(qh, k_s[h // group], (((1,), (1,)), ((), ())), preferred_element_type=F32)
        m = jnp.max(s, axis=-1, keepdims=True)
        p = jnp.exp(s - m)
        denom = jnp.sum(p, axis=-1, keepdims=True)
        o = _dot(p.astype(BF16), v_s[h // group])
        outs.append(o / denom)
    o_ref[0] = jnp.concatenate(outs, axis=-1)


def attn(q, kv, cos, sin, qn, kn, tq):
    b, l, _ = q.shape
    return pl.pallas_call(
        _attn_kernel,
        grid=(b, l // tq),
        in_specs=[pl.BlockSpec((1, tq, ATTN_WIDTH), lambda bi, i: (bi, i, 0)),
                  pl.BlockSpec((1, l, 2 * KV_WIDTH), lambda bi, i: (bi, 0, 0)),
                  pl.BlockSpec((tq, HEAD_DIM), lambda bi, i: (i, 0)),
                  pl.BlockSpec((tq, HEAD_DIM), lambda bi, i: (i, 0)),
                  _const_spec((l, HEAD_DIM)), _const_spec((l, HEAD_DIM)),
                  _const_spec((1, HEAD_DIM)), _const_spec((1, HEAD_DIM))],
        out_specs=pl.BlockSpec((1, tq, ATTN_WIDTH), lambda bi, i: (bi, i, 0)),
        out_shape=jax.ShapeDtypeStruct((b, l, ATTN_WIDTH), F32),
        scratch_shapes=[pltpu.VMEM((ATTN_KV_HEADS, l, HEAD_DIM), BF16),
                        pltpu.VMEM((ATTN_KV_HEADS, l, HEAD_DIM), BF16)],
        compiler_params=_params("parallel", "arbitrary"),
        name="attn",
    )(q, kv, cos, sin, cos, sin, qn, kn)


def _halo_specs(tt, w, l):
    nb = l // SUBLANES
    per = tt // SUBLANES
    main = pl.BlockSpec((1, tt, w), lambda bi, i: (bi, i, 0))
    prev = pl.BlockSpec((1, SUBLANES, w), lambda bi, i: (bi, jnp.maximum(i * per - 1, 0), 0))
    nxt = pl.BlockSpec((1, SUBLANES, w), lambda bi, i: (bi, jnp.minimum((i + 1) * per, nb - 1), 0))
    return [main, prev, nxt]


def _neighbours(cur, xp_ref, xn_ref, axis):
    i = pl.program_id(axis)
    last = pl.num_programs(axis) - 1
    tt = cur.shape[0]
    row = lax.broadcasted_iota(jnp.int32, cur.shape, 0)
    prow = jnp.where(i > 0, xp_ref[0, SUBLANES - 1:SUBLANES, :], 0.0)
    nrow = jnp.where(i < last, xn_ref[0, 0:1, :], 0.0)
    prev = jnp.where(row == 0, prow, pltpu.roll(cur, 1, 0))
    nxt = jnp.where(row == tt - 1, nrow, pltpu.roll(cur, tt - 1, 0))
    return prev, nxt


def _rwkv_prep_kernel(x_ref, xp_ref, xn_ref, mu_ref, w0_ref, wup_ref, a0_ref, aup_ref, gup_ref,
                      kk_ref, ka_ref, rk_ref, seg_ref,
                      r_o, z_o, v_o, w_o, kd_o, b_o, bonus_o, g_o):
    cur = x_ref[0]
    prev, nxt = _neighbours(cur, xp_ref, xn_ref, 1)
    ps = cur + (0.5 * (prev + nxt) - cur) * mu_ref[...]
    c = RWKV_WIDTH
    r = ps[:, 0:c]
    k = ps[:, c:2 * c]
    v = ps[:, 2 * c:3 * c]
    o_a = 3 * c + 2 * DECAY_LORA
    o_g = o_a + 2 * AAA_LORA
    kk = k * kk_ref[...]
    kk = kk * lax.rsqrt(jnp.maximum(_dot_hi(kk * kk, seg_ref[...]), 1e-24))
    r_o[0] = r
    v_o[0] = v
    z_o[0] = -kk
    for d in range(2):
        w_lo = ps[:, 3 * c + DECAY_LORA * d:3 * c + DECAY_LORA * (d + 1)]
        a_lo = ps[:, o_a + AAA_LORA * d:o_a + AAA_LORA * (d + 1)]
        decay = jnp.exp(-math.exp(-0.5) * jax.nn.sigmoid(w0_ref[d] + _dot_hi(jnp.tanh(w_lo), wup_ref[d])))
        a = jax.nn.sigmoid(a0_ref[d] + _dot_hi(a_lo, aup_ref[d]))
        w_o[d, 0] = decay
        kd_o[d, 0] = k * (1.0 + (a - 1.0) * ka_ref[...])
        b_o[d, 0] = kk * a
    bonus_o[0] = _dot_hi(r * k * rk_ref[...], seg_ref[...]) * v
    g_o[0] = _dot_hi(jax.nn.sigmoid(ps[:, o_g:RWKV_IN_PAD]), gup_ref[...])


def rwkv_prep(rw, mu, w0, w_up, a0, a_up, g_up, k_k, k_a, r_k, seg, tt):
    b, l, w = rw.shape
    c = RWKV_WIDTH
    one = jax.ShapeDtypeStruct((b, l, c), F32)
    two = jax.ShapeDtypeStruct((2, b, l, c), F32)
    spec1 = pl.BlockSpec((1, tt, c), lambda bi, i: (bi, i, 0))
    spec2 = pl.BlockSpec((2, 1, tt, c), lambda bi, i: (0, bi, i, 0))
    consts = [mu, w0, w_up, a0, a_up, g_up, k_k, k_a, r_k, seg]
    return pl.pallas_call(
        _rwkv_prep_kernel,
        grid=(b, l // tt),
        in_specs=_halo_specs(tt, w, l) + [_const_spec(a.shape) for a in consts],
        out_specs=[spec1, spec1, spec1, spec2, spec2, spec2, spec1, spec1],
        out_shape=[one, one, one, two, two, two, one, one],
        compiler_params=_params("parallel", "parallel"),
        name="rwkv_prep",
    )(rw, rw, rw, *consts)


def _rwkv_scan_kernel(r_ref, w_ref, kd_ref, v_ref, z_ref, b_ref, y_ref, s_ref):
    @pl.when(pl.program_id(0) == 0)
    def _():
        s_ref[...] = jnp.zeros_like(s_ref)

    n = RWKV_HEAD

    def step(t, carry):
        sz = s_ref[0] * z_ref[t, 0:1, :]
        for k in range(1, n):
            sz = sz + s_ref[k] * z_ref[t, k:k + 1, :]
        vt = v_ref[t]
        y = None
        for k in range(n):
            sk = s_ref[k] * w_ref[t, k:k + 1, :] + sz * b_ref[t, k:k + 1, :] + vt * kd_ref[t, k:k + 1, :]
            s_ref[k] = sk
            yk = sk * r_ref[t, k:k + 1, :]
            y = yk if y is None else y + yk
        y_ref[t] = y
        return carry

    lax.fori_loop(0, r_ref.shape[0], step, 0)


def rwkv_scan(r, w, kd, v, z, bb, tb):
    l, n, lanes = r.shape
    spec = pl.BlockSpec((tb, n, lanes), lambda i: (i, 0, 0))
    return pl.pallas_call(
        _rwkv_scan_kernel,
        grid=(l // tb,),
        in_specs=[spec] * 6,
        out_specs=spec,
        out_shape=jax.ShapeDtypeStruct((l, n, lanes), F32),
        scratch_shapes=[pltpu.VMEM((n, n, lanes), F32)],
        compiler_params=_params("arbitrary"),
        name="rwkv_scan",
    )(r, w, kd, v, z, bb)


def _mix_out_kernel(rwkv, *refs):
    if rwkv:
        (ya_ref, yf_ref, yb_ref, bonus_ref, g_ref, lng_ref, lnb_ref, seg_ref,
         x_ref, gate_ref, w_ref, o_ref) = refs
        y = yf_ref[0] + yb_ref[0]
        inv_n = 1.0 / RWKV_HEAD
        mean = _dot_hi(y, seg_ref[...]) * inv_n
        cen = y - mean
        var = _dot_hi(cen * cen, seg_ref[...]) * inv_n
        yn = cen * lax.rsqrt(var + GN_EPS) * lng_ref[...] + lnb_ref[...]
        yr = (yn + bonus_ref[0]) * g_ref[0]
        a = jnp.concatenate([ya_ref[0], yr], axis=-1)
    else:
        a_ref, x_ref, gate_ref, w_ref, o_ref = refs
        a = a_ref[0]
    o_ref[0] = x_ref[0] + gate_ref[...] * _dot(a.astype(BF16), w_ref[...])


def mix_out(acts, consts, x, mod4, layer, w, tm, rwkv):
    b, l, d = x.shape
    in_specs = [pl.BlockSpec((1, tm, a.shape[2]), lambda bi, i: (bi, i, 0)) for a in acts]
    in_specs += [_const_spec(a.shape) for a in consts]
    in_specs += [pl.BlockSpec((1, tm, d), lambda bi, i: (bi, i, 0)), _mod_spec(d, layer, 2), _const_spec(w.shape)]
    return pl.pallas_call(
        functools.partial(_mix_out_kernel, rwkv),
        grid=(b, l // tm),
        in_specs=in_specs,
        out_specs=pl.BlockSpec((1, tm, d), lambda bi, i: (bi, i, 0)),
        out_shape=jax.ShapeDtypeStruct((b, l, d), F32),
        compiler_params=_params("parallel", "parallel"),
        name="mix_out",
    )(*acts, *consts, x, mod4, w)


def _ffn_kernel(final, x_ref, sh_ref, sc_ref, gate_ref, g_ref, w1_ref, w3_ref, w2_ref, fg_ref, o_ref):
    x = x_ref[0]
    h = _rms_mod(x, g_ref[...], sc_ref[...], sh_ref[...]).astype(BF16)
    a = _dot(h, w1_ref[...])
    u = (a * jax.nn.sigmoid(a)) * _dot(h, w3_ref[...])
    out = x + gate_ref[...] * _dot(u.astype(BF16), w2_ref[...])
    if final:
        ms = jnp.mean(out * out, axis=-1, keepdims=True)
        out = out * lax.rsqrt(ms + NORM_EPS) * fg_ref[...]
    o_ref[0] = out


def ffn(x, mod4, layer, gain, w1, w3, w2, final_gain, tm, final):
    b, l, d = x.shape
    return pl.pallas_call(
        functools.partial(_ffn_kernel, final),
        grid=(b, l // tm),
        in_specs=[pl.BlockSpec((1, tm, d), lambda bi, i: (bi, i, 0)),
                  _mod_spec(d, layer, 3), _mod_spec(d, layer, 4), _mod_spec(d, layer, 5),
                  _const_spec((1, d)), _const_spec(w1.shape), _const_spec(w3.shape), _const_spec(w2.shape),
                  _const_spec((1, d))],
        out_specs=pl.BlockSpec((1, tm, d), lambda bi, i: (bi, i, 0)),
        out_shape=jax.ShapeDtypeStruct((b, l, d), F32),
        compiler_params=_params("parallel", "parallel"),
        name="ffn",
    )(x, mod4, mod4, mod4, gain.reshape(1, d), w1, w3, w2, final_gain.reshape(1, d))


def _short_conv_kernel(x_ref, xp_ref, xn_ref, w_ref, b_ref, o_ref):
    cur = x_ref[0]
    prev, nxt = _neighbours(cur, xp_ref, xn_ref, 2)
    o_ref[0] = prev * w_ref[0:1, :] + cur * w_ref[1:2, :] + nxt * w_ref[2:3, :] + b_ref[...]


def short_conv(p, conv_w, conv_b, tt, tw):
    b, l, w = p.shape
    return pl.pallas_call(
        _short_conv_kernel,
        grid=(b, w // tw, l // tt),
        in_specs=[pl.BlockSpec((1, tt, tw), lambda bi, j, i: (bi, i, j)),
                  pl.BlockSpec((1, SUBLANES, tw),
                               lambda bi, j, i: (bi, jnp.maximum(i * (tt // SUBLANES) - 1, 0), j)),
                  pl.BlockSpec((1, SUBLANES, tw),
                               lambda bi, j, i: (bi, jnp.minimum((i + 1) * (tt // SUBLANES), l // SUBLANES - 1), j)),
                  pl.BlockSpec((3, tw), lambda bi, j, i: (0, j)),
                  pl.BlockSpec((1, tw), lambda bi, j, i: (0, j))],
        out_specs=pl.BlockSpec((1, tt, tw), lambda bi, j, i: (bi, i, j)),
        out_shape=jax.ShapeDtypeStruct((b, l, w), F32),
        compiler_params=_params("parallel", "parallel", "parallel"),
        name="short_conv",
    )(p, p, p, conv_w, conv_b.reshape(1, w))


def _hy_filter_kernel(d_tiles, feats_ref, w1_ref, b1_ref, w2_ref, b2_ref, w3_ref, b3_ref, fr_ref,
                      wout_ref, win_ref, o_ref, hid_s):
    j = pl.program_id(0)

    @pl.when(j == 0)
    def _():
        fr = fr_ref[...]
        hid = jnp.sin(fr * (_dot_hi(feats_ref[...], w1_ref[...]) + b1_ref[...]))
        hid = jnp.sin(fr * (_dot_hi(hid, w2_ref[...]) + b2_ref[...]))
        hid_s[...] = jnp.sin(fr * (_dot_hi(hid, w3_ref[...]) + b3_ref[...]))

    filt = _dot_hi(hid_s[...], wout_ref[...]) * win_ref[...]
    backward = (j // d_tiles) % 2 == 1
    row = lax.broadcasted_iota(jnp.int32, filt.shape, 0)
    o_ref[...] = jnp.where(jnp.logical_and(backward, row == 0), 0.0, filt)


def hy_filter(feats, w1, b1, w2, b2, w3, b3, fr, w_out, window, tn):
    l = feats.shape[0]
    hf = w2.shape[0]
    n = w_out.shape[1]
    d = window.shape[1]
    d_tiles = d // tn
    consts = [feats, w1, b1.reshape(1, hf), w2, b2.reshape(1, hf), w3, b3.reshape(1, hf), fr.reshape(1, hf)]
    return pl.pallas_call(
        functools.partial(_hy_filter_kernel, d_tiles),
        grid=(n // tn,),
        in_specs=[_const_spec(a.shape) for a in consts]
        + [pl.BlockSpec((hf, tn), lambda j: (0, j)), pl.BlockSpec((l, tn), lambda j: (0, j % d_tiles))],
        out_specs=pl.BlockSpec((l, tn), lambda j: (0, j)),
        out_shape=jax.ShapeDtypeStruct((l, n), F32),
        scratch_shapes=[pltpu.VMEM((l, hf), F32)],
        compiler_params=_params("arbitrary"),
        name="hy_filter",
    )(*consts, w_out, window)


def _dft_matrices(l):
    n = 2 * l
    ft = min(FREQ_TILE, l)
    k = jnp.arange(l, dtype=jnp.int32)
    t = jnp.arange(l, dtype=jnp.int32)
    ang = ((k[:, None] * t[None, :]) % n).astype(F32) * (2.0 * math.pi / n)
    cos = jnp.cos(ang)
    sin = jnp.sin(ang)
    alt = jnp.where(t % 2 == 0, 1.0, -1.0).astype(F32)
    first = (k == 0)[:, None]
    f_re = cos
    f_im = jnp.where(first, alt[None, :], -sin)
    fwd = jnp.concatenate([f_re.reshape(l // ft, ft, l), f_im.reshape(l // ft, ft, l)], axis=1).reshape(n, l)
    g_re = jnp.where(first, 1.0 / n, (2.0 / n) * cos)
    g_im = jnp.where(first, alt[None, :] / n, (-2.0 / n) * sin)
    inv = jnp.concatenate([g_re.reshape(l // ft, ft, l), g_im.reshape(l // ft, ft, l)], axis=1).reshape(n, l).T
    return _split_bf16(fwd), _split_bf16(inv)


def _dft3(m_hi, m_lo, x_hi, x_lo):
    return _dot(m_hi, x_hi) + (_dot(m_lo, x_hi) + _dot(m_hi, x_lo))


def _dft_spec_kernel(fh_ref, fl_ref, tf_ref, tb_ref, o_ref):
    i = pl.program_id(1)
    af = _dft3(fh_ref[...], fl_ref[...], *_split_bf16(tf_ref[...]))
    ab = _dft3(fh_ref[...], fl_ref[...], *_split_bf16(tb_ref[...]))
    ft = af.shape[0] // 2
    row = lax.broadcasted_iota(jnp.int32, (ft, af.shape[1]), 0)
    nyq = jnp.logical_and(i == 0, row == 0)
    o_ref[0:ft, :] = af[:ft] + ab[:ft]
    o_ref[ft:, :] = jnp.where(nyq, af[ft:] + ab[ft:], af[ft:] - ab[ft:])


def dft_spec(fwd, taps, d, tn):
    fh, fl = fwd
    n, l = fh.shape
    ft2 = 2 * min(FREQ_TILE, l)
    d_tiles = d // tn
    orders = taps.shape[1] // (2 * d)
    return pl.pallas_call(
        _dft_spec_kernel,
        grid=(orders * d_tiles, n // ft2),
        in_specs=[pl.BlockSpec((ft2, l), lambda j, i: (i, 0)), pl.BlockSpec((ft2, l), lambda j, i: (i, 0)),
                  pl.BlockSpec((l, tn), lambda j, i: (0, (j // d_tiles) * 2 * d_tiles + j % d_tiles)),
                  pl.BlockSpec((l, tn), lambda j, i: (0, (j // d_tiles) * 2 * d_tiles + d_tiles + j % d_tiles))],
        out_specs=pl.BlockSpec((ft2, tn), lambda j, i: (i, j)),
        out_shape=jax.ShapeDtypeStruct((n, orders * d), F32),
        compiler_params=_params("parallel", "parallel"),
        name="dft_spec",
    )(fh, fl, taps, taps)


def _dft_mul_kernel(fh_ref, fl_ref, u_ref, h_ref, o_ref, uh_s, ul_s):
    i = pl.program_id(2)

    @pl.when(i == 0)
    def _():
        uh, ul = _split_bf16(u_ref[0])
        uh_s[...] = uh
        ul_s[...] = ul

    acc = _dft3(fh_ref[...], fl_ref[...], uh_s[...], ul_s[...])
    ft = acc.shape[0] // 2
    ur, ui = acc[:ft], acc[ft:]
    hr, hi = h_ref[0:ft, :], h_ref[ft:, :]
    row = lax.broadcasted_iota(jnp.int32, ur.shape, 0)
    nyq = jnp.logical_and(i == 0, row == 0)
    o_ref[0, 0:ft, :] = ur * hr - jnp.where(nyq, 0.0, ui * hi)
    o_ref[0, ft:, :] = jnp.where(nyq, ui * hi, ur * hi + ui * hr)


def dft_mul(fwd, u, u_col, spec, spec_col, d, tn):
    fh, fl = fwd
    n, l = fh.shape
    b = u.shape[0]
    ft2 = 2 * min(FREQ_TILE, l)
    d_tiles = d // tn
    return pl.pallas_call(
        _dft_mul_kernel,
        grid=(b, d_tiles, n // ft2),
        in_specs=[pl.BlockSpec((ft2, l), lambda bi, j, i: (i, 0)), pl.BlockSpec((ft2, l), lambda bi, j, i: (i, 0)),
                  pl.BlockSpec((1, l, tn), lambda bi, j, i: (bi, 0, u_col * d_tiles + j)),
                  pl.BlockSpec((ft2, tn), lambda bi, j, i: (i, spec_col * d_tiles + j))],
        out_specs=pl.BlockSpec((1, ft2, tn), lambda bi, j, i: (bi, i, j)),
        out_shape=jax.ShapeDtypeStruct((b, n, d), F32),
        scratch_shapes=[pltpu.VMEM((l, tn), BF16), pltpu.VMEM((l, tn), BF16)],
        compiler_params=_params("parallel", "parallel", "arbitrary"),
        name="dft_mul",
    )(fh, fl, u, spec)


def _dft_inv_kernel(gh_ref, gl_ref, y_ref, u_ref, gate_ref, skip_ref, o_ref, yh_s, yl_s):
    @pl.when(pl.program_id(2) == 0)
    def _():
        yh, yl = _split_bf16(y_ref[0])
        yh_s[...] = yh
        yl_s[...] = yl

    conv = _dft3(gh_ref[...], gl_ref[...], yh_s[...], yl_s[...])
    o_ref[0] = gate_ref[0] * (conv + u_ref[0] * skip_ref[...])


def dft_inv(inv, y, u, u_col, gate, gate_col, skip, tm, tn):
    gh, gl = inv
    l, n = gh.shape
    b, _, d = y.shape
    d_tiles = d // tn
    return pl.pallas_call(
        _dft_inv_kernel,
        grid=(b, d_tiles, l // tm),
        in_specs=[pl.BlockSpec((tm, n), lambda bi, j, i: (i, 0)), pl.BlockSpec((tm, n), lambda bi, j, i: (i, 0)),
                  pl.BlockSpec((1, n, tn), lambda bi, j, i: (bi, 0, j)),
                  pl.BlockSpec((1, tm, tn), lambda bi, j, i: (bi, i, u_col * d_tiles + j)),
                  pl.BlockSpec((1, tm, tn), lambda bi, j, i: (bi, i, gate_col * d_tiles + j)),
                  pl.BlockSpec((1, tn), lambda bi, j, i: (0, j))],
        out_specs=pl.BlockSpec((1, tm, tn), lambda bi, j, i: (bi, i, j)),
        out_shape=jax.ShapeDtypeStruct((b, l, d), F32),
        scratch_shapes=[pltpu.VMEM((n, tn), BF16), pltpu.VMEM((n, tn), BF16)],
        compiler_params=_params("parallel", "parallel", "arbitrary"),
        name="dft_inv",
    )(gh, gl, y, u, gate, skip.reshape(1, d))


def _rope_tables(l):
    rows = l // GRID_W
    row = jnp.repeat(jnp.arange(rows), GRID_W).astype(F32)
    col = jnp.tile(jnp.arange(GRID_W), rows).astype(F32)
    half = HEAD_DIM // 2
    inv_freq = ROPE_THETA ** (-jnp.arange(0, half, 2, dtype=F32) / half)
    ang = jnp.concatenate([row[:, None] * inv_freq, col[:, None] * inv_freq], axis=-1)
    cos, sin = jnp.cos(ang), jnp.sin(ang)
    return jnp.concatenate([cos, cos], axis=-1), jnp.concatenate([sin, sin], axis=-1)


def _hyena_tables(l, d):
    t = jnp.linspace(0.0, 1.0, l, dtype=F32)[:, None]
    omega = (2.0 * math.pi / l) * jnp.arange(l, dtype=F32)[:, None]
    bands = jnp.linspace(1e-4, HYENA_BANDS - 1, HYENA_BANDS, dtype=F32)[None, :]
    feats = jnp.concatenate([t, jnp.cos(bands * omega), -jnp.sin(bands * omega)], axis=-1)
    min_decay = math.log(HYENA_TARGET) / HYENA_SLOW_DECAY
    max_decay = math.log(HYENA_TARGET) / HYENA_FAST_DECAY
    deltas = jnp.abs(jnp.linspace(min_decay, max_decay, d, dtype=F32))
    window = jnp.exp(-t * deltas)
    return feats, window


def _scan_layout(fwd, bwd):
    b, l, _ = fwd.shape

    def one(x):
        return x.reshape(b, l, RWKV_HEADS, RWKV_HEAD).transpose(1, 3, 0, 2).reshape(l, RWKV_HEAD, b * RWKV_HEADS)

    return jnp.concatenate([one(fwd), jnp.flip(one(bwd), axis=0)], axis=-1)


def _from_scan_layout(y, b):
    l = y.shape[0]
    half = b * RWKV_HEADS

    def one(x):
        return x.reshape(l, RWKV_HEAD, b, RWKV_HEADS).transpose(2, 0, 3, 1).reshape(b, l, RWKV_WIDTH)

    return one(y[:, :, :half]), one(jnp.flip(y[:, :, half:], axis=0))


def kernel(x, c, mix_w_in, mix_w_out, attn_q_norm, attn_k_norm, rwkv_mu, rwkv_w0, rwkv_w_up, rwkv_a0, rwkv_a_up, rwkv_g_up, rwkv_k_k, rwkv_k_a, rwkv_r_k, rwkv_ln_g, rwkv_ln_b, hy_w_in, hy_conv_w, hy_conv_b, hy_f_w1, hy_f_b1, hy_f_w2, hy_f_b2, hy_f_w3, hy_f_b3, hy_sin_freq, hy_f_out, hy_skip, hy_w_out, ada_w, ada_b, norm_mix, norm_ffn, ffn_w1, ffn_w3, ffn_w2, final_norm):
    b, l, d = x.shape
    depth = ada_w.shape[0]
    tm = min(256, l)
    c_w = RWKV_WIDTH

    mod4 = ada_mod(c, ada_w, ada_b).reshape(depth, b, 1, 6 * d)

    perm = jnp.concatenate([jnp.arange(0, HEAD_DIM, 2), jnp.arange(1, HEAD_DIM, 2)])
    w_in = mix_w_in[0]
    w_q = w_in[:, :ATTN_WIDTH].reshape(d, ATTN_HEADS, HEAD_DIM)[:, :, perm].reshape(d, ATTN_WIDTH)
    w_kv = w_in[:, ATTN_WIDTH:ATTN_WIDTH + 2 * KV_WIDTH]
    w_k = w_kv[:, :KV_WIDTH].reshape(d, ATTN_KV_HEADS, HEAD_DIM)[:, :, perm].reshape(d, KV_WIDTH)
    w_kv = jnp.concatenate([w_k, w_kv[:, KV_WIDTH:]], axis=1)
    w_rw = jnp.pad(w_in[:, ATTN_WIDTH + 2 * KV_WIDTH:], ((0, 0), (0, RWKV_IN_PAD - RWKV_IN)))
    q, kv, rw = proj(x, mod4, 0, norm_mix[0], [w_q.astype(BF16), w_kv.astype(BF16), w_rw.astype(BF16)], tm)

    cos, sin = _rope_tables(l)
    y_attn = attn(q, kv, cos, sin, attn_q_norm[0][perm].reshape(1, HEAD_DIM),
                  attn_k_norm[0][perm].reshape(1, HEAD_DIM), tm)

    seg = jnp.kron(jnp.eye(RWKV_HEADS, dtype=F32), jnp.ones((RWKV_HEAD, RWKV_HEAD), F32))
    r, z, v, w, kd, bb, bonus, g = rwkv_prep(
        rw, jnp.pad(rwkv_mu[0], (0, RWKV_IN_PAD - RWKV_IN)).reshape(1, RWKV_IN_PAD),
        rwkv_w0[0].reshape(2, 1, c_w), rwkv_w_up[0], rwkv_a0[0].reshape(2, 1, c_w), rwkv_a_up[0],
        jnp.pad(rwkv_g_up[0], ((0, RWKV_IN_PAD - RWKV_IN), (0, 0))),
        rwkv_k_k[0].reshape(1, c_w), rwkv_k_a[0].reshape(1, c_w), rwkv_r_k[0].reshape(1, c_w), seg, tm)
    y_scan = rwkv_scan(_scan_layout(r, r), _scan_layout(w[0], w[1]), _scan_layout(kd[0], kd[1]),
                       _scan_layout(v, v), _scan_layout(z, z), _scan_layout(bb[0], bb[1]), min(32, l))
    y_f, y_b = _from_scan_layout(y_scan, b)
    x = mix_out([y_attn, y_f, y_b, bonus, g],
                [rwkv_ln_g[0].reshape(1, c_w), rwkv_ln_b[0].reshape(1, c_w), seg],
                x, mod4, 0, mix_w_out[0].astype(BF16), tm, True)
    x = ffn(x, mod4, 0, norm_ffn[0], ffn_w1[0].astype(BF16), ffn_w3[0].astype(BF16), ffn_w2[0].astype(BF16),
            final_norm, tm, False)

    (p3,) = proj(x, mod4, 1, norm_mix[1], [hy_w_in[0].astype(BF16)], tm)
    tn = min(512, d)
    p3 = short_conv(p3, hy_conv_w[0], hy_conv_b[0], tm, tn)
    feats, window = _hyena_tables(l, d)
    k_pad = 128
    feats = jnp.pad(feats, ((0, 0), (0, k_pad - feats.shape[1])))
    f_w1 = jnp.pad(hy_f_w1[0], ((0, k_pad - hy_f_w1.shape[1]), (0, 0)))
    taps = hy_filter(feats, f_w1, hy_f_b1[0], hy_f_w2[0], hy_f_b2[0], hy_f_w3[0], hy_f_b3[0],
                     hy_sin_freq[0], hy_f_out[0], window, tn)
    fwd, inv = _dft_matrices(l)
    spec = dft_spec(fwd, taps, d, tn)
    y1 = dft_mul(fwd, p3, 0, spec, 0, d, tn)
    z1 = dft_inv(inv, y1, p3, 0, p3, 1, hy_skip[0, 0], tm, tn)
    y2 = dft_mul(fwd, z1, 0, spec, 1, d, tn)
    z2 = dft_inv(inv, y2, z1, 0, p3, 2, hy_skip[0, 1], tm, tn)
    x = mix_out([z2], [], x, mod4, 1, hy_w_out[0].astype(BF16), tm, False)
    x = ffn(x, mod4, 1, norm_ffn[1], ffn_w1[1].astype(BF16), ffn_w3[1].astype(BF16), ffn_w2[1].astype(BF16),
            final_norm, tm, True)
    return x
```

```python
import functools
import math

import jax
import jax.numpy as jnp
from jax import lax
from jax.experimental import pallas as pl
from jax.experimental.pallas import tpu as pltpu

F32 = jnp.float32
BF16 = jnp.bfloat16
HIGHEST = lax.Precision.HIGHEST

GRID_W = 64
HEAD_DIM = 64
ATTN_HEADS = 8
ATTN_KV_HEADS = 2
ATTN_WIDTH = ATTN_HEADS * HEAD_DIM
KV_WIDTH = ATTN_KV_HEADS * HEAD_DIM
ROPE_THETA = 10000.0
RWKV_HEADS = 8
RWKV_HEAD = 64
RWKV_WIDTH = RWKV_HEADS * RWKV_HEAD
DECAY_LORA = 64
AAA_LORA = 64
GATE_LORA = 160
RWKV_IN = 3 * RWKV_WIDTH + 2 * DECAY_LORA + 2 * AAA_LORA + GATE_LORA
RWKV_IN_PAD = 2048
GN_EPS = 64e-5
HYENA_BANDS = 16
HYENA_TARGET = 1e-2
HYENA_FAST_DECAY = 0.3
HYENA_SLOW_DECAY = 1.5
NORM_EPS = 1e-6

V7X_VMEM_BYTES = 64 * 1024 * 1024
VMEM_LIMIT = V7X_VMEM_BYTES - 8 * 1024 * 1024
FREQ_TILE = 256
SUBLANES = 8


def _params(*sem):
    return pltpu.CompilerParams(dimension_semantics=sem, vmem_limit_bytes=VMEM_LIMIT)


def _const_spec(shape):
    zeros = (0,) * len(shape)
    return pl.BlockSpec(shape, lambda *_: zeros, pipeline_mode=pl.Buffered(1))


def _dot(a, b):
    return jnp.dot(a, b, preferred_element_type=F32)


def _dot_hi(a, b):
    return jnp.dot(a, b, precision=HIGHEST, preferred_element_type=F32)


def _split_bf16(x):
    hi = x.astype(BF16)
    lo = (x - hi.astype(F32)).astype(BF16)
    return hi, lo


def _rms_mod(x, gain, scale, shift):
    ms = jnp.mean(x * x, axis=-1, keepdims=True)
    return (x * lax.rsqrt(ms + NORM_EPS) * gain) * (1.0 + scale) + shift


def _ada_kernel(c_ref, w_ref, b_ref, o_ref):
    c = c_ref[...]
    cond = c * jax.nn.sigmoid(c)
    o_ref[...] = _dot_hi(cond, w_ref[...]) + b_ref[...]


def ada_mod(c, ada_w, ada_b):
    depth, d, n = ada_w.shape
    b = c.shape[0]
    tn = 1536
    return pl.pallas_call(
        _ada_kernel,
        grid=(depth, n // tn),
        in_specs=[pl.BlockSpec((b, d), lambda l, j: (0, 0)),
                  pl.BlockSpec((None, d, tn), lambda l, j: (l, 0, j)),
                  pl.BlockSpec((None, 1, tn), lambda l, j: (l, 0, j))],
        out_specs=pl.BlockSpec((None, b, tn), lambda l, j: (l, 0, j)),
        out_shape=jax.ShapeDtypeStruct((depth, b, n), F32),
        compiler_params=_params("parallel", "parallel"),
        name="ada_mod",
    )(c, ada_w, ada_b.reshape(depth, 1, n))


def _mod_spec(d, layer, chunk):
    return pl.BlockSpec((None, None, 1, d), lambda b, *_: (layer, b, 0, chunk))


def _proj_kernel(n_out, x_ref, sh_ref, sc_ref, g_ref, *refs):
    h = _rms_mod(x_ref[0], g_ref[...], sc_ref[...], sh_ref[...]).astype(BF16)
    for w_ref, o_ref in zip(refs[:n_out], refs[n_out:]):
        o_ref[0] = _dot(h, w_ref[...])


def proj(x, mod4, layer, gain, weights, tm):
    b, l, d = x.shape
    n_out = len(weights)
    return pl.pallas_call(
        functools.partial(_proj_kernel, n_out),
        grid=(b, l // tm),
        in_specs=[pl.BlockSpec((1, tm, d), lambda bi, i: (bi, i, 0)),
                  _mod_spec(d, layer, 0), _mod_spec(d, layer, 1),
                  _const_spec((1, d))] + [_const_spec(w.shape) for w in weights],
        out_specs=[pl.BlockSpec((1, tm, w.shape[1]), lambda bi, i: (bi, i, 0)) for w in weights],
        out_shape=[jax.ShapeDtypeStruct((b, l, w.shape[1]), F32) for w in weights],
        compiler_params=_params("parallel", "parallel"),
        name="proj",
    )(x, mod4, mod4, gain.reshape(1, d), *weights)


def _headnorm_rope(x, gain, cos, sin):
    ms = jnp.mean(x * x, axis=-1, keepdims=True)
    y = x * lax.rsqrt(ms + NORM_EPS) * gain
    half = HEAD_DIM // 2
    rot = jnp.concatenate([-y[:, half:], y[:, :half]], axis=-1)
    return y * cos + rot * sin


def _attn_kernel(q_ref, kv_ref, cq_ref, sq_ref, ck_ref, sk_ref, qn_ref, kn_ref, o_ref, k_s, v_s):
    @pl.when(pl.program_id(1) == 0)
    def _():
        kv = kv_ref[0]
        for hk in range(ATTN_KV_HEADS):
            k = kv[:, HEAD_DIM * hk:HEAD_DIM * (hk + 1)]
            k_s[hk] = _headnorm_rope(k, kn_ref[...], ck_ref[...], sk_ref[...]).astype(BF16)
            v_s[hk] = kv[:, KV_WIDTH + HEAD_DIM * hk:KV_WIDTH + HEAD_DIM * (hk + 1)].astype(BF16)

    q = q_ref[0]
    group = ATTN_HEADS // ATTN_KV_HEADS
    outs = []
    for h in range(ATTN_HEADS):
        qh = _headnorm_rope(q[:, HEAD_DIM * h:HEAD_DIM * (h + 1)], qn_ref[...], cq_ref[...], sq_ref[...])
        qh = (qh * (HEAD_DIM ** -0.5)).astype(BF16)
        s = lax.dot_general(qh, k_s[h // group], (((1,), (1,)), ((), ())), preferred_element_type=F32)
        m = jnp.max(s, axis=-1, keepdims=True)
        p = jnp.exp(s - m)
        denom = jnp.sum(p, axis=-1, keepdims=True)
        o = _dot(p.astype(BF16), v_s[h // group])
        outs.append(o / denom)
    o_ref[0] = jnp.concatenate(outs, axis=-1)


def attn(q, kv, cos, sin, qn, kn, tq):
    b, l, _ = q.shape
    return pl.pallas_call(
        _attn_kernel,
        grid=(b, l // tq),
        in_specs=[pl.BlockSpec((1, tq, ATTN_WIDTH), lambda bi, i: (bi, i, 0)),
                  pl.BlockSpec((1, l, 2 * KV_WIDTH), lambda bi, i: (bi, 0, 0)),
                  pl.BlockSpec((tq, HEAD_DIM), lambda bi, i: (i, 0)),
                  pl.BlockSpec((tq, HEAD_DIM), lambda bi, i: (i, 0)),
                  _const_spec((l, HEAD_DIM)), _const_spec((l, HEAD_DIM)),
                  _const_spec((1, HEAD_DIM)), _const_spec((1, HEAD_DIM))],
        out_specs=pl.BlockSpec((1, tq, ATTN_WIDTH), lambda bi, i: (bi, i, 0)),
        out_shape=jax.ShapeDtypeStruct((b, l, ATTN_WIDTH), F32),
        scratch_shapes=[pltpu.VMEM((ATTN_KV_HEADS, l, HEAD_DIM), BF16),
                        pltpu.VMEM((ATTN_KV_HEADS, l, HEAD_DIM), BF16)],
        compiler_params=_params("parallel", "arbitrary"),
        name="attn",
    )(q, kv, cos, sin, cos, sin, qn, kn)


def _halo_specs(tt, w, l):
    nb = l // SUBLANES
    per = tt // SUBLANES
    main = pl.BlockSpec((1, tt, w), lambda bi, i: (bi, i, 0))
    prev = pl.BlockSpec((1, SUBLANES, w), lambda bi, i: (bi, jnp.maximum(i * per - 1, 0), 0))
    nxt = pl.BlockSpec((1, SUBLANES, w), lambda bi, i: (bi, jnp.minimum((i + 1) * per, nb - 1), 0))
    return [main, prev, nxt]


def _neighbours(cur, prow, nrow):
    tt = cur.shape[0]
    row = lax.broadcasted_iota(jnp.int32, cur.shape, 0)
    prev = jnp.where(row == 0, prow, pltpu.roll(cur, 1, 0))
    nxt = jnp.where(row == tt - 1, nrow, pltpu.roll(cur, tt - 1, 0))
    return prev, nxt


def _halo_rows(xp_ref, xn_ref, axis):
    i = pl.program_id(axis)
    last = pl.num_programs(axis) - 1
    prow = jnp.where(i > 0, xp_ref[0, SUBLANES - 1:SUBLANES, :], 0.0)
    nrow = jnp.where(i < last, xn_ref[0, 0:1, :], 0.0)
    return prow, nrow


def _rwkv_prep_kernel(x_ref, xp_ref, xn_ref, mu_ref, w0_ref, wup_ref, a0_ref, aup_ref, gup_ref,
                      kk_ref, ka_ref, rk_ref, seg_ref,
                      r_o, z_o, v_o, w_o, kd_o, b_o, bonus_o, g_o):
    cur = x_ref[0]
    prev, nxt = _neighbours(cur, *_halo_rows(xp_ref, xn_ref, 1))
    ps = cur + (0.5 * (prev + nxt) - cur) * mu_ref[...]
    c = RWKV_WIDTH
    r = ps[:, 0:c]
    k = ps[:, c:2 * c]
    v = ps[:, 2 * c:3 * c]
    o_a = 3 * c + 2 * DECAY_LORA
    o_g = o_a + 2 * AAA_LORA
    kk = k * kk_ref[...]
    kk = kk * lax.rsqrt(jnp.maximum(_dot_hi(kk * kk, seg_ref[...]), 1e-24))
    r_o[0] = r
    v_o[0] = v
    z_o[0] = -kk
    for d in range(2):
        w_lo = ps[:, 3 * c + DECAY_LORA * d:3 * c + DECAY_LORA * (d + 1)]
        a_lo = ps[:, o_a + AAA_LORA * d:o_a + AAA_LORA * (d + 1)]
        decay = jnp.exp(-math.exp(-0.5) * jax.nn.sigmoid(w0_ref[d] + _dot_hi(jnp.tanh(w_lo), wup_ref[d])))
        a = jax.nn.sigmoid(a0_ref[d] + _dot_hi(a_lo, aup_ref[d]))
        w_o[d, 0] = decay
        kd_o[d, 0] = k * (1.0 + (a - 1.0) * ka_ref[...])
        b_o[d, 0] = kk * a
    bonus_o[0] = _dot_hi(r * k * rk_ref[...], seg_ref[...]) * v
    g_o[0] = _dot_hi(jax.nn.sigmoid(ps[:, o_g:RWKV_IN_PAD]), gup_ref[...])


def rwkv_prep(rw, mu, w0, w_up, a0, a_up, g_up, k_k, k_a, r_k, seg, tt):
    b, l, w = rw.shape
    c = RWKV_WIDTH
    one = jax.ShapeDtypeStruct((b, l, c), F32)
    two = jax.ShapeDtypeStruct((2, b, l, c), F32)
    spec1 = pl.BlockSpec((1, tt, c), lambda bi, i: (bi, i, 0))
    spec2 = pl.BlockSpec((2, 1, tt, c), lambda bi, i: (0, bi, i, 0))
    consts = [mu, w0, w_up, a0, a_up, g_up, k_k, k_a, r_k, seg]
    return pl.pallas_call(
        _rwkv_prep_kernel,
        grid=(b, l // tt),
        in_specs=_halo_specs(tt, w, l) + [_const_spec(a.shape) for a in consts],
        out_specs=[spec1, spec1, spec1, spec2, spec2, spec2, spec1, spec1],
        out_shape=[one, one, one, two, two, two, one, one],
        compiler_params=_params("parallel", "parallel"),
        name="rwkv_prep",
    )(rw, rw, rw, *consts)


SCAN_ROWS = 32
_R, _W, _KD, _V, _Z, _B = range(6)


def _rwkv_scan_kernel(*refs):
    ins, (yf_ref, yb_ref, s_ref, m_ref, sz_ref) = refs[:12], refs[12:]
    tb, n, lanes = ins[0].shape
    fwd_lane = lax.broadcasted_iota(jnp.int32, (n, lanes), 1) < lanes // 2

    def stage(which, t):
        m_ref[which] = jnp.where(fwd_lane, ins[2 * which][t], ins[2 * which + 1][tb - 1 - t])

    @pl.when(pl.program_id(0) == 0)
    def _():
        s_ref[...] = jnp.zeros_like(s_ref)

    halves = [slice(h * SCAN_ROWS, (h + 1) * SCAN_ROWS) for h in range(n // SCAN_ROWS)]

    stage(_Z, 0)
    for rows in halves:
        acc = s_ref[0, rows, :] * m_ref[_Z, 0:1, :]
        for k in range(1, n):
            acc = acc + s_ref[k, rows, :] * m_ref[_Z, k:k + 1, :]
        sz_ref[rows, :] = acc

    def step(t, carry):
        for which in (_R, _W, _KD, _V, _B):
            stage(which, t)
        stage(_Z, jnp.minimum(t + 1, tb - 1))
        for rows in halves:
            sz = sz_ref[rows, :]
            vt = m_ref[_V, rows, :]
            y = None
            sz_next = None
            for k in range(n):
                sk = (s_ref[k, rows, :] * m_ref[_W, k:k + 1, :] + sz * m_ref[_B, k:k + 1, :]
                      + vt * m_ref[_KD, k:k + 1, :])
                s_ref[k, rows, :] = sk
                yk = sk * m_ref[_R, k:k + 1, :]
                zk = sk * m_ref[_Z, k:k + 1, :]
                y = yk if y is None else y + yk
                sz_next = zk if sz_next is None else sz_next + zk
            sz_ref[rows, :] = sz_next
            yf_ref[t, rows, :] = y
            yb_ref[tb - 1 - t, rows, :] = y
        return carry

    lax.fori_loop(0, tb, step, 0)


def rwkv_scan(arrays, tb):
    l, n, lanes = arrays[0].shape
    nt = l // tb
    fwd = pl.BlockSpec((tb, n, lanes), lambda i: (i, 0, 0))
    rev = pl.BlockSpec((tb, n, lanes), lambda i: (nt - 1 - i, 0, 0))
    out = jax.ShapeDtypeStruct((l, n, lanes), F32)
    return pl.pallas_call(
        _rwkv_scan_kernel,
        grid=(nt,),
        in_specs=[fwd, rev] * 6,
        out_specs=[fwd, rev],
        out_shape=[out, out],
        scratch_shapes=[pltpu.VMEM((n, n, lanes), F32), pltpu.VMEM((6, n, lanes), F32),
                        pltpu.VMEM((n, lanes), F32)],
        compiler_params=_params("arbitrary"),
        name="rwkv_scan",
    )(*[a for a in arrays for _ in range(2)])


def _mix_out_kernel(rwkv, *refs):
    if rwkv:
        (ya_ref, yf_ref, yb_ref, bonus_ref, g_ref, lng_ref, lnb_ref, seg_ref,
         x_ref, gate_ref, w_ref, o_ref) = refs
        y = yf_ref[0] + yb_ref[0]
        inv_n = 1.0 / RWKV_HEAD
        mean = _dot_hi(y, seg_ref[...]) * inv_n
        cen = y - mean
        var = _dot_hi(cen * cen, seg_ref[...]) * inv_n
        yn = cen * lax.rsqrt(var + GN_EPS) * lng_ref[...] + lnb_ref[...]
        yr = (yn + bonus_ref[0]) * g_ref[0]
        a = jnp.concatenate([ya_ref[0], yr], axis=-1)
    else:
        a_ref, x_ref, gate_ref, w_ref, o_ref = refs
        a = a_ref[0]
    o_ref[0] = x_ref[0] + gate_ref[...] * _dot(a.astype(BF16), w_ref[...])


def mix_out(acts, consts, x, mod4, layer, w, tm, rwkv):
    b, l, d = x.shape
    in_specs = [pl.BlockSpec((1, tm, a.shape[2]), lambda bi, i: (bi, i, 0)) for a in acts]
    in_specs += [_const_spec(a.shape) for a in consts]
    in_specs += [pl.BlockSpec((1, tm, d), lambda bi, i: (bi, i, 0)), _mod_spec(d, layer, 2), _const_spec(w.shape)]
    return pl.pallas_call(
        functools.partial(_mix_out_kernel, rwkv),
        grid=(b, l // tm),
        in_specs=in_specs,
        out_specs=pl.BlockSpec((1, tm, d), lambda bi, i: (bi, i, 0)),
        out_shape=jax.ShapeDtypeStruct((b, l, d), F32),
        compiler_params=_params("parallel", "parallel"),
        name="mix_out",
    )(*acts, *consts, x, mod4, w)


def _ffn_kernel(final, x_ref, sh_ref, sc_ref, gate_ref, g_ref, w1_ref, w3_ref, w2_ref, fg_ref, o_ref):
    x = x_ref[0]
    h = _rms_mod(x, g_ref[...], sc_ref[...], sh_ref[...]).astype(BF16)
    a = _dot(h, w1_ref[...])
    u = (a * jax.nn.sigmoid(a)) * _dot(h, w3_ref[...])
    out = x + gate_ref[...] * _dot(u.astype(BF16), w2_ref[...])
    if final:
        ms = jnp.mean(out * out, axis=-1, keepdims=True)
        out = out * lax.rsqrt(ms + NORM_EPS) * fg_ref[...]
    o_ref[0] = out


def ffn(x, mod4, layer, gain, w1, w3, w2, final_gain, tm, final):
    b, l, d = x.shape
    return pl.pallas_call(
        functools.partial(_ffn_kernel, final),
        grid=(b, l // tm),
        in_specs=[pl.BlockSpec((1, tm, d), lambda bi, i: (bi, i, 0)),
                  _mod_spec(d, layer, 3), _mod_spec(d, layer, 4), _mod_spec(d, layer, 5),
                  _const_spec((1, d)), _const_spec(w1.shape), _const_spec(w3.shape), _const_spec(w2.shape),
                  _const_spec((1, d))],
        out_specs=pl.BlockSpec((1, tm, d), lambda bi, i: (bi, i, 0)),
        out_shape=jax.ShapeDtypeStruct((b, l, d), F32),
        compiler_params=_params("parallel", "parallel"),
        name="ffn",
    )(x, mod4, mod4, mod4, gain.reshape(1, d), w1, w3, w2, final_gain.reshape(1, d))


def _short_conv_kernel(x_ref, w_ref, b_ref, o_ref):
    cur = x_ref[0]
    prev, nxt = _neighbours(cur, 0.0, 0.0)
    o_ref[0] = prev * w_ref[0:1, :] + cur * w_ref[1:2, :] + nxt * w_ref[2:3, :] + b_ref[...]


def short_conv(p, conv_w, conv_b, tw):
    b, l, w = p.shape
    return pl.pallas_call(
        _short_conv_kernel,
        grid=(b, w // tw),
        in_specs=[pl.BlockSpec((1, l, tw), lambda bi, j: (bi, 0, j)),
                  pl.BlockSpec((3, tw), lambda bi, j: (0, j)),
                  pl.BlockSpec((1, tw), lambda bi, j: (0, j))],
        out_specs=pl.BlockSpec((1, l, tw), lambda bi, j: (bi, 0, j)),
        out_shape=jax.ShapeDtypeStruct((b, l, w), F32),
        compiler_params=_params("parallel", "parallel"),
        name="short_conv",
    )(p, conv_w, conv_b.reshape(1, w))


def _hy_filter_kernel(d_tiles, feats_ref, w1_ref, b1_ref, w2_ref, b2_ref, w3_ref, b3_ref, fr_ref,
                      wout_ref, win_ref, o_ref, hid_s):
    j = pl.program_id(0)

    @pl.when(j == 0)
    def _():
        fr = fr_ref[...]
        hid = jnp.sin(fr * (_dot_hi(feats_ref[...], w1_ref[...]) + b1_ref[...]))
        hid = jnp.sin(fr * (_dot_hi(hid, w2_ref[...]) + b2_ref[...]))
        hid_s[...] = jnp.sin(fr * (_dot_hi(hid, w3_ref[...]) + b3_ref[...]))

    filt = _dot_hi(hid_s[...], wout_ref[...]) * win_ref[...]
    backward = (j // d_tiles) % 2 == 1
    row = lax.broadcasted_iota(jnp.int32, filt.shape, 0)
    o_ref[...] = jnp.where(jnp.logical_and(backward, row == 0), 0.0, filt)


def hy_filter(feats, w1, b1, w2, b2, w3, b3, fr, w_out, window, tn):
    l = feats.shape[0]
    hf = w2.shape[0]
    n = w_out.shape[1]
    d = window.shape[1]
    d_tiles = d // tn
    consts = [feats, w1, b1.reshape(1, hf), w2, b2.reshape(1, hf), w3, b3.reshape(1, hf), fr.reshape(1, hf)]
    return pl.pallas_call(
        functools.partial(_hy_filter_kernel, d_tiles),
        grid=(n // tn,),
        in_specs=[_const_spec(a.shape) for a in consts]
        + [pl.BlockSpec((hf, tn), lambda j: (0, j)), pl.BlockSpec((l, tn), lambda j: (0, j % d_tiles))],
        out_specs=pl.BlockSpec((l, tn), lambda j: (0, j)),
        out_shape=jax.ShapeDtypeStruct((l, n), F32),
        scratch_shapes=[pltpu.VMEM((l, hf), F32)],
        compiler_params=_params("arbitrary"),
        name="hy_filter",
    )(*consts, w_out, window)


def _dft_matrices(l):
    n = 2 * l
    ft = min(FREQ_TILE, l)
    lane = min(128, l)
    k = jnp.arange(l, dtype=jnp.int32)[:, None]
    t = jnp.arange(l, dtype=jnp.int32)
    a = ((k * jnp.arange(lane, dtype=jnp.int32)[None, :]) % n).astype(F32) * (2.0 * math.pi / n)
    c = ((k * (lane * jnp.arange(l // lane, dtype=jnp.int32))[None, :]) % n).astype(F32) * (2.0 * math.pi / n)
    ca, sa, cc, sc = jnp.cos(a)[:, None, :], jnp.sin(a)[:, None, :], jnp.cos(c)[:, :, None], jnp.sin(c)[:, :, None]
    cos = (ca * cc - sa * sc).reshape(l, l)
    sin = (sa * cc + ca * sc).reshape(l, l)
    alt = jnp.where(t % 2 == 0, 1.0, -1.0).astype(F32)
    first = k == 0
    f_re = cos
    f_im = jnp.where(first, alt[None, :], -sin)
    fwd = jnp.concatenate([f_re.reshape(l // ft, ft, l), f_im.reshape(l // ft, ft, l)], axis=1).reshape(n, l)
    g_re = jnp.where(first, 1.0 / n, (2.0 / n) * cos)
    g_im = jnp.where(first, alt[None, :] / n, (-2.0 / n) * sin)
    inv = jnp.concatenate([g_re.reshape(l // ft, ft, l), g_im.reshape(l // ft, ft, l)], axis=1).reshape(n, l).T
    return _split_bf16(fwd), inv.astype(BF16)


def _dft3(m_hi, m_lo, x_hi, x_lo):
    return _dot(m_hi, x_hi) + (_dot(m_lo, x_hi) + _dot(m_hi, x_lo))


def _dft_spec_kernel(fh_ref, fl_ref, tf_ref, tb_ref, o_ref):
    i = pl.program_id(1)
    af = _dft3(fh_ref[...], fl_ref[...], *_split_bf16(tf_ref[...]))
    ab = _dft3(fh_ref[...], fl_ref[...], *_split_bf16(tb_ref[...]))
    ft = af.shape[0] // 2
    row = lax.broadcasted_iota(jnp.int32, (ft, af.shape[1]), 0)
    nyq = jnp.logical_and(i == 0, row == 0)
    o_ref[0:ft, :] = af[:ft] + ab[:ft]
    o_ref[ft:, :] = jnp.where(nyq, af[ft:] + ab[ft:], af[ft:] - ab[ft:])


def dft_spec(fwd, taps, d, tn):
    fh, fl = fwd
    n, l = fh.shape
    ft2 = 2 * min(FREQ_TILE, l)
    d_tiles = d // tn
    orders = taps.shape[1] // (2 * d)
    return pl.pallas_call(
        _dft_spec_kernel,
        grid=(orders * d_tiles, n // ft2),
        in_specs=[pl.BlockSpec((ft2, l), lambda j, i: (i, 0)), pl.BlockSpec((ft2, l), lambda j, i: (i, 0)),
                  pl.BlockSpec((l, tn), lambda j, i: (0, (j // d_tiles) * 2 * d_tiles + j % d_tiles)),
                  pl.BlockSpec((l, tn), lambda j, i: (0, (j // d_tiles) * 2 * d_tiles + d_tiles + j % d_tiles))],
        out_specs=pl.BlockSpec((ft2, tn), lambda j, i: (i, j)),
        out_shape=jax.ShapeDtypeStruct((n, orders * d), F32),
        compiler_params=_params("parallel", "parallel"),
        name="dft_spec",
    )(fh, fl, taps, taps)


def _dft_mul_kernel(f_ref, u_ref, h_ref, o_ref, u_s):
    i = pl.program_id(2)

    @pl.when(i == 0)
    def _():
        u_s[...] = u_ref[0].astype(BF16)

    acc = _dot(f_ref[...], u_s[...])
    ft = acc.shape[0] // 2
    ur, ui = acc[:ft], acc[ft:]
    hr, hi = h_ref[0:ft, :], h_ref[ft:, :]
    row = lax.broadcasted_iota(jnp.int32, ur.shape, 0)
    nyq = jnp.logical_and(i == 0, row == 0)
    o_ref[0, 0:ft, :] = (ur * hr - jnp.where(nyq, 0.0, ui * hi)).astype(BF16)
    o_ref[0, ft:, :] = jnp.where(nyq, ui * hi, ur * hi + ui * hr).astype(BF16)


def dft_mul(fwd, u, u_col, spec, spec_col, d, tn):
    n, l = fwd.shape
    b = u.shape[0]
    ft2 = 2 * min(FREQ_TILE, l)
    d_tiles = d // tn
    return pl.pallas_call(
        _dft_mul_kernel,
        grid=(b, d_tiles, n // ft2),
        in_specs=[pl.BlockSpec((ft2, l), lambda bi, j, i: (i, 0)),
                  pl.BlockSpec((1, l, tn), lambda bi, j, i: (bi, 0, u_col * d_tiles + j)),
                  pl.BlockSpec((ft2, tn), lambda bi, j, i: (i, spec_col * d_tiles + j))],
        out_specs=pl.BlockSpec((1, ft2, tn), lambda bi, j, i: (bi, i, j)),
        out_shape=jax.ShapeDtypeStruct((b, n, d), BF16),
        scratch_shapes=[pltpu.VMEM((l, tn), BF16)],
        compiler_params=_params("parallel", "parallel", "arbitrary"),
        name="dft_mul",
    )(fwd, u, spec)


def _dft_inv_kernel(g_ref, y_ref, u_ref, gate_ref, skip_ref, o_ref):
    conv = _dot(g_ref[...], y_ref[0])
    o_ref[0] = gate_ref[0] * (conv + u_ref[0] * skip_ref[...])


def dft_inv(inv, y, u, u_col, gate, gate_col, skip, tm, tn):
    l, n = inv.shape
    b, _, d = y.shape
    d_tiles = d // tn
    return pl.pallas_call(
        _dft_inv_kernel,
        grid=(b, d_tiles, l // tm),
        in_specs=[pl.BlockSpec((tm, n), lambda bi, j, i: (i, 0)),
                  pl.BlockSpec((1, n, tn), lambda bi, j, i: (bi, 0, j)),
                  pl.BlockSpec((1, tm, tn), lambda bi, j, i: (bi, i, u_col * d_tiles + j)),
                  pl.BlockSpec((1, tm, tn), lambda bi, j, i: (bi, i, gate_col * d_tiles + j)),
                  pl.BlockSpec((1, tn), lambda bi, j, i: (0, j))],
        out_specs=pl.BlockSpec((1, tm, tn), lambda bi, j, i: (bi, i, j)),
        out_shape=jax.ShapeDtypeStruct((b, l, d), F32),
        compiler_params=_params("parallel", "parallel", "parallel"),
        name="dft_inv",
    )(inv, y, u, gate, skip.reshape(1, d))


def _rope_tables(l):
    rows = l // GRID_W
    row = jnp.repeat(jnp.arange(rows), GRID_W).astype(F32)
    col = jnp.tile(jnp.arange(GRID_W), rows).astype(F32)
    half = HEAD_DIM // 2
    inv_freq = ROPE_THETA ** (-jnp.arange(0, half, 2, dtype=F32) / half)
    ang = jnp.concatenate([row[:, None] * inv_freq, col[:, None] * inv_freq], axis=-1)
    cos, sin = jnp.cos(ang), jnp.sin(ang)
    return jnp.concatenate([cos, cos], axis=-1), jnp.concatenate([sin, sin], axis=-1)


def _hyena_tables(l, d):
    t = jnp.linspace(0.0, 1.0, l, dtype=F32)[:, None]
    omega = (2.0 * math.pi / l) * jnp.arange(l, dtype=F32)[:, None]
    bands = jnp.linspace(1e-4, HYENA_BANDS - 1, HYENA_BANDS, dtype=F32)[None, :]
    feats = jnp.concatenate([t, jnp.cos(bands * omega), -jnp.sin(bands * omega)], axis=-1)
    min_decay = math.log(HYENA_TARGET) / HYENA_SLOW_DECAY
    max_decay = math.log(HYENA_TARGET) / HYENA_FAST_DECAY
    deltas = jnp.abs(jnp.linspace(min_decay, max_decay, d, dtype=F32))
    window = jnp.exp(-t * deltas)
    return feats, window


def _scan_layout(fwd, bwd):
    b, l, _ = fwd.shape

    def one(x):
        return x.reshape(b, l, RWKV_HEADS, RWKV_HEAD).transpose(1, 3, 0, 2).reshape(l, RWKV_HEAD, b * RWKV_HEADS)

    return jnp.concatenate([one(fwd), one(bwd)], axis=-1)


def _from_scan_layout(y_f, y_b, b):
    l = y_f.shape[0]
    half = b * RWKV_HEADS

    def one(x):
        return x.reshape(l, RWKV_HEAD, b, RWKV_HEADS).transpose(2, 0, 3, 1).reshape(b, l, RWKV_WIDTH)

    return one(y_f[:, :, :half]), one(y_b[:, :, half:])


def kernel(x, c, mix_w_in, mix_w_out, attn_q_norm, attn_k_norm, rwkv_mu, rwkv_w0, rwkv_w_up, rwkv_a0, rwkv_a_up, rwkv_g_up, rwkv_k_k, rwkv_k_a, rwkv_r_k, rwkv_ln_g, rwkv_ln_b, hy_w_in, hy_conv_w, hy_conv_b, hy_f_w1, hy_f_b1, hy_f_w2, hy_f_b2, hy_f_w3, hy_f_b3, hy_sin_freq, hy_f_out, hy_skip, hy_w_out, ada_w, ada_b, norm_mix, norm_ffn, ffn_w1, ffn_w3, ffn_w2, final_norm):
    b, l, d = x.shape
    depth = ada_w.shape[0]
    tm = min(256, l)
    c_w = RWKV_WIDTH

    mod4 = ada_mod(c, ada_w, ada_b).reshape(depth, b, 1, 6 * d)

    perm = jnp.concatenate([jnp.arange(0, HEAD_DIM, 2), jnp.arange(1, HEAD_DIM, 2)])
    w_in = mix_w_in[0]
    w_q = w_in[:, :ATTN_WIDTH].reshape(d, ATTN_HEADS, HEAD_DIM)[:, :, perm].reshape(d, ATTN_WIDTH)
    w_kv = w_in[:, ATTN_WIDTH:ATTN_WIDTH + 2 * KV_WIDTH]
    w_k = w_kv[:, :KV_WIDTH].reshape(d, ATTN_KV_HEADS, HEAD_DIM)[:, :, perm].reshape(d, KV_WIDTH)
    w_kv = jnp.concatenate([w_k, w_kv[:, KV_WIDTH:]], axis=1)
    w_rw = jnp.pad(w_in[:, ATTN_WIDTH + 2 * KV_WIDTH:], ((0, 0), (0, RWKV_IN_PAD - RWKV_IN)))
    q, kv, rw = proj(x, mod4, 0, norm_mix[0], [w_q.astype(BF16), w_kv.astype(BF16), w_rw.astype(BF16)], tm)

    cos, sin = _rope_tables(l)
    y_attn = attn(q, kv, cos, sin, attn_q_norm[0][perm].reshape(1, HEAD_DIM),
                  attn_k_norm[0][perm].reshape(1, HEAD_DIM), tm)

    seg = jnp.kron(jnp.eye(RWKV_HEADS, dtype=F32), jnp.ones((RWKV_HEAD, RWKV_HEAD), F32))
    r, z, v, w, kd, bb, bonus, g = rwkv_prep(
        rw, jnp.pad(rwkv_mu[0], (0, RWKV_IN_PAD - RWKV_IN)).reshape(1, RWKV_IN_PAD),
        rwkv_w0[0].reshape(2, 1, c_w), rwkv_w_up[0], rwkv_a0[0].reshape(2, 1, c_w), rwkv_a_up[0],
        jnp.pad(rwkv_g_up[0], ((0, RWKV_IN_PAD - RWKV_IN), (0, 0))),
        rwkv_k_k[0].reshape(1, c_w), rwkv_k_a[0].reshape(1, c_w), rwkv_r_k[0].reshape(1, c_w), seg, tm)
    y_f, y_b = rwkv_scan([_scan_layout(r, r), _scan_layout(w[0], w[1]), _scan_layout(kd[0], kd[1]),
                          _scan_layout(v, v), _scan_layout(z, z), _scan_layout(bb[0], bb[1])], min(32, l))
    y_f, y_b = _from_scan_layout(y_f, y_b, b)
    x = mix_out([y_attn, y_f, y_b, bonus, g],
                [rwkv_ln_g[0].reshape(1, c_w), rwkv_ln_b[0].reshape(1, c_w), seg],
                x, mod4, 0, mix_w_out[0].astype(BF16), tm, True)
    x = ffn(x, mod4, 0, norm_ffn[0], ffn_w1[0].astype(BF16), ffn_w3[0].astype(BF16), ffn_w2[0].astype(BF16),
            final_norm, tm, False)

    (p3,) = proj(x, mod4, 1, norm_mix[1], [hy_w_in[0].astype(BF16)], tm)
    tn = min(512, d)
    p3 = short_conv(p3, hy_conv_w[0], hy_conv_b[0], min(256, d))
    feats, window = _hyena_tables(l, d)
    k_pad = 128
    feats = jnp.pad(feats, ((0, 0), (0, k_pad - feats.shape[1])))
    f_w1 = jnp.pad(hy_f_w1[0], ((0, k_pad - hy_f_w1.shape[1]), (0, 0)))
    taps = hy_filter(feats, f_w1, hy_f_b1[0], hy_f_w2[0], hy_f_b2[0], hy_f_w3[0], hy_f_b3[0],
                     hy_sin_freq[0], hy_f_out[0], window, tn)
    fwd, inv = _dft_matrices(l)
    spec = dft_spec(fwd, taps, d, tn)
    y1 = dft_mul(fwd[0], p3, 0, spec, 0, d, tn)
    z1 = dft_inv(inv, y1, p3, 0, p3, 1, hy_skip[0, 0], tm, tn)
    y2 = dft_mul(fwd[0], z1, 0, spec, 1, d, tn)
    z2 = dft_inv(inv, y2, z1, 0, p3, 2, hy_skip[0, 1], tm, tn)
    x = mix_out([z2], [], x, mod4, 1, hy_w_out[0].astype(BF16), tm, False)
    x = ffn(x, mod4, 1, norm_ffn[1], ffn_w1[1].astype(BF16), ffn_w3[1].astype(BF16), ffn_w2[1].astype(BF16),
            final_norm, tm, True)
    return x
```

```python
import functools
import math

import jax
import jax.numpy as jnp
from jax import lax
from jax.experimental import pallas as pl
from jax.experimental.pallas import tpu as pltpu

F32 = jnp.float32
BF16 = jnp.bfloat16
HIGHEST = lax.Precision.HIGHEST

GRID_W = 64
HEAD_DIM = 64
ATTN_HEADS = 8
ATTN_KV_HEADS = 2
ATTN_WIDTH = ATTN_HEADS * HEAD_DIM
KV_WIDTH = ATTN_KV_HEADS * HEAD_DIM
ROPE_THETA = 10000.0
RWKV_HEADS = 8
RWKV_HEAD = 64
RWKV_WIDTH = RWKV_HEADS * RWKV_HEAD
DECAY_LORA = 64
AAA_LORA = 64
GATE_LORA = 160
RWKV_IN = 3 * RWKV_WIDTH + 2 * DECAY_LORA + 2 * AAA_LORA + GATE_LORA
RWKV_IN_PAD = 2048
GN_EPS = 64e-5
HYENA_BANDS = 16
HYENA_TARGET = 1e-2
HYENA_FAST_DECAY = 0.3
HYENA_SLOW_DECAY = 1.5
NORM_EPS = 1e-6

V7X_VMEM_BYTES = 64 * 1024 * 1024
VMEM_LIMIT = V7X_VMEM_BYTES - 8 * 1024 * 1024
FREQ_TILE = 512
SCAN_BLOCK = 32
SUBLANES = 8


def _params(*sem):
    return pltpu.CompilerParams(dimension_semantics=sem, vmem_limit_bytes=VMEM_LIMIT)


def _const_spec(shape):
    zeros = (0,) * len(shape)
    return pl.BlockSpec(shape, lambda *_: zeros, pipeline_mode=pl.Buffered(1))


def _dot(a, b):
    return jnp.dot(a, b, preferred_element_type=F32)


def _split(x):
    hi = x.astype(BF16)
    return hi, (x - hi.astype(F32)).astype(BF16)


def _dot_split(a, b):
    ah, al = _split(a)
    bh, bl = _split(b)
    return _dot(ah, bh) + (_dot(ah, bl) + _dot(al, bh))


def _seg_sum(a, seg):
    ah, al = _split(a)
    return _dot(ah, seg) + _dot(al, seg)


def _dot_hi(a, b):
    return jnp.dot(a, b, precision=HIGHEST, preferred_element_type=F32)


def _rms_mod(x, gain, scale, shift):
    ms = jnp.mean(x * x, axis=-1, keepdims=True)
    return (x * lax.rsqrt(ms + NORM_EPS) * gain) * (1.0 + scale) + shift


def _ada_kernel(c_ref, w_ref, b_ref, o_ref):
    c = c_ref[...]
    cond = c * jax.nn.sigmoid(c)
    o_ref[...] = _dot_hi(cond, w_ref[...]) + b_ref[...]


def ada_mod(c, ada_w, ada_b):
    depth, d, n = ada_w.shape
    b = c.shape[0]
    tn = 1536
    return pl.pallas_call(
        _ada_kernel,
        grid=(depth, n // tn),
        in_specs=[pl.BlockSpec((b, d), lambda l, j: (0, 0)),
                  pl.BlockSpec((None, d, tn), lambda l, j: (l, 0, j)),
                  pl.BlockSpec((None, 1, tn), lambda l, j: (l, 0, j))],
        out_specs=pl.BlockSpec((None, b, tn), lambda l, j: (l, 0, j)),
        out_shape=jax.ShapeDtypeStruct((depth, b, n), F32),
        compiler_params=_params("parallel", "parallel"),
        name="ada_mod",
    )(c, ada_w, ada_b.reshape(depth, 1, n))


def _mod_spec(d, layer, chunk):
    return pl.BlockSpec((None, None, 1, d), lambda b, *_: (layer, b, 0, chunk))


def _proj_kernel(n_out, x_ref, sh_ref, sc_ref, g_ref, *refs):
    h = _rms_mod(x_ref[0], g_ref[...], sc_ref[...], sh_ref[...]).astype(BF16)
    for w_ref, o_ref in zip(refs[:n_out], refs[n_out:]):
        o_ref[0] = _dot(h, w_ref[...])


def proj(x, mod4, layer, gain, weights, tm):
    b, l, d = x.shape
    n_out = len(weights)
    return pl.pallas_call(
        functools.partial(_proj_kernel, n_out),
        grid=(b, l // tm),
        in_specs=[pl.BlockSpec((1, tm, d), lambda bi, i: (bi, i, 0)),
                  _mod_spec(d, layer, 0), _mod_spec(d, layer, 1),
                  _const_spec((1, d))] + [_const_spec(w.shape) for w in weights],
        out_specs=[pl.BlockSpec((1, tm, w.shape[1]), lambda bi, i: (bi, i, 0)) for w in weights],
        out_shape=[jax.ShapeDtypeStruct((b, l, w.shape[1]), F32) for w in weights],
        compiler_params=_params("parallel", "parallel"),
        name="proj",
    )(x, mod4, mod4, gain.reshape(1, d), *weights)


def _headnorm_rope(x, gain, cos, sin):
    ms = jnp.mean(x * x, axis=-1, keepdims=True)
    y = x * lax.rsqrt(ms + NORM_EPS) * gain
    half = HEAD_DIM // 2
    rot = jnp.concatenate([-y[:, half:], y[:, :half]], axis=-1)
    return y * cos + rot * sin


def _attn_kernel(q_ref, kv_ref, cq_ref, sq_ref, ck_ref, sk_ref, qn_ref, kn_ref, o_ref, k_s, v_s):
    @pl.when(pl.program_id(1) == 0)
    def _():
        kv = kv_ref[0]
        for hk in range(ATTN_KV_HEADS):
            k = kv[:, HEAD_DIM * hk:HEAD_DIM * (hk + 1)]
            k_s[hk] = _headnorm_rope(k, kn_ref[...], ck_ref[...], sk_ref[...]).astype(BF16)
            v_s[hk] = kv[:, KV_WIDTH + HEAD_DIM * hk:KV_WIDTH + HEAD_DIM * (hk + 1)].astype(BF16)

    q = q_ref[0]
    group = ATTN_HEADS // ATTN_KV_HEADS
    outs = []
    for h in range(ATTN_HEADS):
        qh = _headnorm_rope(q[:, HEAD_DIM * h:HEAD_DIM * (h + 1)], qn_ref[...], cq_ref[...], sq_ref[...])
        qh = (qh * (HEAD_DIM ** -0.5)).astype(BF16)
        s = lax.dot_general(qh, k_s[h // group], (((1,), (1,)), ((), ())), preferred_element_type=F32)
        m = jnp.max(s, axis=-1, keepdims=True)
        p = jnp.exp(s - m)
        denom = jnp.sum(p, axis=-1, keepdims=True)
        o = _dot(p.astype(BF16), v_s[h // group])
        outs.append(o / denom)
    o_ref[0] = jnp.concatenate(outs, axis=-1)


def attn(q, kv, cos, sin, qn, kn, tq):
    b, l, _ = q.shape
    return pl.pallas_call(
        _attn_kernel,
        grid=(b, l // tq),
        in_specs=[pl.BlockSpec((1, tq, ATTN_WIDTH), lambda bi, i: (bi, i, 0)),
                  pl.BlockSpec((1, l, 2 * KV_WIDTH), lambda bi, i: (bi, 0, 0)),
                  pl.BlockSpec((tq, HEAD_DIM), lambda bi, i: (i, 0)),
                  pl.BlockSpec((tq, HEAD_DIM), lambda bi, i: (i, 0)),
                  _const_spec((l, HEAD_DIM)), _const_spec((l, HEAD_DIM)),
                  _const_spec((1, HEAD_DIM)), _const_spec((1, HEAD_DIM))],
        out_specs=pl.BlockSpec((1, tq, ATTN_WIDTH), lambda bi, i: (bi, i, 0)),
        out_shape=jax.ShapeDtypeStruct((b, l, ATTN_WIDTH), F32),
        scratch_shapes=[pltpu.VMEM((ATTN_KV_HEADS, l, HEAD_DIM), BF16),
                        pltpu.VMEM((ATTN_KV_HEADS, l, HEAD_DIM), BF16)],
        compiler_params=_params("parallel", "arbitrary"),
        name="attn",
    )(q, kv, cos, sin, cos, sin, qn, kn)


def _halo_specs(tt, w, l):
    nb = l // SUBLANES
    per = tt // SUBLANES
    main = pl.BlockSpec((1, tt, w), lambda bi, i: (bi, i, 0))
    prev = pl.BlockSpec((1, SUBLANES, w), lambda bi, i: (bi, jnp.maximum(i * per - 1, 0), 0))
    nxt = pl.BlockSpec((1, SUBLANES, w), lambda bi, i: (bi, jnp.minimum((i + 1) * per, nb - 1), 0))
    return [main, prev, nxt]


def _neighbours(cur, prow, nrow):
    tt = cur.shape[0]
    row = lax.broadcasted_iota(jnp.int32, cur.shape, 0)
    prev = jnp.where(row == 0, prow, pltpu.roll(cur, 1, 0))
    nxt = jnp.where(row == tt - 1, nrow, pltpu.roll(cur, tt - 1, 0))
    return prev, nxt


def _halo_rows(xp_ref, xn_ref, axis):
    i = pl.program_id(axis)
    last = pl.num_programs(axis) - 1
    prow = jnp.where(i > 0, xp_ref[0, SUBLANES - 1:SUBLANES, :], 0.0)
    nrow = jnp.where(i < last, xn_ref[0, 0:1, :], 0.0)
    return prow, nrow


def _rwkv_prep_kernel(x_ref, xp_ref, xn_ref, mu_ref, w0_ref, wup_ref, a0_ref, aup_ref, gup_ref,
                      kk_ref, ka_ref, rk_ref, seg_ref,
                      r_o, z_o, v_o, w_o, kd_o, b_o, bonus_o, g_o):
    cur = x_ref[0]
    prev, nxt = _neighbours(cur, *_halo_rows(xp_ref, xn_ref, 1))
    ps = cur + (0.5 * (prev + nxt) - cur) * mu_ref[...]
    c = RWKV_WIDTH
    r = ps[:, 0:c]
    k = ps[:, c:2 * c]
    v = ps[:, 2 * c:3 * c]
    o_a = 3 * c + 2 * DECAY_LORA
    o_g = o_a + 2 * AAA_LORA
    kk = k * kk_ref[...]
    kk = kk * lax.rsqrt(jnp.maximum(_seg_sum(kk * kk, seg_ref[...]), 1e-24))
    for d in range(2):
        r_o[d, 0] = r
        v_o[d, 0] = v
        z_o[d, 0] = -kk
        w_lo = ps[:, 3 * c + DECAY_LORA * d:3 * c + DECAY_LORA * (d + 1)]
        a_lo = ps[:, o_a + AAA_LORA * d:o_a + AAA_LORA * (d + 1)]
        decay = jnp.exp(-math.exp(-0.5) * jax.nn.sigmoid(w0_ref[d] + _dot_split(jnp.tanh(w_lo), wup_ref[d])))
        a = jax.nn.sigmoid(a0_ref[d] + _dot_split(a_lo, aup_ref[d]))
        w_o[d, 0] = decay
        kd_o[d, 0] = k * (1.0 + (a - 1.0) * ka_ref[...])
        b_o[d, 0] = kk * a
    bonus_o[0] = _seg_sum(r * k * rk_ref[...], seg_ref[...]) * v
    g_o[0] = _dot_split(jax.nn.sigmoid(ps[:, o_g:RWKV_IN_PAD]), gup_ref[...])


def rwkv_prep(rw, mu, w0, w_up, a0, a_up, g_up, k_k, k_a, r_k, seg, tt):
    b, l, w = rw.shape
    c = RWKV_WIDTH
    one = jax.ShapeDtypeStruct((b, l, c), F32)
    two = jax.ShapeDtypeStruct((2, b, l, c), F32)
    spec1 = pl.BlockSpec((1, tt, c), lambda bi, i: (bi, i, 0))
    spec2 = pl.BlockSpec((2, 1, tt, c), lambda bi, i: (0, bi, i, 0))
    consts = [mu, w0, w_up, a0, a_up, g_up, k_k, k_a, r_k, seg]
    return pl.pallas_call(
        _rwkv_prep_kernel,
        grid=(b, l // tt),
        in_specs=_halo_specs(tt, w, l) + [_const_spec(a.shape) for a in consts],
        out_specs=[spec2] * 6 + [spec1] * 2,
        out_shape=[two] * 6 + [one] * 2,
        compiler_params=_params("parallel", "parallel"),
        name="rwkv_prep",
    )(rw, rw, rw, *consts)


SCAN_ROWS = 32
_R, _W, _KD, _V, _Z, _B = range(6)


def _rwkv_scan_kernel(*refs):
    ins, (yf_ref, yb_ref, s_ref, m_ref, sz_ref) = refs[:12], refs[12:]
    tb, n, lanes = ins[0].shape
    fwd_lane = lax.broadcasted_iota(jnp.int32, (n, lanes), 1) < lanes // 2

    def stage(which, t):
        m_ref[which] = jnp.where(fwd_lane, ins[2 * which][t], ins[2 * which + 1][tb - 1 - t])

    @pl.when(pl.program_id(0) == 0)
    def _():
        s_ref[...] = jnp.zeros_like(s_ref)

    halves = [slice(h * SCAN_ROWS, (h + 1) * SCAN_ROWS) for h in range(n // SCAN_ROWS)]

    stage(_Z, 0)
    for rows in halves:
        acc = s_ref[0, rows, :] * m_ref[_Z, 0:1, :]
        for k in range(1, n):
            acc = acc + s_ref[k, rows, :] * m_ref[_Z, k:k + 1, :]
        sz_ref[rows, :] = acc

    def step(t, carry):
        for which in (_R, _W, _KD, _V, _B):
            stage(which, t)
        stage(_Z, jnp.minimum(t + 1, tb - 1))
        for rows in halves:
            sz = sz_ref[rows, :]
            vt = m_ref[_V, rows, :]
            y = None
            sz_next = None
            for k in range(n):
                sk = (s_ref[k, rows, :] * m_ref[_W, k:k + 1, :] + sz * m_ref[_B, k:k + 1, :]
                      + vt * m_ref[_KD, k:k + 1, :])
                s_ref[k, rows, :] = sk
                yk = sk * m_ref[_R, k:k + 1, :]
                zk = sk * m_ref[_Z, k:k + 1, :]
                y = yk if y is None else y + yk
                sz_next = zk if sz_next is None else sz_next + zk
            sz_ref[rows, :] = sz_next
            yf_ref[t, rows, :] = y
            yb_ref[tb - 1 - t, rows, :] = y
        return carry

    lax.fori_loop(0, tb, step, 0)


def rwkv_scan(arrays, tb):
    l, n, lanes = arrays[0].shape
    nt = l // tb
    fwd = pl.BlockSpec((tb, n, lanes), lambda i: (i, 0, 0))
    rev = pl.BlockSpec((tb, n, lanes), lambda i: (nt - 1 - i, 0, 0))
    out = jax.ShapeDtypeStruct((l, n, lanes), F32)
    return pl.pallas_call(
        _rwkv_scan_kernel,
        grid=(nt,),
        in_specs=[fwd, rev] * 6,
        out_specs=[fwd, rev],
        out_shape=[out, out],
        scratch_shapes=[pltpu.VMEM((n, n, lanes), F32), pltpu.VMEM((6, n, lanes), F32),
                        pltpu.VMEM((n, lanes), F32)],
        compiler_params=_params("arbitrary"),
        name="rwkv_scan",
    )(*[a for a in arrays for _ in range(2)])


def _mix_out_kernel(rwkv, *refs):
    if rwkv:
        (ya_ref, yf_ref, yb_ref, bonus_ref, g_ref, lng_ref, lnb_ref, seg_ref,
         x_ref, gate_ref, w_ref, o_ref) = refs
        y = yf_ref[0] + yb_ref[0]
        inv_n = 1.0 / RWKV_HEAD
        mean = _seg_sum(y, seg_ref[...]) * inv_n
        cen = y - mean
        var = _seg_sum(cen * cen, seg_ref[...]) * inv_n
        yn = cen * lax.rsqrt(var + GN_EPS) * lng_ref[...] + lnb_ref[...]
        yr = (yn + bonus_ref[0]) * g_ref[0]
        a = jnp.concatenate([ya_ref[0], yr], axis=-1)
    else:
        a_ref, x_ref, gate_ref, w_ref, o_ref = refs
        a = a_ref[0]
    o_ref[0] = x_ref[0] + gate_ref[...] * _dot(a.astype(BF16), w_ref[...])


def mix_out(acts, consts, x, mod4, layer, w, tm, rwkv):
    b, l, d = x.shape
    in_specs = []
    for i_act, a in enumerate(acts):
        if isinstance(a, tuple):
            arr, direction = a
            in_specs.append(pl.BlockSpec((None, 1, tm, arr.shape[3]), lambda bi, i, dr=direction: (dr, bi, i, 0)))
            acts[i_act] = arr
        else:
            in_specs.append(pl.BlockSpec((1, tm, a.shape[2]), lambda bi, i: (bi, i, 0)))
    in_specs += [_const_spec(a.shape) for a in consts]
    in_specs += [pl.BlockSpec((1, tm, d), lambda bi, i: (bi, i, 0)), _mod_spec(d, layer, 2), _const_spec(w.shape)]
    return pl.pallas_call(
        functools.partial(_mix_out_kernel, rwkv),
        grid=(b, l // tm),
        in_specs=in_specs,
        out_specs=pl.BlockSpec((1, tm, d), lambda bi, i: (bi, i, 0)),
        out_shape=jax.ShapeDtypeStruct((b, l, d), F32),
        compiler_params=_params("parallel", "parallel"),
        name="mix_out",
    )(*acts, *consts, x, mod4, w)


def _ffn_kernel(final, x_ref, sh_ref, sc_ref, gate_ref, g_ref, w1_ref, w3_ref, w2_ref, fg_ref, o_ref):
    x = x_ref[0]
    h = _rms_mod(x, g_ref[...], sc_ref[...], sh_ref[...]).astype(BF16)
    a = _dot(h, w1_ref[...])
    u = (a * jax.nn.sigmoid(a)) * _dot(h, w3_ref[...])
    out = x + gate_ref[...] * _dot(u.astype(BF16), w2_ref[...])
    if final:
        ms = jnp.mean(out * out, axis=-1, keepdims=True)
        out = out * lax.rsqrt(ms + NORM_EPS) * fg_ref[...]
    o_ref[0] = out


def ffn(x, mod4, layer, gain, w1, w3, w2, final_gain, tm, final):
    b, l, d = x.shape
    return pl.pallas_call(
        functools.partial(_ffn_kernel, final),
        grid=(b, l // tm),
        in_specs=[pl.BlockSpec((1, tm, d), lambda bi, i: (bi, i, 0)),
                  _mod_spec(d, layer, 3), _mod_spec(d, layer, 4), _mod_spec(d, layer, 5),
                  _const_spec((1, d)), _const_spec(w1.shape), _const_spec(w3.shape), _const_spec(w2.shape),
                  _const_spec((1, d))],
        out_specs=pl.BlockSpec((1, tm, d), lambda bi, i: (bi, i, 0)),
        out_shape=jax.ShapeDtypeStruct((b, l, d), F32),
        compiler_params=_params("parallel", "parallel"),
        name="ffn",
    )(x, mod4, mod4, mod4, gain.reshape(1, d), w1, w3, w2, final_gain.reshape(1, d))


def _short_conv_kernel(x_ref, w_ref, b_ref, o_ref):
    cur = x_ref[0]
    prev, nxt = _neighbours(cur, 0.0, 0.0)
    o_ref[0] = prev * w_ref[0:1, :] + cur * w_ref[1:2, :] + nxt * w_ref[2:3, :] + b_ref[...]


def short_conv(p, conv_w, conv_b, tw):
    b, l, w = p.shape
    return pl.pallas_call(
        _short_conv_kernel,
        grid=(b, w // tw),
        in_specs=[pl.BlockSpec((1, l, tw), lambda bi, j: (bi, 0, j)),
                  pl.BlockSpec((3, tw), lambda bi, j: (0, j)),
                  pl.BlockSpec((1, tw), lambda bi, j: (0, j))],
        out_specs=pl.BlockSpec((1, l, tw), lambda bi, j: (bi, 0, j)),
        out_shape=jax.ShapeDtypeStruct((b, l, w), F32),
        compiler_params=_params("parallel", "parallel"),
        name="short_conv",
    )(p, conv_w, conv_b.reshape(1, w))


def _hy_filter_kernel(d_tiles, feats_ref, w1_ref, b1_ref, w2_ref, b2_ref, w3_ref, b3_ref, fr_ref,
                      wout_ref, win_ref, o_ref, hid_s):
    j = pl.program_id(0)

    @pl.when(j == 0)
    def _():
        fr = fr_ref[...]
        hid = jnp.sin(fr * (_dot_hi(feats_ref[...], w1_ref[...]) + b1_ref[...]))
        hid = jnp.sin(fr * (_dot_hi(hid, w2_ref[...]) + b2_ref[...]))
        hid_s[...] = jnp.sin(fr * (_dot_hi(hid, w3_ref[...]) + b3_ref[...]))

    filt = _dot_hi(hid_s[...], wout_ref[...]) * win_ref[...]
    backward = (j // d_tiles) % 2 == 1
    row = lax.broadcasted_iota(jnp.int32, filt.shape, 0)
    o_ref[...] = jnp.where(jnp.logical_and(backward, row == 0), 0.0, filt)


def hy_filter(feats, w1, b1, w2, b2, w3, b3, fr, w_out, window, tn):
    l = feats.shape[0]
    hf = w2.shape[0]
    n = w_out.shape[1]
    d = window.shape[1]
    d_tiles = d // tn
    consts = [feats, w1, b1.reshape(1, hf), w2, b2.reshape(1, hf), w3, b3.reshape(1, hf), fr.reshape(1, hf)]
    return pl.pallas_call(
        functools.partial(_hy_filter_kernel, d_tiles),
        grid=(n // tn,),
        in_specs=[_const_spec(a.shape) for a in consts]
        + [pl.BlockSpec((hf, tn), lambda j: (0, j)), pl.BlockSpec((l, tn), lambda j: (0, j % d_tiles))],
        out_specs=pl.BlockSpec((l, tn), lambda j: (0, j)),
        out_shape=jax.ShapeDtypeStruct((l, n), F32),
        scratch_shapes=[pltpu.VMEM((l, hf), F32)],
        compiler_params=_params("arbitrary"),
        name="hy_filter",
    )(*consts, w_out, window)


def _dft_matrices(l):
    n = 2 * l
    lane = min(128, l)
    k = jnp.arange(l, dtype=jnp.int32)[:, None]
    t = jnp.arange(l, dtype=jnp.int32)
    a = ((k * jnp.arange(lane, dtype=jnp.int32)[None, :]) % n).astype(F32) * (2.0 * math.pi / n)
    c = ((k * (lane * jnp.arange(l // lane, dtype=jnp.int32))[None, :]) % n).astype(F32) * (2.0 * math.pi / n)
    ca, sa, cc, sc = jnp.cos(a)[:, None, :], jnp.sin(a)[:, None, :], jnp.cos(c)[:, :, None], jnp.sin(c)[:, :, None]
    cos = (ca * cc - sa * sc).reshape(l, l)
    sin = (sa * cc + ca * sc).reshape(l, l)
    alt = jnp.where(t % 2 == 0, 1.0, -1.0).astype(F32)
    f_re = cos.astype(BF16)
    f_im = jnp.where(k == 0, alt[None, :], -sin).astype(BF16)
    col0 = (t == 0)[None, :]
    g_re = jnp.where(col0, 1.0 / n, (2.0 / n) * cos).astype(BF16)
    g_im = jnp.where(col0, alt[:, None] / n, (-2.0 / n) * sin).astype(BF16)
    return (f_re, f_im), (g_re, g_im)


def _nyquist_slot(i, shape):
    row = lax.broadcasted_iota(jnp.int32, shape, 0)
    return jnp.logical_and(i == 0, row == 0)


def _dft_spec_kernel(fre_ref, fim_ref, tf_ref, tb_ref, o_ref):
    tf = tf_ref[...].astype(BF16)
    tb = tb_ref[...].astype(BF16)
    o_ref[0] = _dot(fre_ref[...], tf) + _dot(fre_ref[...], tb)
    im_f = _dot(fim_ref[...], tf)
    im_b = _dot(fim_ref[...], tb)
    o_ref[1] = jnp.where(_nyquist_slot(pl.program_id(1), im_f.shape), im_f + im_b, im_f - im_b)


def dft_spec(fwd, taps, d, ft, tn):
    f_re, f_im = fwd
    l = f_re.shape[0]
    d_tiles = d // tn
    orders = taps.shape[1] // (2 * d)
    f_spec = pl.BlockSpec((ft, l), lambda j, i: (i, 0))
    return pl.pallas_call(
        _dft_spec_kernel,
        grid=(orders * d_tiles, l // ft),
        in_specs=[f_spec, f_spec,
                  pl.BlockSpec((l, tn), lambda j, i: (0, (j // d_tiles) * 2 * d_tiles + j % d_tiles)),
                  pl.BlockSpec((l, tn), lambda j, i: (0, (j // d_tiles) * 2 * d_tiles + d_tiles + j % d_tiles))],
        out_specs=pl.BlockSpec((2, ft, tn), lambda j, i: (0, i, j)),
        out_shape=jax.ShapeDtypeStruct((2, l, orders * d), F32),
        compiler_params=_params("parallel", "parallel"),
        name="dft_spec",
    )(f_re, f_im, taps, taps)


def _dft_mul_kernel(fre_ref, fim_ref, u_ref, h_ref, o_ref, u_s):
    i = pl.program_id(2)

    @pl.when(i == 0)
    def _():
        u_s[...] = u_ref[0].astype(BF16)

    ur = _dot(fre_ref[...], u_s[...])
    ui = _dot(fim_ref[...], u_s[...])
    hr, hi = h_ref[0], h_ref[1]
    nyq = _nyquist_slot(i, ur.shape)
    o_ref[0, 0] = (ur * hr - jnp.where(nyq, 0.0, ui * hi)).astype(BF16)
    o_ref[0, 1] = jnp.where(nyq, ui * hi, ur * hi + ui * hr).astype(BF16)


def dft_mul(fwd, u, u_col, spec, spec_col, d, ft, tn):
    f_re, f_im = fwd
    l = f_re.shape[0]
    b = u.shape[0]
    d_tiles = d // tn
    f_spec = pl.BlockSpec((ft, l), lambda bi, j, i: (i, 0))
    return pl.pallas_call(
        _dft_mul_kernel,
        grid=(b, d_tiles, l // ft),
        in_specs=[f_spec, f_spec,
                  pl.BlockSpec((1, l, tn), lambda bi, j, i: (bi, 0, u_col * d_tiles + j)),
                  pl.BlockSpec((2, ft, tn), lambda bi, j, i: (0, i, spec_col * d_tiles + j))],
        out_specs=pl.BlockSpec((1, 2, ft, tn), lambda bi, j, i: (bi, 0, i, j)),
        out_shape=jax.ShapeDtypeStruct((b, 2, l, d), BF16),
        scratch_shapes=[pltpu.VMEM((l, tn), BF16)],
        compiler_params=_params("parallel", "parallel", "arbitrary"),
        name="dft_mul",
    )(f_re, f_im, u, spec)


def _dft_inv_kernel(gre_ref, gim_ref, y_ref, u_ref, gate_ref, skip_ref, o_ref):
    conv = _dot(gre_ref[...], y_ref[0, 0]) + _dot(gim_ref[...], y_ref[0, 1])
    o_ref[0] = gate_ref[0] * (conv + u_ref[0] * skip_ref[...])


def dft_inv(inv, y, u, u_col, gate, gate_col, skip, tm, tn):
    g_re, g_im = inv
    l = g_re.shape[0]
    b, _, _, d = y.shape
    d_tiles = d // tn
    g_spec = pl.BlockSpec((tm, l), lambda bi, j, i: (i, 0))
    return pl.pallas_call(
        _dft_inv_kernel,
        grid=(b, d_tiles, l // tm),
        in_specs=[g_spec, g_spec,
                  pl.BlockSpec((1, 2, l, tn), lambda bi, j, i: (bi, 0, 0, j)),
                  pl.BlockSpec((1, tm, tn), lambda bi, j, i: (bi, i, u_col * d_tiles + j)),
                  pl.BlockSpec((1, tm, tn), lambda bi, j, i: (bi, i, gate_col * d_tiles + j)),
                  pl.BlockSpec((1, tn), lambda bi, j, i: (0, j))],
        out_specs=pl.BlockSpec((1, tm, tn), lambda bi, j, i: (bi, i, j)),
        out_shape=jax.ShapeDtypeStruct((b, l, d), F32),
        compiler_params=_params("parallel", "parallel", "parallel"),
        name="dft_inv",
    )(g_re, g_im, y, u, gate, skip.reshape(1, d))


def _rope_tables(l):
    rows = l // GRID_W
    row = jnp.repeat(jnp.arange(rows), GRID_W).astype(F32)
    col = jnp.tile(jnp.arange(GRID_W), rows).astype(F32)
    half = HEAD_DIM // 2
    inv_freq = ROPE_THETA ** (-jnp.arange(0, half, 2, dtype=F32) / half)
    ang = jnp.concatenate([row[:, None] * inv_freq, col[:, None] * inv_freq], axis=-1)
    cos, sin = jnp.cos(ang), jnp.sin(ang)
    return jnp.concatenate([cos, cos], axis=-1), jnp.concatenate([sin, sin], axis=-1)


def _hyena_tables(l, d):
    t = jnp.linspace(0.0, 1.0, l, dtype=F32)[:, None]
    omega = (2.0 * math.pi / l) * jnp.arange(l, dtype=F32)[:, None]
    bands = jnp.linspace(1e-4, HYENA_BANDS - 1, HYENA_BANDS, dtype=F32)[None, :]
    feats = jnp.concatenate([t, jnp.cos(bands * omega), -jnp.sin(bands * omega)], axis=-1)
    min_decay = math.log(HYENA_TARGET) / HYENA_SLOW_DECAY
    max_decay = math.log(HYENA_TARGET) / HYENA_FAST_DECAY
    deltas = jnp.abs(jnp.linspace(min_decay, max_decay, d, dtype=F32))
    window = jnp.exp(-t * deltas)
    return feats, window


def _to_scan_layout(x):
    _, b, l, _ = x.shape
    x = x.reshape(2 * b, l, RWKV_HEADS, RWKV_HEAD).transpose(1, 3, 0, 2)
    return x.reshape(l, RWKV_HEAD, 2 * b * RWKV_HEADS)


def _from_scan_layout(y, b):
    l = y.shape[0]
    return y.reshape(l, RWKV_HEAD, 2 * b, RWKV_HEADS).transpose(2, 0, 3, 1).reshape(2, b, l, RWKV_WIDTH)


def kernel(x, c, mix_w_in, mix_w_out, attn_q_norm, attn_k_norm, rwkv_mu, rwkv_w0, rwkv_w_up, rwkv_a0, rwkv_a_up, rwkv_g_up, rwkv_k_k, rwkv_k_a, rwkv_r_k, rwkv_ln_g, rwkv_ln_b, hy_w_in, hy_conv_w, hy_conv_b, hy_f_w1, hy_f_b1, hy_f_w2, hy_f_b2, hy_f_w3, hy_f_b3, hy_sin_freq, hy_f_out, hy_skip, hy_w_out, ada_w, ada_b, norm_mix, norm_ffn, ffn_w1, ffn_w3, ffn_w2, final_norm):
    b, l, d = x.shape
    depth = ada_w.shape[0]
    tm = min(256, l)
    c_w = RWKV_WIDTH

    mod4 = ada_mod(c, ada_w, ada_b).reshape(depth, b, 1, 6 * d)

    perm = jnp.concatenate([jnp.arange(0, HEAD_DIM, 2), jnp.arange(1, HEAD_DIM, 2)])
    w_in = mix_w_in[0]
    w_q = w_in[:, :ATTN_WIDTH].reshape(d, ATTN_HEADS, HEAD_DIM)[:, :, perm].reshape(d, ATTN_WIDTH)
    w_kv = w_in[:, ATTN_WIDTH:ATTN_WIDTH + 2 * KV_WIDTH]
    w_k = w_kv[:, :KV_WIDTH].reshape(d, ATTN_KV_HEADS, HEAD_DIM)[:, :, perm].reshape(d, KV_WIDTH)
    w_kv = jnp.concatenate([w_k, w_kv[:, KV_WIDTH:]], axis=1)
    w_rw = jnp.pad(w_in[:, ATTN_WIDTH + 2 * KV_WIDTH:], ((0, 0), (0, RWKV_IN_PAD - RWKV_IN)))
    q, kv, rw = proj(x, mod4, 0, norm_mix[0], [w_q.astype(BF16), w_kv.astype(BF16), w_rw.astype(BF16)], tm)

    cos, sin = _rope_tables(l)
    y_attn = attn(q, kv, cos, sin, attn_q_norm[0][perm].reshape(1, HEAD_DIM),
                  attn_k_norm[0][perm].reshape(1, HEAD_DIM), tm)

    seg = jnp.kron(jnp.eye(RWKV_HEADS, dtype=BF16), jnp.ones((RWKV_HEAD, RWKV_HEAD), BF16))
    r, z, v, w, kd, bb, bonus, g = rwkv_prep(
        rw, jnp.pad(rwkv_mu[0], (0, RWKV_IN_PAD - RWKV_IN)).reshape(1, RWKV_IN_PAD),
        rwkv_w0[0].reshape(2, 1, c_w), rwkv_w_up[0], rwkv_a0[0].reshape(2, 1, c_w), rwkv_a_up[0],
        jnp.pad(rwkv_g_up[0], ((0, RWKV_IN_PAD - RWKV_IN), (0, 0))),
        rwkv_k_k[0].reshape(1, c_w), rwkv_k_a[0].reshape(1, c_w), rwkv_r_k[0].reshape(1, c_w), seg, tm)
    y_f, y_b = rwkv_scan([_to_scan_layout(a) for a in (r, w, kd, v, z, bb)], min(SCAN_BLOCK, l))
    x = mix_out([y_attn, (_from_scan_layout(y_f, b), 0), (_from_scan_layout(y_b, b), 1), bonus, g],
                [rwkv_ln_g[0].reshape(1, c_w), rwkv_ln_b[0].reshape(1, c_w), seg],
                x, mod4, 0, mix_w_out[0].astype(BF16), tm, True)
    x = ffn(x, mod4, 0, norm_ffn[0], ffn_w1[0].astype(BF16), ffn_w3[0].astype(BF16), ffn_w2[0].astype(BF16),
            final_norm, tm, False)

    (p3,) = proj(x, mod4, 1, norm_mix[1], [hy_w_in[0].astype(BF16)], tm)
    tn = min(512, d)
    p3 = short_conv(p3, hy_conv_w[0], hy_conv_b[0], min(256, d))
    feats, window = _hyena_tables(l, d)
    k_pad = 128
    feats = jnp.pad(feats, ((0, 0), (0, k_pad - feats.shape[1])))
    f_w1 = jnp.pad(hy_f_w1[0], ((0, k_pad - hy_f_w1.shape[1]), (0, 0)))
    taps = hy_filter(feats, f_w1, hy_f_b1[0], hy_f_w2[0], hy_f_b2[0], hy_f_w3[0], hy_f_b3[0],
                     hy_sin_freq[0], hy_f_out[0], window, tn)
    fwd, inv = _dft_matrices(l)
    ft = min(FREQ_TILE, l)
    spec = dft_spec(fwd, taps, d, ft, tn)
    y1 = dft_mul(fwd, p3, 0, spec, 0, d, ft, tn)
    z1 = dft_inv(inv, y1, p3, 0, p3, 1, hy_skip[0, 0], ft, tn)
    y2 = dft_mul(fwd, z1, 0, spec, 1, d, ft, tn)
    z2 = dft_inv(inv, y2, z1, 0, p3, 2, hy_skip[0, 1], ft, tn)
    x = mix_out([z2], [], x, mod4, 1, hy_w_out[0].astype(BF16), tm, False)
    x = ffn(x, mod4, 1, norm_ffn[1], ffn_w1[1].astype(BF16), ffn_w3[1].astype(BF16), ffn_w2[1].astype(BF16),
            final_norm, tm, True)
    return x
```

```python
import functools
import math

import jax
import jax.numpy as jnp
from jax import lax
from jax.experimental import pallas as pl
from jax.experimental.pallas import tpu as pltpu

F32 = jnp.float32
BF16 = jnp.bfloat16
HIGHEST = lax.Precision.HIGHEST

GRID_W = 64
HEAD_DIM = 64
ATTN_HEADS = 8
ATTN_KV_HEADS = 2
ATTN_WIDTH = ATTN_HEADS * HEAD_DIM
KV_WIDTH = ATTN_KV_HEADS * HEAD_DIM
ROPE_THETA = 10000.0
RWKV_HEADS = 8
RWKV_HEAD = 64
RWKV_WIDTH = RWKV_HEADS * RWKV_HEAD
DECAY_LORA = 64
AAA_LORA = 64
GATE_LORA = 160
RWKV_IN = 3 * RWKV_WIDTH + 2 * DECAY_LORA + 2 * AAA_LORA + GATE_LORA
RWKV_IN_PAD = 2048
GN_EPS = 64e-5
HYENA_BANDS = 16
HYENA_TARGET = 1e-2
HYENA_FAST_DECAY = 0.3
HYENA_SLOW_DECAY = 1.5
NORM_EPS = 1e-6

V7X_VMEM_BYTES = 64 * 1024 * 1024
VMEM_LIMIT = V7X_VMEM_BYTES - 8 * 1024 * 1024
FREQ_TILE = 512
SCAN_BLOCK = 32
SUBLANES = 8


def _params(*sem):
    return pltpu.CompilerParams(dimension_semantics=sem, vmem_limit_bytes=VMEM_LIMIT)


def _const_spec(shape):
    zeros = (0,) * len(shape)
    return pl.BlockSpec(shape, lambda *_: zeros, pipeline_mode=pl.Buffered(1))


def _dot(a, b):
    return jnp.dot(a, b, preferred_element_type=F32)


def _split(x):
    hi = x.astype(BF16)
    return hi, (x - hi.astype(F32)).astype(BF16)


def _dot_split(a, b):
    ah, al = _split(a)
    bh, bl = _split(b)
    return _dot(ah, bh) + (_dot(ah, bl) + _dot(al, bh))


def _seg_sum(a, seg):
    ah, al = _split(a)
    return _dot(ah, seg) + _dot(al, seg)


def _dot_hi(a, b):
    return jnp.dot(a, b, precision=HIGHEST, preferred_element_type=F32)


def _rms_mod(x, gain, scale, shift):
    ms = jnp.mean(x * x, axis=-1, keepdims=True)
    return (x * lax.rsqrt(ms + NORM_EPS) * gain) * (1.0 + scale) + shift


def _ada_kernel(c_ref, w_ref, b_ref, o_ref):
    c = c_ref[...]
    cond = c * jax.nn.sigmoid(c)
    o_ref[...] = _dot_hi(cond, w_ref[...]) + b_ref[...]


def ada_mod(c, ada_w, ada_b):
    depth, d, n = ada_w.shape
    b = c.shape[0]
    tn = 1536
    return pl.pallas_call(
        _ada_kernel,
        grid=(depth, n // tn),
        in_specs=[pl.BlockSpec((b, d), lambda l, j: (0, 0)),
                  pl.BlockSpec((None, d, tn), lambda l, j: (l, 0, j)),
                  pl.BlockSpec((None, 1, tn), lambda l, j: (l, 0, j))],
        out_specs=pl.BlockSpec((None, b, tn), lambda l, j: (l, 0, j)),
        out_shape=jax.ShapeDtypeStruct((depth, b, n), F32),
        compiler_params=_params("parallel", "parallel"),
        name="ada_mod",
    )(c, ada_w, ada_b.reshape(depth, 1, n))


def _mod_spec(d, layer, chunk):
    return pl.BlockSpec((None, None, 1, d), lambda b, *_: (layer, b, 0, chunk))


def _proj_kernel(n_out, x_ref, sh_ref, sc_ref, g_ref, *refs):
    h = _rms_mod(x_ref[0], g_ref[...], sc_ref[...], sh_ref[...]).astype(BF16)
    for w_ref, o_ref in zip(refs[:n_out], refs[n_out:]):
        o_ref[0] = _dot(h, w_ref[...])


def proj(x, mod4, layer, gain, weights, tm):
    b, l, d = x.shape
    n_out = len(weights)
    return pl.pallas_call(
        functools.partial(_proj_kernel, n_out),
        grid=(b, l // tm),
        in_specs=[pl.BlockSpec((1, tm, d), lambda bi, i: (bi, i, 0)),
                  _mod_spec(d, layer, 0), _mod_spec(d, layer, 1),
                  _const_spec((1, d))] + [_const_spec(w.shape) for w in weights],
        out_specs=[pl.BlockSpec((1, tm, w.shape[1]), lambda bi, i: (bi, i, 0)) for w in weights],
        out_shape=[jax.ShapeDtypeStruct((b, l, w.shape[1]), F32) for w in weights],
        compiler_params=_params("parallel", "parallel"),
        name="proj",
    )(x, mod4, mod4, gain.reshape(1, d), *weights)


def _headnorm_rope(x, gain, cos, sin):
    ms = jnp.mean(x * x, axis=-1, keepdims=True)
    y = x * lax.rsqrt(ms + NORM_EPS) * gain
    half = HEAD_DIM // 2
    rot = jnp.concatenate([-y[:, half:], y[:, :half]], axis=-1)
    return y * cos + rot * sin


def _attn_kernel(q_ref, kv_ref, cq_ref, sq_ref, ck_ref, sk_ref, qn_ref, kn_ref, o_ref, k_s, v_s):
    @pl.when(pl.program_id(1) == 0)
    def _():
        kv = kv_ref[0]
        for hk in range(ATTN_KV_HEADS):
            k = kv[:, HEAD_DIM * hk:HEAD_DIM * (hk + 1)]
            k_s[hk] = _headnorm_rope(k, kn_ref[...], ck_ref[...], sk_ref[...]).astype(BF16)
            v_s[hk] = kv[:, KV_WIDTH + HEAD_DIM * hk:KV_WIDTH + HEAD_DIM * (hk + 1)].astype(BF16)

    q = q_ref[0]
    group = ATTN_HEADS // ATTN_KV_HEADS
    outs = []
    for h in range(ATTN_HEADS):
        qh = _headnorm_rope(q[:, HEAD_DIM * h:HEAD_DIM * (h + 1)], qn_ref[...], cq_ref[...], sq_ref[...])
        qh = (qh * (HEAD_DIM ** -0.5)).astype(BF16)
        s = lax.dot_general(qh, k_s[h // group], (((1,), (1,)), ((), ())), preferred_element_type=F32)
        m = jnp.max(s, axis=-1, keepdims=True)
        p = jnp.exp(s - m)
        denom = jnp.sum(p, axis=-1, keepdims=True)
        o = _dot(p.astype(BF16), v_s[h // group])
        outs.append(o / denom)
    o_ref[0] = jnp.concatenate(outs, axis=-1)


def attn(q, kv, cos, sin, qn, kn, tq):
    b, l, _ = q.shape
    return pl.pallas_call(
        _attn_kernel,
        grid=(b, l // tq),
        in_specs=[pl.BlockSpec((1, tq, ATTN_WIDTH), lambda bi, i: (bi, i, 0)),
                  pl.BlockSpec((1, l, 2 * KV_WIDTH), lambda bi, i: (bi, 0, 0)),
                  pl.BlockSpec((tq, HEAD_DIM), lambda bi, i: (i, 0)),
                  pl.BlockSpec((tq, HEAD_DIM), lambda bi, i: (i, 0)),
                  _const_spec((l, HEAD_DIM)), _const_spec((l, HEAD_DIM)),
                  _const_spec((1, HEAD_DIM)), _const_spec((1, HEAD_DIM))],
        out_specs=pl.BlockSpec((1, tq, ATTN_WIDTH), lambda bi, i: (bi, i, 0)),
        out_shape=jax.ShapeDtypeStruct((b, l, ATTN_WIDTH), F32),
        scratch_shapes=[pltpu.VMEM((ATTN_KV_HEADS, l, HEAD_DIM), BF16),
                        pltpu.VMEM((ATTN_KV_HEADS, l, HEAD_DIM), BF16)],
        compiler_params=_params("parallel", "arbitrary"),
        name="attn",
    )(q, kv, cos, sin, cos, sin, qn, kn)


def _halo_specs(tt, w, l):
    nb = l // SUBLANES
    per = tt // SUBLANES
    main = pl.BlockSpec((1, tt, w), lambda bi, i: (bi, i, 0))
    prev = pl.BlockSpec((1, SUBLANES, w), lambda bi, i: (bi, jnp.maximum(i * per - 1, 0), 0))
    nxt = pl.BlockSpec((1, SUBLANES, w), lambda bi, i: (bi, jnp.minimum((i + 1) * per, nb - 1), 0))
    return [main, prev, nxt]


def _neighbours(cur, prow, nrow):
    tt = cur.shape[0]
    row = lax.broadcasted_iota(jnp.int32, cur.shape, 0)
    prev = jnp.where(row == 0, prow, pltpu.roll(cur, 1, 0))
    nxt = jnp.where(row == tt - 1, nrow, pltpu.roll(cur, tt - 1, 0))
    return prev, nxt


def _halo_rows(xp_ref, xn_ref, axis):
    i = pl.program_id(axis)
    last = pl.num_programs(axis) - 1
    prow = jnp.where(i > 0, xp_ref[0, SUBLANES - 1:SUBLANES, :], 0.0)
    nrow = jnp.where(i < last, xn_ref[0, 0:1, :], 0.0)
    return prow, nrow


def _rwkv_prep_kernel(x_ref, xp_ref, xn_ref, mu_ref, w0_ref, wup_ref, a0_ref, aup_ref, gup_ref,
                      kk_ref, ka_ref, rk_ref, seg_ref,
                      r_o, z_o, v_o, w0_o, w1_o, kd0_o, kd1_o, b0_o, b1_o, bonus_o, g_o):
    cur = x_ref[0]
    prev, nxt = _neighbours(cur, *_halo_rows(xp_ref, xn_ref, 1))
    ps = cur + (0.5 * (prev + nxt) - cur) * mu_ref[...]
    c = RWKV_WIDTH
    r = ps[:, 0:c]
    k = ps[:, c:2 * c]
    v = ps[:, 2 * c:3 * c]
    o_a = 3 * c + 2 * DECAY_LORA
    o_g = o_a + 2 * AAA_LORA
    kk = k * kk_ref[...]
    kk = kk * lax.rsqrt(jnp.maximum(_seg_sum(kk * kk, seg_ref[...]), 1e-24))
    r_o[...] = r
    v_o[...] = v
    z_o[...] = -kk
    for d, (w_o, kd_o, b_o) in enumerate(((w0_o, kd0_o, b0_o), (w1_o, kd1_o, b1_o))):
        w_lo = ps[:, 3 * c + DECAY_LORA * d:3 * c + DECAY_LORA * (d + 1)]
        a_lo = ps[:, o_a + AAA_LORA * d:o_a + AAA_LORA * (d + 1)]
        decay = jnp.exp(-math.exp(-0.5) * jax.nn.sigmoid(w0_ref[d] + _dot_split(jnp.tanh(w_lo), wup_ref[d])))
        a = jax.nn.sigmoid(a0_ref[d] + _dot_split(a_lo, aup_ref[d]))
        w_o[...] = decay
        kd_o[...] = k * (1.0 + (a - 1.0) * ka_ref[...])
        b_o[...] = kk * a
    bonus_o[0] = _seg_sum(r * k * rk_ref[...], seg_ref[...]) * v
    g_o[0] = _dot_split(jax.nn.sigmoid(ps[:, o_g:RWKV_IN_PAD]), gup_ref[...])


def rwkv_prep(rw, mu, w0, w_up, a0, a_up, g_up, k_k, k_a, r_k, seg, tt):
    b, l, w = rw.shape
    c = RWKV_WIDTH
    natural = jax.ShapeDtypeStruct((b, l, c), F32)
    time_major = jax.ShapeDtypeStruct((l, b * c), F32)
    spec_n = pl.BlockSpec((1, tt, c), lambda bi, i: (bi, i, 0))
    spec_t = pl.BlockSpec((tt, c), lambda bi, i: (i, bi))
    consts = [mu, w0, w_up, a0, a_up, g_up, k_k, k_a, r_k, seg]
    return pl.pallas_call(
        _rwkv_prep_kernel,
        grid=(b, l // tt),
        in_specs=_halo_specs(tt, w, l) + [_const_spec(a.shape) for a in consts],
        out_specs=[spec_t] * 9 + [spec_n] * 2,
        out_shape=[time_major] * 9 + [natural] * 2,
        compiler_params=_params("parallel", "parallel"),
        name="rwkv_prep",
    )(rw, rw, rw, *consts)


SCAN_ROWS = 32
_R, _W, _KD, _V, _Z, _B = range(6)


def _rwkv_scan_kernel(*refs):
    ins, (yf_ref, yb_ref, s_ref, m_ref, sz_ref) = refs[:12], refs[12:]
    tb, n, lanes = ins[0].shape
    fwd_lane = lax.broadcasted_iota(jnp.int32, (n, lanes), 1) < lanes // 2

    def stage(which, t):
        m_ref[which] = jnp.where(fwd_lane, ins[2 * which][t], ins[2 * which + 1][tb - 1 - t])

    @pl.when(pl.program_id(0) == 0)
    def _():
        s_ref[...] = jnp.zeros_like(s_ref)

    halves = [slice(h * SCAN_ROWS, (h + 1) * SCAN_ROWS) for h in range(n // SCAN_ROWS)]

    stage(_Z, 0)
    for rows in halves:
        acc = s_ref[0, rows, :] * m_ref[_Z, 0:1, :]
        for k in range(1, n):
            acc = acc + s_ref[k, rows, :] * m_ref[_Z, k:k + 1, :]
        sz_ref[rows, :] = acc

    def step(t, carry):
        for which in (_R, _W, _KD, _V, _B):
            stage(which, t)
        stage(_Z, jnp.minimum(t + 1, tb - 1))
        for rows in halves:
            sz = sz_ref[rows, :]
            vt = m_ref[_V, rows, :]
            y = None
            sz_next = None
            for k in range(n):
                sk = (s_ref[k, rows, :] * m_ref[_W, k:k + 1, :] + sz * m_ref[_B, k:k + 1, :]
                      + vt * m_ref[_KD, k:k + 1, :])
                s_ref[k, rows, :] = sk
                yk = sk * m_ref[_R, k:k + 1, :]
                zk = sk * m_ref[_Z, k:k + 1, :]
                y = yk if y is None else y + yk
                sz_next = zk if sz_next is None else sz_next + zk
            sz_ref[rows, :] = sz_next
            yf_ref[t, rows, :] = y
            yb_ref[tb - 1 - t, rows, :] = y
        return carry

    lax.fori_loop(0, tb, step, 0)


def rwkv_scan(arrays, tb):
    l, n, lanes = arrays[0].shape
    nt = l // tb
    fwd = pl.BlockSpec((tb, n, lanes), lambda i: (i, 0, 0))
    rev = pl.BlockSpec((tb, n, lanes), lambda i: (nt - 1 - i, 0, 0))
    out = jax.ShapeDtypeStruct((l, n, lanes), F32)
    return pl.pallas_call(
        _rwkv_scan_kernel,
        grid=(nt,),
        in_specs=[fwd, rev] * 6,
        out_specs=[fwd, rev],
        out_shape=[out, out],
        scratch_shapes=[pltpu.VMEM((n, n, lanes), F32), pltpu.VMEM((6, n, lanes), F32),
                        pltpu.VMEM((n, lanes), F32)],
        compiler_params=_params("arbitrary"),
        name="rwkv_scan",
    )(*[a for a in arrays for _ in range(2)])


def _mix_out_kernel(rwkv, *refs):
    if rwkv:
        (ya_ref, yf_ref, yb_ref, bonus_ref, g_ref, lng_ref, lnb_ref, seg_ref,
         x_ref, gate_ref, w_ref, o_ref) = refs
        y = yf_ref[...] + yb_ref[...]
        inv_n = 1.0 / RWKV_HEAD
        mean = _seg_sum(y, seg_ref[...]) * inv_n
        cen = y - mean
        var = _seg_sum(cen * cen, seg_ref[...]) * inv_n
        yn = cen * lax.rsqrt(var + GN_EPS) * lng_ref[...] + lnb_ref[...]
        yr = (yn + bonus_ref[0]) * g_ref[0]
        a = jnp.concatenate([ya_ref[0], yr], axis=-1)
    else:
        a_ref, x_ref, gate_ref, w_ref, o_ref = refs
        a = a_ref[0]
    o_ref[0] = x_ref[0] + gate_ref[...] * _dot(a.astype(BF16), w_ref[...])


def mix_out(acts, consts, x, mod4, layer, w, tm, rwkv):
    b, l, d = x.shape
    in_specs = []
    for i_act, a in enumerate(acts):
        if isinstance(a, tuple):
            arr, direction = a
            width = arr.shape[1] // (2 * b)
            in_specs.append(pl.BlockSpec((tm, width), lambda bi, i, dr=direction: (i, dr * b + bi)))
            acts[i_act] = arr
        else:
            in_specs.append(pl.BlockSpec((1, tm, a.shape[2]), lambda bi, i: (bi, i, 0)))
    in_specs += [_const_spec(a.shape) for a in consts]
    in_specs += [pl.BlockSpec((1, tm, d), lambda bi, i: (bi, i, 0)), _mod_spec(d, layer, 2), _const_spec(w.shape)]
    return pl.pallas_call(
        functools.partial(_mix_out_kernel, rwkv),
        grid=(b, l // tm),
        in_specs=in_specs,
        out_specs=pl.BlockSpec((1, tm, d), lambda bi, i: (bi, i, 0)),
        out_shape=jax.ShapeDtypeStruct((b, l, d), F32),
        compiler_params=_params("parallel", "parallel"),
        name="mix_out",
    )(*acts, *consts, x, mod4, w)


def _ffn_kernel(final, x_ref, sh_ref, sc_ref, gate_ref, g_ref, w1_ref, w3_ref, w2_ref, fg_ref, o_ref):
    x = x_ref[0]
    h = _rms_mod(x, g_ref[...], sc_ref[...], sh_ref[...]).astype(BF16)
    a = _dot(h, w1_ref[...])
    u = (a * jax.nn.sigmoid(a)) * _dot(h, w3_ref[...])
    out = x + gate_ref[...] * _dot(u.astype(BF16), w2_ref[...])
    if final:
        ms = jnp.mean(out * out, axis=-1, keepdims=True)
        out = out * lax.rsqrt(ms + NORM_EPS) * fg_ref[...]
    o_ref[0] = out


def ffn(x, mod4, layer, gain, w1, w3, w2, final_gain, tm, final):
    b, l, d = x.shape
    return pl.pallas_call(
        functools.partial(_ffn_kernel, final),
        grid=(b, l // tm),
        in_specs=[pl.BlockSpec((1, tm, d), lambda bi, i: (bi, i, 0)),
                  _mod_spec(d, layer, 3), _mod_spec(d, layer, 4), _mod_spec(d, layer, 5),
                  _const_spec((1, d)), _const_spec(w1.shape), _const_spec(w3.shape), _const_spec(w2.shape),
                  _const_spec((1, d))],
        out_specs=pl.BlockSpec((1, tm, d), lambda bi, i: (bi, i, 0)),
        out_shape=jax.ShapeDtypeStruct((b, l, d), F32),
        compiler_params=_params("parallel", "parallel"),
        name="ffn",
    )(x, mod4, mod4, mod4, gain.reshape(1, d), w1, w3, w2, final_gain.reshape(1, d))


def _short_conv_kernel(x_ref, w_ref, b_ref, o_ref):
    cur = x_ref[0]
    prev, nxt = _neighbours(cur, 0.0, 0.0)
    o_ref[0] = prev * w_ref[0:1, :] + cur * w_ref[1:2, :] + nxt * w_ref[2:3, :] + b_ref[...]


def short_conv(p, conv_w, conv_b, tw):
    b, l, w = p.shape
    return pl.pallas_call(
        _short_conv_kernel,
        grid=(b, w // tw),
        in_specs=[pl.BlockSpec((1, l, tw), lambda bi, j: (bi, 0, j)),
                  pl.BlockSpec((3, tw), lambda bi, j: (0, j)),
                  pl.BlockSpec((1, tw), lambda bi, j: (0, j))],
        out_specs=pl.BlockSpec((1, l, tw), lambda bi, j: (bi, 0, j)),
        out_shape=jax.ShapeDtypeStruct((b, l, w), F32),
        compiler_params=_params("parallel", "parallel"),
        name="short_conv",
    )(p, conv_w, conv_b.reshape(1, w))


def _hy_filter_kernel(d_tiles, feats_ref, w1_ref, b1_ref, w2_ref, b2_ref, w3_ref, b3_ref, fr_ref,
                      wout_ref, win_ref, o_ref, hid_s):
    j = pl.program_id(0)

    @pl.when(j == 0)
    def _():
        fr = fr_ref[...]
        hid = jnp.sin(fr * (_dot_hi(feats_ref[...], w1_ref[...]) + b1_ref[...]))
        hid = jnp.sin(fr * (_dot_hi(hid, w2_ref[...]) + b2_ref[...]))
        hid_s[...] = jnp.sin(fr * (_dot_hi(hid, w3_ref[...]) + b3_ref[...]))

    filt = _dot_hi(hid_s[...], wout_ref[...]) * win_ref[...]
    backward = (j // d_tiles) % 2 == 1
    row = lax.broadcasted_iota(jnp.int32, filt.shape, 0)
    o_ref[...] = jnp.where(jnp.logical_and(backward, row == 0), 0.0, filt)


def hy_filter(feats, w1, b1, w2, b2, w3, b3, fr, w_out, window, tn):
    l = feats.shape[0]
    hf = w2.shape[0]
    n = w_out.shape[1]
    d = window.shape[1]
    d_tiles = d // tn
    consts = [feats, w1, b1.reshape(1, hf), w2, b2.reshape(1, hf), w3, b3.reshape(1, hf), fr.reshape(1, hf)]
    return pl.pallas_call(
        functools.partial(_hy_filter_kernel, d_tiles),
        grid=(n // tn,),
        in_specs=[_const_spec(a.shape) for a in consts]
        + [pl.BlockSpec((hf, tn), lambda j: (0, j)), pl.BlockSpec((l, tn), lambda j: (0, j % d_tiles))],
        out_specs=pl.BlockSpec((l, tn), lambda j: (0, j)),
        out_shape=jax.ShapeDtypeStruct((l, n), F32),
        scratch_shapes=[pltpu.VMEM((l, hf), F32)],
        compiler_params=_params("arbitrary"),
        name="hy_filter",
    )(*consts, w_out, window)


def _dft_matrices(l):
    n = 2 * l
    lane = min(128, l)
    k = jnp.arange(l, dtype=jnp.int32)[:, None]
    t = jnp.arange(l, dtype=jnp.int32)
    a = ((k * jnp.arange(lane, dtype=jnp.int32)[None, :]) % n).astype(F32) * (2.0 * math.pi / n)
    c = ((k * (lane * jnp.arange(l // lane, dtype=jnp.int32))[None, :]) % n).astype(F32) * (2.0 * math.pi / n)
    ca, sa, cc, sc = jnp.cos(a)[:, None, :], jnp.sin(a)[:, None, :], jnp.cos(c)[:, :, None], jnp.sin(c)[:, :, None]
    cos = (ca * cc - sa * sc).reshape(l, l)
    sin = (sa * cc + ca * sc).reshape(l, l)
    alt = jnp.where(t % 2 == 0, 1.0, -1.0).astype(F32)
    f_re = cos.astype(BF16)
    f_im = jnp.where(k == 0, alt[None, :], -sin).astype(BF16)
    col0 = (t == 0)[None, :]
    g_re = jnp.where(col0, 1.0 / n, (2.0 / n) * cos).astype(BF16)
    g_im = jnp.where(col0, alt[:, None] / n, (-2.0 / n) * sin).astype(BF16)
    return (f_re, f_im), (g_re, g_im)


def _nyquist_slot(i, shape):
    row = lax.broadcasted_iota(jnp.int32, shape, 0)
    return jnp.logical_and(i == 0, row == 0)


def _dft_spec_kernel(fre_ref, fim_ref, ft_ref, fb_ref, bt_ref, bb_ref, o_ref):
    fre, fim = fre_ref[...], fim_ref[...]
    ft, fb, bt, bb = (ref[...].astype(BF16) for ref in (ft_ref, fb_ref, bt_ref, bb_ref))
    af_r, af_i = _dot(fre, ft), _dot(fim, ft)
    bf_r, bf_i = _dot(fre, fb), _dot(fim, fb)
    ab_r, ab_i = _dot(fre, bt), _dot(fim, bt)
    bb_r, bb_i = _dot(fre, bb), _dot(fim, bb)
    nyq = _nyquist_slot(pl.program_id(1), af_r.shape)
    row = lax.broadcasted_iota(jnp.int32, af_r.shape, 0)
    sgn = jnp.where(row % 2 == 0, 1.0, -1.0)
    lag0 = ft_ref[0:1, :]
    o_ref[0, 0] = af_r + ab_r
    o_ref[0, 1] = jnp.where(nyq, af_i + ab_i, af_i - ab_i)
    o_ref[1, 0] = bf_r + sgn * (af_r - lag0)
    o_ref[1, 1] = bf_i + jnp.where(nyq, af_i - lag0, sgn * af_i)
    o_ref[2, 0] = sgn * ab_r + bb_r
    o_ref[2, 1] = jnp.where(nyq, ab_i + bb_i, -(sgn * ab_i + bb_i))


def dft_spec(fwd, taps, d, ft, tn):
    f_re, f_im = fwd
    m = f_re.shape[0]
    d_tiles = d // tn
    orders = taps.shape[1] // (2 * d)
    f_spec = pl.BlockSpec((ft, m), lambda j, i: (i, 0))

    def tap_spec(half, direction):
        return pl.BlockSpec((m, tn), lambda j, i: (half, (j // d_tiles) * 2 * d_tiles + direction * d_tiles
                                                    + j % d_tiles))

    return pl.pallas_call(
        _dft_spec_kernel,
        grid=(orders * d_tiles, m // ft),
        in_specs=[f_spec, f_spec, tap_spec(0, 0), tap_spec(1, 0), tap_spec(0, 1), tap_spec(1, 1)],
        out_specs=pl.BlockSpec((3, 2, ft, tn), lambda j, i: (0, 0, i, j)),
        out_shape=jax.ShapeDtypeStruct((3, 2, m, orders * d), F32),
        compiler_params=_params("parallel", "parallel"),
        name="dft_spec",
    )(f_re, f_im, taps, taps, taps, taps)


def _dft_mul_kernel(fre_ref, fim_ref, u_ref, h_ref, o_ref, u_s):
    i = pl.program_id(2)

    @pl.when(i == 0)
    def _():
        u_s[...] = u_ref[0].astype(BF16)

    m = u_s.shape[0] // 2
    fre, fim = fre_ref[...], fim_ref[...]
    tr, ti = _dot(fre, u_s[0:m]), _dot(fim, u_s[0:m])
    br, bi = _dot(fre, u_s[m:]), _dot(fim, u_s[m:])
    h0r, h0i, hpr, hpi, hmr, hmi = (h_ref[a, p] for a in range(3) for p in range(2))
    head = 2 * SUBLANES
    nyq = _nyquist_slot(i, (head, tr.shape[1]))

    def emit(half, plane, general, packed):
        o_ref[0, half, plane, 0:head] = jnp.where(nyq, packed, general[0:head]).astype(BF16)
        o_ref[0, half, plane, head:] = general[head:].astype(BF16)

    def top(x):
        return x[0:head]

    emit(0, 0, h0r * tr - h0i * ti + hmr * br - hmi * bi, top(h0r) * top(tr) + top(hmr) * top(br))
    emit(0, 1, h0r * ti + h0i * tr + hmr * bi + hmi * br, top(h0i) * top(ti) + top(hmi) * top(bi))
    emit(1, 0, hpr * tr - hpi * ti + h0r * br - h0i * bi, top(hpr) * top(tr) + top(h0r) * top(br))
    emit(1, 1, hpr * ti + hpi * tr + h0r * bi + h0i * br, top(hpi) * top(ti) + top(h0i) * top(bi))


def dft_mul(fwd, u, u_col, spec, spec_col, d, ft, tn):
    f_re, f_im = fwd
    m = f_re.shape[0]
    b = u.shape[0]
    d_tiles = d // tn
    f_spec = pl.BlockSpec((ft, m), lambda bi, j, i: (i, 0))
    return pl.pallas_call(
        _dft_mul_kernel,
        grid=(b, d_tiles, m // ft),
        in_specs=[f_spec, f_spec,
                  pl.BlockSpec((1, 2 * m, tn), lambda bi, j, i: (bi, 0, u_col * d_tiles + j)),
                  pl.BlockSpec((3, 2, ft, tn), lambda bi, j, i: (0, 0, i, spec_col * d_tiles + j))],
        out_specs=pl.BlockSpec((1, 2, 2, ft, tn), lambda bi, j, i: (bi, 0, 0, i, j)),
        out_shape=jax.ShapeDtypeStruct((b, 2, 2, m, d), BF16),
        scratch_shapes=[pltpu.VMEM((2 * m, tn), BF16)],
        compiler_params=_params("parallel", "parallel", "arbitrary"),
        name="dft_mul",
    )(f_re, f_im, u, spec)


def _dft_inv_kernel(gre_ref, gim_ref, y_ref, u_ref, gate_ref, skip_ref, o_ref):
    conv = _dot(gre_ref[...], y_ref[0, 0, 0]) + _dot(gim_ref[...], y_ref[0, 0, 1])
    o_ref[0] = gate_ref[0] * (conv + u_ref[0] * skip_ref[...])


def dft_inv(inv, y, u, u_col, gate, gate_col, skip, tm, tn):
    g_re, g_im = inv
    m = g_re.shape[0]
    b, _, _, _, d = y.shape
    d_tiles = d // tn
    per_half = m // tm
    g_spec = pl.BlockSpec((tm, m), lambda bi, j, i: (i % per_half, 0))
    return pl.pallas_call(
        _dft_inv_kernel,
        grid=(b, d_tiles, 2 * per_half),
        in_specs=[g_spec, g_spec,
                  pl.BlockSpec((1, 1, 2, m, tn), lambda bi, j, i: (bi, i // per_half, 0, 0, j)),
                  pl.BlockSpec((1, tm, tn), lambda bi, j, i: (bi, i, u_col * d_tiles + j)),
                  pl.BlockSpec((1, tm, tn), lambda bi, j, i: (bi, i, gate_col * d_tiles + j)),
                  pl.BlockSpec((1, tn), lambda bi, j, i: (0, j))],
        out_specs=pl.BlockSpec((1, tm, tn), lambda bi, j, i: (bi, i, j)),
        out_shape=jax.ShapeDtypeStruct((b, 2 * m, d), F32),
        compiler_params=_params("parallel", "parallel", "parallel"),
        name="dft_inv",
    )(g_re, g_im, y, u, gate, skip.reshape(1, d))


def _rope_tables(l):
    rows = l // GRID_W
    row = jnp.repeat(jnp.arange(rows), GRID_W).astype(F32)
    col = jnp.tile(jnp.arange(GRID_W), rows).astype(F32)
    half = HEAD_DIM // 2
    inv_freq = ROPE_THETA ** (-jnp.arange(0, half, 2, dtype=F32) / half)
    ang = jnp.concatenate([row[:, None] * inv_freq, col[:, None] * inv_freq], axis=-1)
    cos, sin = jnp.cos(ang), jnp.sin(ang)
    return jnp.concatenate([cos, cos], axis=-1), jnp.concatenate([sin, sin], axis=-1)


def _hyena_tables(l, d):
    t = jnp.linspace(0.0, 1.0, l, dtype=F32)[:, None]
    omega = (2.0 * math.pi / l) * jnp.arange(l, dtype=F32)[:, None]
    bands = jnp.linspace(1e-4, HYENA_BANDS - 1, HYENA_BANDS, dtype=F32)[None, :]
    feats = jnp.concatenate([t, jnp.cos(bands * omega), -jnp.sin(bands * omega)], axis=-1)
    min_decay = math.log(HYENA_TARGET) / HYENA_SLOW_DECAY
    max_decay = math.log(HYENA_TARGET) / HYENA_FAST_DECAY
    deltas = jnp.abs(jnp.linspace(min_decay, max_decay, d, dtype=F32))
    window = jnp.exp(-t * deltas)
    return feats, window


def _to_scan_layout(fwd, bwd):
    l = fwd.shape[0]
    x = jnp.concatenate([fwd.reshape(l, -1, RWKV_HEAD), bwd.reshape(l, -1, RWKV_HEAD)], axis=1)
    return x.transpose(0, 2, 1)


def _from_scan_layout(y):
    return y.transpose(0, 2, 1).reshape(y.shape[0], -1)


def kernel(x, c, mix_w_in, mix_w_out, attn_q_norm, attn_k_norm, rwkv_mu, rwkv_w0, rwkv_w_up, rwkv_a0, rwkv_a_up, rwkv_g_up, rwkv_k_k, rwkv_k_a, rwkv_r_k, rwkv_ln_g, rwkv_ln_b, hy_w_in, hy_conv_w, hy_conv_b, hy_f_w1, hy_f_b1, hy_f_w2, hy_f_b2, hy_f_w3, hy_f_b3, hy_sin_freq, hy_f_out, hy_skip, hy_w_out, ada_w, ada_b, norm_mix, norm_ffn, ffn_w1, ffn_w3, ffn_w2, final_norm):
    b, l, d = x.shape
    depth = ada_w.shape[0]
    tm = min(256, l)
    c_w = RWKV_WIDTH

    mod4 = ada_mod(c, ada_w, ada_b).reshape(depth, b, 1, 6 * d)

    perm = jnp.concatenate([jnp.arange(0, HEAD_DIM, 2), jnp.arange(1, HEAD_DIM, 2)])
    w_in = mix_w_in[0]
    w_q = w_in[:, :ATTN_WIDTH].reshape(d, ATTN_HEADS, HEAD_DIM)[:, :, perm].reshape(d, ATTN_WIDTH)
    w_kv = w_in[:, ATTN_WIDTH:ATTN_WIDTH + 2 * KV_WIDTH]
    w_k = w_kv[:, :KV_WIDTH].reshape(d, ATTN_KV_HEADS, HEAD_DIM)[:, :, perm].reshape(d, KV_WIDTH)
    w_kv = jnp.concatenate([w_k, w_kv[:, KV_WIDTH:]], axis=1)
    w_rw = jnp.pad(w_in[:, ATTN_WIDTH + 2 * KV_WIDTH:], ((0, 0), (0, RWKV_IN_PAD - RWKV_IN)))
    q, kv, rw = proj(x, mod4, 0, norm_mix[0], [w_q.astype(BF16), w_kv.astype(BF16), w_rw.astype(BF16)], tm)

    cos, sin = _rope_tables(l)
    y_attn = attn(q, kv, cos, sin, attn_q_norm[0][perm].reshape(1, HEAD_DIM),
                  attn_k_norm[0][perm].reshape(1, HEAD_DIM), tm)

    seg = jnp.kron(jnp.eye(RWKV_HEADS, dtype=BF16), jnp.ones((RWKV_HEAD, RWKV_HEAD), BF16))
    r, z, v, w0, w1, kd0, kd1, b0, b1, bonus, g = rwkv_prep(
        rw, jnp.pad(rwkv_mu[0], (0, RWKV_IN_PAD - RWKV_IN)).reshape(1, RWKV_IN_PAD),
        rwkv_w0[0].reshape(2, 1, c_w), rwkv_w_up[0], rwkv_a0[0].reshape(2, 1, c_w), rwkv_a_up[0],
        jnp.pad(rwkv_g_up[0], ((0, RWKV_IN_PAD - RWKV_IN), (0, 0))),
        rwkv_k_k[0].reshape(1, c_w), rwkv_k_a[0].reshape(1, c_w), rwkv_r_k[0].reshape(1, c_w), seg, tm)
    pairs = ((r, r), (w0, w1), (kd0, kd1), (v, v), (z, z), (b0, b1))
    y_f, y_b = rwkv_scan([_to_scan_layout(f, bw) for f, bw in pairs], min(SCAN_BLOCK, l))
    x = mix_out([y_attn, (_from_scan_layout(y_f), 0), (_from_scan_layout(y_b), 1), bonus, g],
                [rwkv_ln_g[0].reshape(1, c_w), rwkv_ln_b[0].reshape(1, c_w), seg],
                x, mod4, 0, mix_w_out[0].astype(BF16), tm, True)
    x = ffn(x, mod4, 0, norm_ffn[0], ffn_w1[0].astype(BF16), ffn_w3[0].astype(BF16), ffn_w2[0].astype(BF16),
            final_norm, tm, False)

    (p3,) = proj(x, mod4, 1, norm_mix[1], [hy_w_in[0].astype(BF16)], tm)
    tn = min(512, d)
    p3 = short_conv(p3, hy_conv_w[0], hy_conv_b[0], min(256, d))
    feats, window = _hyena_tables(l, d)
    k_pad = 128
    feats = jnp.pad(feats, ((0, 0), (0, k_pad - feats.shape[1])))
    f_w1 = jnp.pad(hy_f_w1[0], ((0, k_pad - hy_f_w1.shape[1]), (0, 0)))
    taps = hy_filter(feats, f_w1, hy_f_b1[0], hy_f_w2[0], hy_f_b2[0], hy_f_w3[0], hy_f_b3[0],
                     hy_sin_freq[0], hy_f_out[0], window, tn)
    fwd, inv = _dft_matrices(l // 2)
    ft = min(FREQ_TILE, l // 2)
    spec = dft_spec(fwd, taps, d, ft, tn)
    y1 = dft_mul(fwd, p3, 0, spec, 0, d, ft, tn)
    z1 = dft_inv(inv, y1, p3, 0, p3, 1, hy_skip[0, 0], ft, tn)
    y2 = dft_mul(fwd, z1, 0, spec, 1, d, ft, tn)
    z2 = dft_inv(inv, y2, z1, 0, p3, 2, hy_skip[0, 1], ft, tn)
    x = mix_out([z2], [], x, mod4, 1, hy_w_out[0].astype(BF16), tm, False)
    x = ffn(x, mod4, 1, norm_ffn[1], ffn_w1[1].astype(BF16), ffn_w3[1].astype(BF16), ffn_w2[1].astype(BF16),
            final_norm, tm, True)
    return x
```

```python
import functools
import math

import jax
import jax.numpy as jnp
from jax import lax
from jax.experimental import pallas as pl
from jax.experimental.pallas import tpu as pltpu

F32 = jnp.float32
BF16 = jnp.bfloat16
HIGHEST = lax.Precision.HIGHEST

GRID_W = 64
HEAD_DIM = 64
ATTN_HEADS = 8
ATTN_KV_HEADS = 2
ATTN_WIDTH = ATTN_HEADS * HEAD_DIM
KV_WIDTH = ATTN_KV_HEADS * HEAD_DIM
ROPE_THETA = 10000.0
RWKV_HEADS = 8
RWKV_HEAD = 64
RWKV_WIDTH = RWKV_HEADS * RWKV_HEAD
DECAY_LORA = 64
AAA_LORA = 64
GATE_LORA = 160
RWKV_IN = 3 * RWKV_WIDTH + 2 * DECAY_LORA + 2 * AAA_LORA + GATE_LORA
RWKV_IN_PAD = 2048
GN_EPS = 64e-5
HYENA_BANDS = 16
HYENA_TARGET = 1e-2
HYENA_FAST_DECAY = 0.3
HYENA_SLOW_DECAY = 1.5
NORM_EPS = 1e-6

V7X_VMEM_BYTES = 64 * 1024 * 1024
VMEM_LIMIT = V7X_VMEM_BYTES - 8 * 1024 * 1024
FREQ_TILE = 512
SCAN_BLOCK = 32
SUBLANES = 8


def _params(*sem):
    return pltpu.CompilerParams(dimension_semantics=sem, vmem_limit_bytes=VMEM_LIMIT)


def _const_spec(shape):
    zeros = (0,) * len(shape)
    return pl.BlockSpec(shape, lambda *_: zeros, pipeline_mode=pl.Buffered(1))


def _dot(a, b):
    return jnp.dot(a, b, preferred_element_type=F32)


def _split(x):
    hi = x.astype(BF16)
    return hi, (x - hi.astype(F32)).astype(BF16)


def _dot_split(a, b):
    ah, al = _split(a)
    bh, bl = _split(b)
    return _dot(ah, bh) + (_dot(ah, bl) + _dot(al, bh))


def _seg_sum(a, seg):
    ah, al = _split(a)
    return _dot(ah, seg) + _dot(al, seg)


def _dot_hi(a, b):
    return jnp.dot(a, b, precision=HIGHEST, preferred_element_type=F32)


def _rms_mod(x, gain, scale, shift):
    ms = jnp.mean(x * x, axis=-1, keepdims=True)
    return (x * lax.rsqrt(ms + NORM_EPS) * gain) * (1.0 + scale) + shift


def _ada_kernel(c_ref, w_ref, b_ref, o_ref):
    c = c_ref[...]
    cond = c * jax.nn.sigmoid(c)
    o_ref[...] = _dot_hi(cond, w_ref[...]) + b_ref[...]


def ada_mod(c, ada_w, ada_b):
    depth, d, n = ada_w.shape
    b = c.shape[0]
    tn = 1536
    return pl.pallas_call(
        _ada_kernel,
        grid=(depth, n // tn),
        in_specs=[pl.BlockSpec((b, d), lambda l, j: (0, 0)),
                  pl.BlockSpec((None, d, tn), lambda l, j: (l, 0, j)),
                  pl.BlockSpec((None, 1, tn), lambda l, j: (l, 0, j))],
        out_specs=pl.BlockSpec((None, b, tn), lambda l, j: (l, 0, j)),
        out_shape=jax.ShapeDtypeStruct((depth, b, n), F32),
        compiler_params=_params("parallel", "parallel"),
        name="ada_mod",
    )(c, ada_w, ada_b.reshape(depth, 1, n))


def _mod_spec(d, layer, chunk):
    return pl.BlockSpec((None, None, 1, d), lambda b, *_: (layer, b, 0, chunk))


def _proj_kernel(n_out, x_ref, sh_ref, sc_ref, g_ref, *refs):
    h = _rms_mod(x_ref[0], g_ref[...], sc_ref[...], sh_ref[...]).astype(BF16)
    for w_ref, o_ref in zip(refs[:n_out], refs[n_out:]):
        o_ref[0] = _dot(h, w_ref[...])


def proj(x, mod4, layer, gain, weights, tm):
    b, l, d = x.shape
    n_out = len(weights)
    return pl.pallas_call(
        functools.partial(_proj_kernel, n_out),
        grid=(b, l // tm),
        in_specs=[pl.BlockSpec((1, tm, d), lambda bi, i: (bi, i, 0)),
                  _mod_spec(d, layer, 0), _mod_spec(d, layer, 1),
                  _const_spec((1, d))] + [_const_spec(w.shape) for w in weights],
        out_specs=[pl.BlockSpec((1, tm, w.shape[1]), lambda bi, i: (bi, i, 0)) for w in weights],
        out_shape=[jax.ShapeDtypeStruct((b, l, w.shape[1]), F32) for w in weights],
        compiler_params=_params("parallel", "parallel"),
        name="proj",
    )(x, mod4, mod4, gain.reshape(1, d), *weights)


def _headnorm_rope(x, gain, cos, sin):
    ms = jnp.mean(x * x, axis=-1, keepdims=True)
    y = x * lax.rsqrt(ms + NORM_EPS) * gain
    half = HEAD_DIM // 2
    rot = jnp.concatenate([-y[:, half:], y[:, :half]], axis=-1)
    return y * cos + rot * sin


def _attn_kernel(q_ref, kv_ref, cq_ref, sq_ref, ck_ref, sk_ref, qn_ref, kn_ref, o_ref, kt_s, v_s):
    @pl.when(pl.program_id(1) == 0)
    def _():
        kv = kv_ref[0]
        ones_col = (lax.broadcasted_iota(jnp.int32, (kv.shape[0], HEAD_DIM), 1) == 0).astype(F32)
        for hk in range(ATTN_KV_HEADS):
            k = kv[:, HEAD_DIM * hk:HEAD_DIM * (hk + 1)]
            kt_s[hk] = _headnorm_rope(k, kn_ref[...], ck_ref[...], sk_ref[...]).T.astype(BF16)
            v = kv[:, KV_WIDTH + HEAD_DIM * hk:KV_WIDTH + HEAD_DIM * (hk + 1)]
            v_s[hk] = jnp.concatenate([v, ones_col], axis=-1).astype(BF16)

    q = q_ref[0]
    group = ATTN_HEADS // ATTN_KV_HEADS
    outs = []
    for h in range(ATTN_HEADS):
        qh = _headnorm_rope(q[:, HEAD_DIM * h:HEAD_DIM * (h + 1)], qn_ref[...], cq_ref[...], sq_ref[...])
        qh = (qh * (HEAD_DIM ** -0.5 * math.log2(math.e))).astype(BF16)
        s = _dot(qh, kt_s[h // group])
        p = jnp.exp2(s - jnp.max(s, axis=-1, keepdims=True)).astype(BF16)
        o = _dot(p, v_s[h // group])
        outs.append(o[:, :HEAD_DIM] / o[:, HEAD_DIM:HEAD_DIM + 1])
    o_ref[0] = jnp.concatenate(outs, axis=-1)


def attn(q, kv, cos, sin, qn, kn, tq):
    b, l, _ = q.shape
    return pl.pallas_call(
        _attn_kernel,
        grid=(b, l // tq),
        in_specs=[pl.BlockSpec((1, tq, ATTN_WIDTH), lambda bi, i: (bi, i, 0)),
                  pl.BlockSpec((1, l, 2 * KV_WIDTH), lambda bi, i: (bi, 0, 0)),
                  pl.BlockSpec((tq, HEAD_DIM), lambda bi, i: (i, 0)),
                  pl.BlockSpec((tq, HEAD_DIM), lambda bi, i: (i, 0)),
                  _const_spec((l, HEAD_DIM)), _const_spec((l, HEAD_DIM)),
                  _const_spec((1, HEAD_DIM)), _const_spec((1, HEAD_DIM))],
        out_specs=pl.BlockSpec((1, tq, ATTN_WIDTH), lambda bi, i: (bi, i, 0)),
        out_shape=jax.ShapeDtypeStruct((b, l, ATTN_WIDTH), F32),
        scratch_shapes=[pltpu.VMEM((ATTN_KV_HEADS, HEAD_DIM, l), BF16),
                        pltpu.VMEM((ATTN_KV_HEADS, l, 2 * HEAD_DIM), BF16)],
        compiler_params=_params("parallel", "arbitrary"),
        name="attn",
    )(q, kv, cos, sin, cos, sin, qn, kn)


def _halo_specs(tt, w, l):
    nb = l // SUBLANES
    per = tt // SUBLANES
    main = pl.BlockSpec((1, tt, w), lambda bi, i: (bi, i, 0))
    prev = pl.BlockSpec((1, SUBLANES, w), lambda bi, i: (bi, jnp.maximum(i * per - 1, 0), 0))
    nxt = pl.BlockSpec((1, SUBLANES, w), lambda bi, i: (bi, jnp.minimum((i + 1) * per, nb - 1), 0))
    return [main, prev, nxt]


def _neighbours(cur, prow, nrow):
    tt = cur.shape[0]
    row = lax.broadcasted_iota(jnp.int32, cur.shape, 0)
    prev = jnp.where(row == 0, prow, pltpu.roll(cur, 1, 0))
    nxt = jnp.where(row == tt - 1, nrow, pltpu.roll(cur, tt - 1, 0))
    return prev, nxt


def _halo_rows(xp_ref, xn_ref, axis):
    i = pl.program_id(axis)
    last = pl.num_programs(axis) - 1
    prow = jnp.where(i > 0, xp_ref[0, SUBLANES - 1:SUBLANES, :], 0.0)
    nrow = jnp.where(i < last, xn_ref[0, 0:1, :], 0.0)
    return prow, nrow


def _rwkv_prep_kernel(x_ref, xp_ref, xn_ref, mu_ref, w0_ref, wup_ref, a0_ref, aup_ref, gup_ref,
                      kk_ref, ka_ref, rk_ref, seg_ref,
                      r_o, z_o, v_o, w_o, kd_o, b_o, bonus_o, g_o):
    cur = x_ref[0]
    prev, nxt = _neighbours(cur, *_halo_rows(xp_ref, xn_ref, 1))
    ps = cur + (0.5 * (prev + nxt) - cur) * mu_ref[...]
    c = RWKV_WIDTH
    r = ps[:, 0:c]
    k = ps[:, c:2 * c]
    v = ps[:, 2 * c:3 * c]
    o_a = 3 * c + 2 * DECAY_LORA
    o_g = o_a + 2 * AAA_LORA
    kk = k * kk_ref[...]
    kk = kk * lax.rsqrt(jnp.maximum(_seg_sum(kk * kk, seg_ref[...]), 1e-24))
    for d in range(2):
        cols = slice(c * d, c * (d + 1))
        r_o[:, cols] = r
        v_o[:, cols] = v
        z_o[:, cols] = -kk
        w_lo = ps[:, 3 * c + DECAY_LORA * d:3 * c + DECAY_LORA * (d + 1)]
        a_lo = ps[:, o_a + AAA_LORA * d:o_a + AAA_LORA * (d + 1)]
        decay = jnp.exp(-math.exp(-0.5) * jax.nn.sigmoid(w0_ref[d] + _dot_split(jnp.tanh(w_lo), wup_ref[d])))
        a = jax.nn.sigmoid(a0_ref[d] + _dot_split(a_lo, aup_ref[d]))
        w_o[:, cols] = decay
        kd_o[:, cols] = k * (1.0 + (a - 1.0) * ka_ref[...])
        b_o[:, cols] = kk * a
    bonus_o[0] = _seg_sum(r * k * rk_ref[...], seg_ref[...]) * v
    g_o[0] = _dot_split(jax.nn.sigmoid(ps[:, o_g:RWKV_IN_PAD]), gup_ref[...])


def rwkv_prep(rw, mu, w0, w_up, a0, a_up, g_up, k_k, k_a, r_k, seg, tt):
    b, l, w = rw.shape
    c = RWKV_WIDTH
    natural = jax.ShapeDtypeStruct((b, l, c), F32)
    time_major = jax.ShapeDtypeStruct((l, b * 2 * c), F32)
    spec_n = pl.BlockSpec((1, tt, c), lambda bi, i: (bi, i, 0))
    spec_t = pl.BlockSpec((tt, 2 * c), lambda bi, i: (i, bi))
    consts = [mu, w0, w_up, a0, a_up, g_up, k_k, k_a, r_k, seg]
    return pl.pallas_call(
        _rwkv_prep_kernel,
        grid=(b, l // tt),
        in_specs=_halo_specs(tt, w, l) + [_const_spec(a.shape) for a in consts],
        out_specs=[spec_t] * 6 + [spec_n] * 2,
        out_shape=[time_major] * 6 + [natural] * 2,
        compiler_params=_params("parallel", "parallel"),
        name="rwkv_prep",
    )(rw, rw, rw, *consts)


SCAN_ROWS = 32
_R, _W, _KD, _V, _Z, _B = range(6)


def _rwkv_scan_kernel(*refs):
    ins, (yf_ref, yb_ref, s_ref, m_ref, sz_ref) = refs[:12], refs[12:]
    tb, n, lanes = ins[0].shape
    fwd_lane = (lax.broadcasted_iota(jnp.int32, (n, lanes), 1) // RWKV_HEADS) % 2 == 0

    def stage(which, t):
        m_ref[which] = jnp.where(fwd_lane, ins[2 * which][t], ins[2 * which + 1][tb - 1 - t])

    @pl.when(pl.program_id(0) == 0)
    def _():
        s_ref[...] = jnp.zeros_like(s_ref)

    halves = [slice(h * SCAN_ROWS, (h + 1) * SCAN_ROWS) for h in range(n // SCAN_ROWS)]

    stage(_Z, 0)
    for rows in halves:
        acc = s_ref[0, rows, :] * m_ref[_Z, 0:1, :]
        for k in range(1, n):
            acc = acc + s_ref[k, rows, :] * m_ref[_Z, k:k + 1, :]
        sz_ref[rows, :] = acc

    def step(t, carry):
        for which in (_R, _W, _KD, _V, _B):
            stage(which, t)
        stage(_Z, jnp.minimum(t + 1, tb - 1))
        for rows in halves:
            sz = sz_ref[rows, :]
            vt = m_ref[_V, rows, :]
            y = None
            sz_next = None
            for k in range(n):
                sk = (s_ref[k, rows, :] * m_ref[_W, k:k + 1, :] + sz * m_ref[_B, k:k + 1, :]
                      + vt * m_ref[_KD, k:k + 1, :])
                s_ref[k, rows, :] = sk
                yk = sk * m_ref[_R, k:k + 1, :]
                zk = sk * m_ref[_Z, k:k + 1, :]
                y = yk if y is None else y + yk
                sz_next = zk if sz_next is None else sz_next + zk
            sz_ref[rows, :] = sz_next
            yf_ref[t, rows, :] = y
            yb_ref[tb - 1 - t, rows, :] = y
        return carry

    lax.fori_loop(0, tb, step, 0)


def rwkv_scan(arrays, tb):
    l, n, lanes = arrays[0].shape
    nt = l // tb
    fwd = pl.BlockSpec((tb, n, lanes), lambda i: (i, 0, 0))
    rev = pl.BlockSpec((tb, n, lanes), lambda i: (nt - 1 - i, 0, 0))
    out = jax.ShapeDtypeStruct((l, n, lanes), F32)
    return pl.pallas_call(
        _rwkv_scan_kernel,
        grid=(nt,),
        in_specs=[fwd, rev] * 6,
        out_specs=[fwd, rev],
        out_shape=[out, out],
        scratch_shapes=[pltpu.VMEM((n, n, lanes), F32), pltpu.VMEM((6, n, lanes), F32),
                        pltpu.VMEM((n, lanes), F32)],
        compiler_params=_params("arbitrary"),
        name="rwkv_scan",
    )(*[a for a in arrays for _ in range(2)])


def _mix_out_kernel(rwkv, *refs):
    if rwkv:
        (ya_ref, yf_ref, yb_ref, bonus_ref, g_ref, lng_ref, lnb_ref, seg_ref,
         x_ref, gate_ref, w_ref, o_ref) = refs
        y = yf_ref[...] + yb_ref[...]
        inv_n = 1.0 / RWKV_HEAD
        mean = _seg_sum(y, seg_ref[...]) * inv_n
        cen = y - mean
        var = _seg_sum(cen * cen, seg_ref[...]) * inv_n
        yn = cen * lax.rsqrt(var + GN_EPS) * lng_ref[...] + lnb_ref[...]
        yr = (yn + bonus_ref[0]) * g_ref[0]
        a = jnp.concatenate([ya_ref[0], yr], axis=-1)
    else:
        a_ref, x_ref, gate_ref, w_ref, o_ref = refs
        a = a_ref[0]
    o_ref[0] = x_ref[0] + gate_ref[...] * _dot(a.astype(BF16), w_ref[...])


def mix_out(acts, consts, x, mod4, layer, w, tm, rwkv):
    b, l, d = x.shape
    in_specs = []
    for i_act, a in enumerate(acts):
        if isinstance(a, tuple):
            arr, direction = a
            width = arr.shape[1] // (2 * b)
            in_specs.append(pl.BlockSpec((tm, width), lambda bi, i, dr=direction: (i, 2 * bi + dr)))
            acts[i_act] = arr
        else:
            in_specs.append(pl.BlockSpec((1, tm, a.shape[2]), lambda bi, i: (bi, i, 0)))
    in_specs += [_const_spec(a.shape) for a in consts]
    in_specs += [pl.BlockSpec((1, tm, d), lambda bi, i: (bi, i, 0)), _mod_spec(d, layer, 2), _const_spec(w.shape)]
    return pl.pallas_call(
        functools.partial(_mix_out_kernel, rwkv),
        grid=(b, l // tm),
        in_specs=in_specs,
        out_specs=pl.BlockSpec((1, tm, d), lambda bi, i: (bi, i, 0)),
        out_shape=jax.ShapeDtypeStruct((b, l, d), F32),
        compiler_params=_params("parallel", "parallel"),
        name="mix_out",
    )(*acts, *consts, x, mod4, w)


def _ffn_kernel(final, x_ref, sh_ref, sc_ref, gate_ref, g_ref, w1_ref, w3_ref, w2_ref, fg_ref, o_ref):
    x = x_ref[0]
    h = _rms_mod(x, g_ref[...], sc_ref[...], sh_ref[...]).astype(BF16)
    a = _dot(h, w1_ref[...])
    u = (a * jax.nn.sigmoid(a)) * _dot(h, w3_ref[...])
    out = x + gate_ref[...] * _dot(u.astype(BF16), w2_ref[...])
    if final:
        ms = jnp.mean(out * out, axis=-1, keepdims=True)
        out = out * lax.rsqrt(ms + NORM_EPS) * fg_ref[...]
    o_ref[0] = out


def ffn(x, mod4, layer, gain, w1, w3, w2, final_gain, tm, final):
    b, l, d = x.shape
    return pl.pallas_call(
        functools.partial(_ffn_kernel, final),
        grid=(b, l // tm),
        in_specs=[pl.BlockSpec((1, tm, d), lambda bi, i: (bi, i, 0)),
                  _mod_spec(d, layer, 3), _mod_spec(d, layer, 4), _mod_spec(d, layer, 5),
                  _const_spec((1, d)), _const_spec(w1.shape), _const_spec(w3.shape), _const_spec(w2.shape),
                  _const_spec((1, d))],
        out_specs=pl.BlockSpec((1, tm, d), lambda bi, i: (bi, i, 0)),
        out_shape=jax.ShapeDtypeStruct((b, l, d), F32),
        compiler_params=_params("parallel", "parallel"),
        name="ffn",
    )(x, mod4, mod4, mod4, gain.reshape(1, d), w1, w3, w2, final_gain.reshape(1, d))


def _short_conv_kernel(x_ref, w_ref, b_ref, o_ref):
    cur = x_ref[0]
    prev, nxt = _neighbours(cur, 0.0, 0.0)
    o_ref[0] = prev * w_ref[0:1, :] + cur * w_ref[1:2, :] + nxt * w_ref[2:3, :] + b_ref[...]


def short_conv(p, conv_w, conv_b, tw):
    b, l, w = p.shape
    return pl.pallas_call(
        _short_conv_kernel,
        grid=(b, w // tw),
        in_specs=[pl.BlockSpec((1, l, tw), lambda bi, j: (bi, 0, j)),
                  pl.BlockSpec((3, tw), lambda bi, j: (0, j)),
                  pl.BlockSpec((1, tw), lambda bi, j: (0, j))],
        out_specs=pl.BlockSpec((1, l, tw), lambda bi, j: (bi, 0, j)),
        out_shape=jax.ShapeDtypeStruct((b, l, w), F32),
        compiler_params=_params("parallel", "parallel"),
        name="short_conv",
    )(p, conv_w, conv_b.reshape(1, w))


def _hy_filter_kernel(d_tiles, feats_ref, w1_ref, b1_ref, w2_ref, b2_ref, w3_ref, b3_ref, fr_ref,
                      wout_ref, win_ref, o_ref, hid_s):
    j = pl.program_id(0)

    @pl.when(j == 0)
    def _():
        fr = fr_ref[...]
        hid = jnp.sin(fr * (_dot_hi(feats_ref[...], w1_ref[...]) + b1_ref[...]))
        hid = jnp.sin(fr * (_dot_hi(hid, w2_ref[...]) + b2_ref[...]))
        hid_s[...] = jnp.sin(fr * (_dot_hi(hid, w3_ref[...]) + b3_ref[...]))

    filt = _dot_hi(hid_s[...], wout_ref[...]) * win_ref[...]
    backward = (j // d_tiles) % 2 == 1
    row = lax.broadcasted_iota(jnp.int32, filt.shape, 0)
    o_ref[...] = jnp.where(jnp.logical_and(backward, row == 0), 0.0, filt)


def hy_filter(feats, w1, b1, w2, b2, w3, b3, fr, w_out, window, tn):
    l = feats.shape[0]
    hf = w2.shape[0]
    n = w_out.shape[1]
    d = window.shape[1]
    d_tiles = d // tn
    consts = [feats, w1, b1.reshape(1, hf), w2, b2.reshape(1, hf), w3, b3.reshape(1, hf), fr.reshape(1, hf)]
    return pl.pallas_call(
        functools.partial(_hy_filter_kernel, d_tiles),
        grid=(n // tn,),
        in_specs=[_const_spec(a.shape) for a in consts]
        + [pl.BlockSpec((hf, tn), lambda j: (0, j)), pl.BlockSpec((l, tn), lambda j: (0, j % d_tiles))],
        out_specs=pl.BlockSpec((l, tn), lambda j: (0, j)),
        out_shape=jax.ShapeDtypeStruct((l, n), F32),
        scratch_shapes=[pltpu.VMEM((l, hf), F32)],
        compiler_params=_params("arbitrary"),
        name="hy_filter",
    )(*consts, w_out, window)


def _dft_matrices(l):
    n = 2 * l
    lane = min(128, l)
    k = jnp.arange(l, dtype=jnp.int32)[:, None]
    t = jnp.arange(l, dtype=jnp.int32)
    a = ((k * jnp.arange(lane, dtype=jnp.int32)[None, :]) % n).astype(F32) * (2.0 * math.pi / n)
    c = ((k * (lane * jnp.arange(l // lane, dtype=jnp.int32))[None, :]) % n).astype(F32) * (2.0 * math.pi / n)
    ca, sa, cc, sc = jnp.cos(a)[:, None, :], jnp.sin(a)[:, None, :], jnp.cos(c)[:, :, None], jnp.sin(c)[:, :, None]
    cos = (ca * cc - sa * sc).reshape(l, l)
    sin = (sa * cc + ca * sc).reshape(l, l)
    alt = jnp.where(t % 2 == 0, 1.0, -1.0).astype(F32)
    f_re = cos.astype(BF16)
    f_im = jnp.where(k == 0, alt[None, :], -sin).astype(BF16)
    col0 = (t == 0)[None, :]
    g_re = jnp.where(col0, 1.0 / n, (2.0 / n) * cos).astype(BF16)
    g_im = jnp.where(col0, alt[:, None] / n, (-2.0 / n) * sin).astype(BF16)
    return (f_re, f_im), (g_re, g_im)


def _nyquist_slot(i, shape):
    row = lax.broadcasted_iota(jnp.int32, shape, 0)
    return jnp.logical_and(i == 0, row == 0)


def _dft_spec_kernel(fre_ref, fim_ref, ft_ref, fb_ref, bt_ref, bb_ref, o_ref):
    fre, fim = fre_ref[...], fim_ref[...]
    ft, fb, bt, bb = (ref[...].astype(BF16) for ref in (ft_ref, fb_ref, bt_ref, bb_ref))
    af_r, af_i = _dot(fre, ft), _dot(fim, ft)
    bf_r, bf_i = _dot(fre, fb), _dot(fim, fb)
    ab_r, ab_i = _dot(fre, bt), _dot(fim, bt)
    bb_r, bb_i = _dot(fre, bb), _dot(fim, bb)
    nyq = _nyquist_slot(pl.program_id(1), af_r.shape)
    row = lax.broadcasted_iota(jnp.int32, af_r.shape, 0)
    sgn = jnp.where(row % 2 == 0, 1.0, -1.0)
    lag0 = ft_ref[0:1, :]
    o_ref[0, 0] = af_r + ab_r
    o_ref[0, 1] = jnp.where(nyq, af_i + ab_i, af_i - ab_i)
    o_ref[1, 0] = bf_r + sgn * (af_r - lag0)
    o_ref[1, 1] = bf_i + jnp.where(nyq, af_i - lag0, sgn * af_i)
    o_ref[2, 0] = sgn * ab_r + bb_r
    o_ref[2, 1] = jnp.where(nyq, ab_i + bb_i, -(sgn * ab_i + bb_i))


def dft_spec(fwd, taps, d, ft, tn):
    f_re, f_im = fwd
    m = f_re.shape[0]
    d_tiles = d // tn
    orders = taps.shape[1] // (2 * d)
    f_spec = pl.BlockSpec((ft, m), lambda j, i: (i, 0))

    def tap_spec(half, direction):
        return pl.BlockSpec((m, tn), lambda j, i: (half, (j // d_tiles) * 2 * d_tiles + direction * d_tiles
                                                    + j % d_tiles))

    return pl.pallas_call(
        _dft_spec_kernel,
        grid=(orders * d_tiles, m // ft),
        in_specs=[f_spec, f_spec, tap_spec(0, 0), tap_spec(1, 0), tap_spec(0, 1), tap_spec(1, 1)],
        out_specs=pl.BlockSpec((3, 2, ft, tn), lambda j, i: (0, 0, i, j)),
        out_shape=jax.ShapeDtypeStruct((3, 2, m, orders * d), F32),
        compiler_params=_params("parallel", "parallel"),
        name="dft_spec",
    )(f_re, f_im, taps, taps, taps, taps)


def _dft_mul_kernel(fre_ref, fim_ref, u_ref, h_ref, o_ref, u_s):
    i = pl.program_id(2)

    @pl.when(i == 0)
    def _():
        u_s[...] = u_ref[0].astype(BF16)

    m = u_s.shape[0] // 2
    fre, fim = fre_ref[...], fim_ref[...]
    tr, ti = _dot(fre, u_s[0:m]), _dot(fim, u_s[0:m])
    br, bi = _dot(fre, u_s[m:]), _dot(fim, u_s[m:])
    h0r, h0i, hpr, hpi, hmr, hmi = (h_ref[a, p] for a in range(3) for p in range(2))
    head = 2 * SUBLANES
    nyq = _nyquist_slot(i, (head, tr.shape[1]))

    def emit(half, plane, general, packed):
        o_ref[0, half, plane, 0:head] = jnp.where(nyq, packed, general[0:head]).astype(BF16)
        o_ref[0, half, plane, head:] = general[head:].astype(BF16)

    def top(x):
        return x[0:head]

    emit(0, 0, h0r * tr - h0i * ti + hmr * br - hmi * bi, top(h0r) * top(tr) + top(hmr) * top(br))
    emit(0, 1, h0r * ti + h0i * tr + hmr * bi + hmi * br, top(h0i) * top(ti) + top(hmi) * top(bi))
    emit(1, 0, hpr * tr - hpi * ti + h0r * br - h0i * bi, top(hpr) * top(tr) + top(h0r) * top(br))
    emit(1, 1, hpr * ti + hpi * tr + h0r * bi + h0i * br, top(hpi) * top(ti) + top(h0i) * top(bi))


def dft_mul(fwd, u, u_col, spec, spec_col, d, ft, tn):
    f_re, f_im = fwd
    m = f_re.shape[0]
    b = u.shape[0]
    d_tiles = d // tn
    f_spec = pl.BlockSpec((ft, m), lambda bi, j, i: (i, 0))
    return pl.pallas_call(
        _dft_mul_kernel,
        grid=(b, d_tiles, m // ft),
        in_specs=[f_spec, f_spec,
                  pl.BlockSpec((1, 2 * m, tn), lambda bi, j, i: (bi, 0, u_col * d_tiles + j)),
                  pl.BlockSpec((3, 2, ft, tn), lambda bi, j, i: (0, 0, i, spec_col * d_tiles + j))],
        out_specs=pl.BlockSpec((1, 2, 2, ft, tn), lambda bi, j, i: (bi, 0, 0, i, j)),
        out_shape=jax.ShapeDtypeStruct((b, 2, 2, m, d), BF16),
        scratch_shapes=[pltpu.VMEM((2 * m, tn), BF16)],
        compiler_params=_params("parallel", "parallel", "arbitrary"),
        name="dft_mul",
    )(f_re, f_im, u, spec)


def _dft_inv_kernel(gre_ref, gim_ref, y_ref, u_ref, gate_ref, skip_ref, o_ref):
    conv = _dot(gre_ref[...], y_ref[0, 0, 0]) + _dot(gim_ref[...], y_ref[0, 0, 1])
    o_ref[0] = gate_ref[0] * (conv + u_ref[0] * skip_ref[...])


def dft_inv(inv, y, u, u_col, gate, gate_col, skip, tm, tn):
    g_re, g_im = inv
    m = g_re.shape[0]
    b, _, _, _, d = y.shape
    d_tiles = d // tn
    per_half = m // tm
    g_spec = pl.BlockSpec((tm, m), lambda bi, j, i: (i % per_half, 0))
    return pl.pallas_call(
        _dft_inv_kernel,
        grid=(b, d_tiles, 2 * per_half),
        in_specs=[g_spec, g_spec,
                  pl.BlockSpec((1, 1, 2, m, tn), lambda bi, j, i: (bi, i // per_half, 0, 0, j)),
                  pl.BlockSpec((1, tm, tn), lambda bi, j, i: (bi, i, u_col * d_tiles + j)),
                  pl.BlockSpec((1, tm, tn), lambda bi, j, i: (bi, i, gate_col * d_tiles + j)),
                  pl.BlockSpec((1, tn), lambda bi, j, i: (0, j))],
        out_specs=pl.BlockSpec((1, tm, tn), lambda bi, j, i: (bi, i, j)),
        out_shape=jax.ShapeDtypeStruct((b, 2 * m, d), F32),
        compiler_params=_params("parallel", "parallel", "parallel"),
        name="dft_inv",
    )(g_re, g_im, y, u, gate, skip.reshape(1, d))


def _rope_tables(l):
    rows = l // GRID_W
    row = jnp.repeat(jnp.arange(rows), GRID_W).astype(F32)
    col = jnp.tile(jnp.arange(GRID_W), rows).astype(F32)
    half = HEAD_DIM // 2
    inv_freq = ROPE_THETA ** (-jnp.arange(0, half, 2, dtype=F32) / half)
    ang = jnp.concatenate([row[:, None] * inv_freq, col[:, None] * inv_freq], axis=-1)
    cos, sin = jnp.cos(ang), jnp.sin(ang)
    return jnp.concatenate([cos, cos], axis=-1), jnp.concatenate([sin, sin], axis=-1)


def _hyena_tables(l, d):
    t = jnp.linspace(0.0, 1.0, l, dtype=F32)[:, None]
    omega = (2.0 * math.pi / l) * jnp.arange(l, dtype=F32)[:, None]
    bands = jnp.linspace(1e-4, HYENA_BANDS - 1, HYENA_BANDS, dtype=F32)[None, :]
    feats = jnp.concatenate([t, jnp.cos(bands * omega), -jnp.sin(bands * omega)], axis=-1)
    min_decay = math.log(HYENA_TARGET) / HYENA_SLOW_DECAY
    max_decay = math.log(HYENA_TARGET) / HYENA_FAST_DECAY
    deltas = jnp.abs(jnp.linspace(min_decay, max_decay, d, dtype=F32))
    window = jnp.exp(-t * deltas)
    return feats, window


def _to_scan_layout(x):
    return x.reshape(x.shape[0], -1, RWKV_HEAD).transpose(0, 2, 1)


def _from_scan_layout(y):
    return y.transpose(0, 2, 1).reshape(y.shape[0], -1)


def kernel(x, c, mix_w_in, mix_w_out, attn_q_norm, attn_k_norm, rwkv_mu, rwkv_w0, rwkv_w_up, rwkv_a0, rwkv_a_up, rwkv_g_up, rwkv_k_k, rwkv_k_a, rwkv_r_k, rwkv_ln_g, rwkv_ln_b, hy_w_in, hy_conv_w, hy_conv_b, hy_f_w1, hy_f_b1, hy_f_w2, hy_f_b2, hy_f_w3, hy_f_b3, hy_sin_freq, hy_f_out, hy_skip, hy_w_out, ada_w, ada_b, norm_mix, norm_ffn, ffn_w1, ffn_w3, ffn_w2, final_norm):
    b, l, d = x.shape
    depth = ada_w.shape[0]
    tm = min(256, l)
    c_w = RWKV_WIDTH

    mod4 = ada_mod(c, ada_w, ada_b).reshape(depth, b, 1, 6 * d)

    perm = jnp.concatenate([jnp.arange(0, HEAD_DIM, 2), jnp.arange(1, HEAD_DIM, 2)])
    w_in = mix_w_in[0]
    w_q = w_in[:, :ATTN_WIDTH].reshape(d, ATTN_HEADS, HEAD_DIM)[:, :, perm].reshape(d, ATTN_WIDTH)
    w_kv = w_in[:, ATTN_WIDTH:ATTN_WIDTH + 2 * KV_WIDTH]
    w_k = w_kv[:, :KV_WIDTH].reshape(d, ATTN_KV_HEADS, HEAD_DIM)[:, :, perm].reshape(d, KV_WIDTH)
    w_kv = jnp.concatenate([w_k, w_kv[:, KV_WIDTH:]], axis=1)
    w_rw = jnp.pad(w_in[:, ATTN_WIDTH + 2 * KV_WIDTH:], ((0, 0), (0, RWKV_IN_PAD - RWKV_IN)))
    q, kv, rw = proj(x, mod4, 0, norm_mix[0], [w_q.astype(BF16), w_kv.astype(BF16), w_rw.astype(BF16)], tm)

    cos, sin = _rope_tables(l)
    y_attn = attn(q, kv, cos, sin, attn_q_norm[0][perm].reshape(1, HEAD_DIM),
                  attn_k_norm[0][perm].reshape(1, HEAD_DIM), tm)

    seg = jnp.kron(jnp.eye(RWKV_HEADS, dtype=BF16), jnp.ones((RWKV_HEAD, RWKV_HEAD), BF16))
    r, z, v, w, kd, bb, bonus, g = rwkv_prep(
        rw, jnp.pad(rwkv_mu[0], (0, RWKV_IN_PAD - RWKV_IN)).reshape(1, RWKV_IN_PAD),
        rwkv_w0[0].reshape(2, 1, c_w), rwkv_w_up[0], rwkv_a0[0].reshape(2, 1, c_w), rwkv_a_up[0],
        jnp.pad(rwkv_g_up[0], ((0, RWKV_IN_PAD - RWKV_IN), (0, 0))),
        rwkv_k_k[0].reshape(1, c_w), rwkv_k_a[0].reshape(1, c_w), rwkv_r_k[0].reshape(1, c_w), seg, tm)
    y_f, y_b = rwkv_scan([_to_scan_layout(a) for a in (r, w, kd, v, z, bb)], min(SCAN_BLOCK, l))
    x = mix_out([y_attn, (_from_scan_layout(y_f), 0), (_from_scan_layout(y_b), 1), bonus, g],
                [rwkv_ln_g[0].reshape(1, c_w), rwkv_ln_b[0].reshape(1, c_w), seg],
                x, mod4, 0, mix_w_out[0].astype(BF16), tm, True)
    x = ffn(x, mod4, 0, norm_ffn[0], ffn_w1[0].astype(BF16), ffn_w3[0].astype(BF16), ffn_w2[0].astype(BF16),
            final_norm, tm, False)

    (p3,) = proj(x, mod4, 1, norm_mix[1], [hy_w_in[0].astype(BF16)], tm)
    tn = min(512, d)
    p3 = short_conv(p3, hy_conv_w[0], hy_conv_b[0], min(256, d))
    feats, window = _hyena_tables(l, d)
    k_pad = 128
    feats = jnp.pad(feats, ((0, 0), (0, k_pad - feats.shape[1])))
    f_w1 = jnp.pad(hy_f_w1[0], ((0, k_pad - hy_f_w1.shape[1]), (0, 0)))
    taps = hy_filter(feats, f_w1, hy_f_b1[0], hy_f_w2[0], hy_f_b2[0], hy_f_w3[0], hy_f_b3[0],
                     hy_sin_freq[0], hy_f_out[0], window, tn)
    fwd, inv = _dft_matrices(l // 2)
    ft = min(FREQ_TILE, l // 2)
    spec = dft_spec(fwd, taps, d, ft, tn)
    y1 = dft_mul(fwd, p3, 0, spec, 0, d, ft, tn)
    z1 = dft_inv(inv, y1, p3, 0, p3, 1, hy_skip[0, 0], ft, tn)
    y2 = dft_mul(fwd, z1, 0, spec, 1, d, ft, tn)
    z2 = dft_inv(inv, y2, z1, 0, p3, 2, hy_skip[0, 1], ft, tn)
    x = mix_out([z2], [], x, mod4, 1, hy_w_out[0].astype(BF16), tm, False)
    x = ffn(x, mod4, 1, norm_ffn[1], ffn_w1[1].astype(BF16), ffn_w3[1].astype(BF16), ffn_w2[1].astype(BF16),
            final_norm, tm, True)
    return x
```

```python
import functools
import math

import jax
import jax.numpy as jnp
from jax import lax
from jax.experimental import pallas as pl
from jax.experimental.pallas import tpu as pltpu

F32 = jnp.float32
BF16 = jnp.bfloat16
HIGHEST = lax.Precision.HIGHEST

GRID_W = 64
HEAD_DIM = 64
ATTN_HEADS = 8
ATTN_KV_HEADS = 2
ATTN_WIDTH = ATTN_HEADS * HEAD_DIM
KV_WIDTH = ATTN_KV_HEADS * HEAD_DIM
ROPE_THETA = 10000.0
RWKV_HEADS = 8
RWKV_HEAD = 64
RWKV_WIDTH = RWKV_HEADS * RWKV_HEAD
DECAY_LORA = 64
AAA_LORA = 64
GATE_LORA = 160
RWKV_IN = 3 * RWKV_WIDTH + 2 * DECAY_LORA + 2 * AAA_LORA + GATE_LORA
RWKV_IN_PAD = 2048
GN_EPS = 64e-5
HYENA_BANDS = 16
HYENA_TARGET = 1e-2
HYENA_FAST_DECAY = 0.3
HYENA_SLOW_DECAY = 1.5
NORM_EPS = 1e-6

V7X_VMEM_BYTES = 64 * 1024 * 1024
VMEM_LIMIT = V7X_VMEM_BYTES - 8 * 1024 * 1024
FREQ_TILE = 512
SCAN_BLOCK = 32
FFN_ROWS = 512
SUBLANES = 8


def _params(*sem):
    return pltpu.CompilerParams(dimension_semantics=sem, vmem_limit_bytes=VMEM_LIMIT)


def _const_spec(shape):
    zeros = (0,) * len(shape)
    return pl.BlockSpec(shape, lambda *_: zeros, pipeline_mode=pl.Buffered(1))


def _dot(a, b):
    return jnp.dot(a, b, preferred_element_type=F32)


def _split(x):
    hi = x.astype(BF16)
    return hi, (x - hi.astype(F32)).astype(BF16)


def _dot_split(a, b):
    ah, al = _split(a)
    bh, bl = _split(b)
    return _dot(ah, bh) + (_dot(ah, bl) + _dot(al, bh))


def _seg_sum(a, seg):
    ah, al = _split(a)
    return _dot(ah, seg) + _dot(al, seg)


def _dot_hi(a, b):
    return jnp.dot(a, b, precision=HIGHEST, preferred_element_type=F32)


def _rms_mod(x, gain, scale, shift):
    ms = jnp.mean(x * x, axis=-1, keepdims=True)
    return (x * lax.rsqrt(ms + NORM_EPS) * gain) * (1.0 + scale) + shift


def _ada_kernel(c_ref, w_ref, b_ref, o_ref):
    c = c_ref[...]
    cond = c * jax.nn.sigmoid(c)
    o_ref[...] = _dot_hi(cond, w_ref[...]) + b_ref[...]


def ada_mod(c, ada_w, ada_b):
    depth, d, n = ada_w.shape
    b = c.shape[0]
    tn = 1536
    return pl.pallas_call(
        _ada_kernel,
        grid=(depth, n // tn),
        in_specs=[pl.BlockSpec((b, d), lambda l, j: (0, 0)),
                  pl.BlockSpec((None, d, tn), lambda l, j: (l, 0, j)),
                  pl.BlockSpec((None, 1, tn), lambda l, j: (l, 0, j))],
        out_specs=pl.BlockSpec((None, b, tn), lambda l, j: (l, 0, j)),
        out_shape=jax.ShapeDtypeStruct((depth, b, n), F32),
        compiler_params=_params("parallel", "parallel"),
        name="ada_mod",
    )(c, ada_w, ada_b.reshape(depth, 1, n))


def _mod_spec(d, layer, chunk):
    return pl.BlockSpec((None, None, 1, d), lambda b, *_: (layer, b, 0, chunk))


def _proj_kernel(n_out, x_ref, sh_ref, sc_ref, g_ref, *refs):
    h = _rms_mod(x_ref[0], g_ref[...], sc_ref[...], sh_ref[...]).astype(BF16)
    for w_ref, o_ref in zip(refs[:n_out], refs[n_out:]):
        o_ref[0] = _dot(h, w_ref[...])


def proj(x, mod4, layer, gain, weights, tm):
    b, l, d = x.shape
    n_out = len(weights)
    return pl.pallas_call(
        functools.partial(_proj_kernel, n_out),
        grid=(b, l // tm),
        in_specs=[pl.BlockSpec((1, tm, d), lambda bi, i: (bi, i, 0)),
                  _mod_spec(d, layer, 0), _mod_spec(d, layer, 1),
                  _const_spec((1, d))] + [_const_spec(w.shape) for w in weights],
        out_specs=[pl.BlockSpec((1, tm, w.shape[1]), lambda bi, i: (bi, i, 0)) for w in weights],
        out_shape=[jax.ShapeDtypeStruct((b, l, w.shape[1]), F32) for w in weights],
        compiler_params=_params("parallel", "parallel"),
        name="proj",
    )(x, mod4, mod4, gain.reshape(1, d), *weights)


def _headnorm_rope(x, gain, cos, sin):
    ms = jnp.mean(x * x, axis=-1, keepdims=True)
    y = x * lax.rsqrt(ms + NORM_EPS) * gain
    half = HEAD_DIM // 2
    rot = jnp.concatenate([-y[:, half:], y[:, :half]], axis=-1)
    return y * cos + rot * sin


def _attn_kernel(q_ref, kv_ref, cq_ref, sq_ref, ck_ref, sk_ref, qn_ref, kn_ref, o_ref, kt_s, v_s):
    @pl.when(pl.program_id(1) == 0)
    def _():
        kv = kv_ref[0]
        ones_col = (lax.broadcasted_iota(jnp.int32, (kv.shape[0], HEAD_DIM), 1) == 0).astype(F32)
        for hk in range(ATTN_KV_HEADS):
            k = kv[:, HEAD_DIM * hk:HEAD_DIM * (hk + 1)]
            kt_s[hk] = _headnorm_rope(k, kn_ref[...], ck_ref[...], sk_ref[...]).T.astype(BF16)
            v = kv[:, KV_WIDTH + HEAD_DIM * hk:KV_WIDTH + HEAD_DIM * (hk + 1)]
            v_s[hk] = jnp.concatenate([v, ones_col], axis=-1).astype(BF16)

    q = q_ref[0]
    group = ATTN_HEADS // ATTN_KV_HEADS
    outs = []
    for h in range(ATTN_HEADS):
        qh = _headnorm_rope(q[:, HEAD_DIM * h:HEAD_DIM * (h + 1)], qn_ref[...], cq_ref[...], sq_ref[...])
        qh = (qh * (HEAD_DIM ** -0.5 * math.log2(math.e))).astype(BF16)
        s = _dot(qh, kt_s[h // group])
        p = jnp.exp2(s - jnp.max(s, axis=-1, keepdims=True)).astype(BF16)
        o = _dot(p, v_s[h // group])
        outs.append(o[:, :HEAD_DIM] / o[:, HEAD_DIM:HEAD_DIM + 1])
    o_ref[0] = jnp.concatenate(outs, axis=-1)


def attn(q, kv, cos, sin, qn, kn, tq):
    b, l, _ = q.shape
    return pl.pallas_call(
        _attn_kernel,
        grid=(b, l // tq),
        in_specs=[pl.BlockSpec((1, tq, ATTN_WIDTH), lambda bi, i: (bi, i, 0)),
                  pl.BlockSpec((1, l, 2 * KV_WIDTH), lambda bi, i: (bi, 0, 0)),
                  pl.BlockSpec((tq, HEAD_DIM), lambda bi, i: (i, 0)),
                  pl.BlockSpec((tq, HEAD_DIM), lambda bi, i: (i, 0)),
                  _const_spec((l, HEAD_DIM)), _const_spec((l, HEAD_DIM)),
                  _const_spec((1, HEAD_DIM)), _const_spec((1, HEAD_DIM))],
        out_specs=pl.BlockSpec((1, tq, ATTN_WIDTH), lambda bi, i: (bi, i, 0)),
        out_shape=jax.ShapeDtypeStruct((b, l, ATTN_WIDTH), F32),
        scratch_shapes=[pltpu.VMEM((ATTN_KV_HEADS, HEAD_DIM, l), BF16),
                        pltpu.VMEM((ATTN_KV_HEADS, l, 2 * HEAD_DIM), BF16)],
        compiler_params=_params("parallel", "arbitrary"),
        name="attn",
    )(q, kv, cos, sin, cos, sin, qn, kn)


def _halo_specs(tt, w, l):
    nb = l // SUBLANES
    per = tt // SUBLANES
    main = pl.BlockSpec((1, tt, w), lambda bi, i: (bi, i, 0))
    prev = pl.BlockSpec((1, SUBLANES, w), lambda bi, i: (bi, jnp.maximum(i * per - 1, 0), 0))
    nxt = pl.BlockSpec((1, SUBLANES, w), lambda bi, i: (bi, jnp.minimum((i + 1) * per, nb - 1), 0))
    return [main, prev, nxt]


def _neighbours(cur, prow, nrow):
    tt = cur.shape[0]
    row = lax.broadcasted_iota(jnp.int32, cur.shape, 0)
    prev = jnp.where(row == 0, prow, pltpu.roll(cur, 1, 0))
    nxt = jnp.where(row == tt - 1, nrow, pltpu.roll(cur, tt - 1, 0))
    return prev, nxt


def _halo_rows(xp_ref, xn_ref, axis):
    i = pl.program_id(axis)
    last = pl.num_programs(axis) - 1
    prow = jnp.where(i > 0, xp_ref[0, SUBLANES - 1:SUBLANES, :], 0.0)
    nrow = jnp.where(i < last, xn_ref[0, 0:1, :], 0.0)
    return prow, nrow


def _rwkv_prep_kernel(x_ref, xp_ref, xn_ref, mu_ref, w0_ref, wup_ref, a0_ref, aup_ref, gup_ref,
                      kk_ref, ka_ref, rk_ref, seg_ref,
                      r_o, z_o, v_o, w_o, kd_o, b_o, bonus_o, g_o):
    cur = x_ref[0]
    prev, nxt = _neighbours(cur, *_halo_rows(xp_ref, xn_ref, 1))
    ps = cur + (0.5 * (prev + nxt) - cur) * mu_ref[...]
    c = RWKV_WIDTH
    r = ps[:, 0:c]
    k = ps[:, c:2 * c]
    v = ps[:, 2 * c:3 * c]
    o_a = 3 * c + 2 * DECAY_LORA
    o_g = o_a + 2 * AAA_LORA
    kk = k * kk_ref[...]
    kk = kk * lax.rsqrt(jnp.maximum(_seg_sum(kk * kk, seg_ref[...]), 1e-24))
    for d in range(2):
        cols = slice(c * d, c * (d + 1))
        r_o[:, cols] = r
        v_o[:, cols] = v
        z_o[:, cols] = -kk
        w_lo = ps[:, 3 * c + DECAY_LORA * d:3 * c + DECAY_LORA * (d + 1)]
        a_lo = ps[:, o_a + AAA_LORA * d:o_a + AAA_LORA * (d + 1)]
        decay = jnp.exp(-math.exp(-0.5) * jax.nn.sigmoid(w0_ref[d] + _dot_split(jnp.tanh(w_lo), wup_ref[d])))
        a = jax.nn.sigmoid(a0_ref[d] + _dot_split(a_lo, aup_ref[d]))
        w_o[:, cols] = decay
        kd_o[:, cols] = k * (1.0 + (a - 1.0) * ka_ref[...])
        b_o[:, cols] = kk * a
    bonus_o[0] = _seg_sum(r * k * rk_ref[...], seg_ref[...]) * v
    g_o[0] = _dot_split(jax.nn.sigmoid(ps[:, o_g:RWKV_IN_PAD]), gup_ref[...])


def rwkv_prep(rw, mu, w0, w_up, a0, a_up, g_up, k_k, k_a, r_k, seg, tt):
    b, l, w = rw.shape
    c = RWKV_WIDTH
    natural = jax.ShapeDtypeStruct((b, l, c), F32)
    time_major = jax.ShapeDtypeStruct((l, b * 2 * c), F32)
    spec_n = pl.BlockSpec((1, tt, c), lambda bi, i: (bi, i, 0))
    spec_t = pl.BlockSpec((tt, 2 * c), lambda bi, i: (i, bi))
    consts = [mu, w0, w_up, a0, a_up, g_up, k_k, k_a, r_k, seg]
    return pl.pallas_call(
        _rwkv_prep_kernel,
        grid=(b, l // tt),
        in_specs=_halo_specs(tt, w, l) + [_const_spec(a.shape) for a in consts],
        out_specs=[spec_t] * 6 + [spec_n] * 2,
        out_shape=[time_major] * 6 + [natural] * 2,
        compiler_params=_params("parallel", "parallel"),
        name="rwkv_prep",
    )(rw, rw, rw, *consts)


SCAN_ROWS = 32
_R, _W, _KD, _V, _Z, _B = range(6)


def _rwkv_scan_kernel(*refs):
    ins, (yf_ref, yb_ref, s_ref, m_ref, sz_ref) = refs[:12], refs[12:]
    tb, n, lanes = ins[0].shape
    fwd_lane = (lax.broadcasted_iota(jnp.int32, (n, lanes), 1) // RWKV_HEADS) % 2 == 0

    def stage(which, t):
        m_ref[which] = jnp.where(fwd_lane, ins[2 * which][t], ins[2 * which + 1][tb - 1 - t])

    @pl.when(pl.program_id(0) == 0)
    def _():
        s_ref[...] = jnp.zeros_like(s_ref)

    halves = [slice(h * SCAN_ROWS, (h + 1) * SCAN_ROWS) for h in range(n // SCAN_ROWS)]

    stage(_Z, 0)
    for rows in halves:
        acc = s_ref[0, rows, :] * m_ref[_Z, 0:1, :]
        for k in range(1, n):
            acc = acc + s_ref[k, rows, :] * m_ref[_Z, k:k + 1, :]
        sz_ref[rows, :] = acc

    def step(t, carry):
        for which in (_R, _W, _KD, _V, _B):
            stage(which, t)
        stage(_Z, jnp.minimum(t + 1, tb - 1))
        for rows in halves:
            sz = sz_ref[rows, :]
            vt = m_ref[_V, rows, :]
            y = None
            sz_next = None
            for k in range(n):
                sk = (s_ref[k, rows, :] * m_ref[_W, k:k + 1, :] + sz * m_ref[_B, k:k + 1, :]
                      + vt * m_ref[_KD, k:k + 1, :])
                s_ref[k, rows, :] = sk
                yk = sk * m_ref[_R, k:k + 1, :]
                zk = sk * m_ref[_Z, k:k + 1, :]
                y = yk if y is None else y + yk
                sz_next = zk if sz_next is None else sz_next + zk
            sz_ref[rows, :] = sz_next
            yf_ref[t, rows, :] = y
            yb_ref[tb - 1 - t, rows, :] = y
        return carry

    lax.fori_loop(0, tb, step, 0)


def rwkv_scan(arrays, tb):
    l, n, lanes = arrays[0].shape
    nt = l // tb
    fwd = pl.BlockSpec((tb, n, lanes), lambda i: (i, 0, 0))
    rev = pl.BlockSpec((tb, n, lanes), lambda i: (nt - 1 - i, 0, 0))
    out = jax.ShapeDtypeStruct((l, n, lanes), F32)
    return pl.pallas_call(
        _rwkv_scan_kernel,
        grid=(nt,),
        in_specs=[fwd, rev] * 6,
        out_specs=[fwd, rev],
        out_shape=[out, out],
        scratch_shapes=[pltpu.VMEM((n, n, lanes), F32), pltpu.VMEM((6, n, lanes), F32),
                        pltpu.VMEM((n, lanes), F32)],
        compiler_params=_params("arbitrary"),
        name="rwkv_scan",
    )(*[a for a in arrays for _ in range(2)])


def _mix_out_kernel(rwkv, *refs):
    if rwkv:
        (ya_ref, yf_ref, yb_ref, bonus_ref, g_ref, lng_ref, lnb_ref, seg_ref,
         x_ref, gate_ref, w_ref, o_ref) = refs
        y = yf_ref[...] + yb_ref[...]
        inv_n = 1.0 / RWKV_HEAD
        mean = _seg_sum(y, seg_ref[...]) * inv_n
        cen = y - mean
        var = _seg_sum(cen * cen, seg_ref[...]) * inv_n
        yn = cen * lax.rsqrt(var + GN_EPS) * lng_ref[...] + lnb_ref[...]
        yr = (yn + bonus_ref[0]) * g_ref[0]
        a = jnp.concatenate([ya_ref[0], yr], axis=-1)
    else:
        a_ref, x_ref, gate_ref, w_ref, o_ref = refs
        a = a_ref[0]
    o_ref[0] = x_ref[0] + gate_ref[...] * _dot(a.astype(BF16), w_ref[...])


def mix_out(acts, consts, x, mod4, layer, w, tm, rwkv):
    b, l, d = x.shape
    in_specs = []
    for i_act, a in enumerate(acts):
        if isinstance(a, tuple):
            arr, direction = a
            width = arr.shape[1] // (2 * b)
            in_specs.append(pl.BlockSpec((tm, width), lambda bi, i, dr=direction: (i, 2 * bi + dr)))
            acts[i_act] = arr
        else:
            in_specs.append(pl.BlockSpec((1, tm, a.shape[2]), lambda bi, i: (bi, i, 0)))
    in_specs += [_const_spec(a.shape) for a in consts]
    in_specs += [pl.BlockSpec((1, tm, d), lambda bi, i: (bi, i, 0)), _mod_spec(d, layer, 2), _const_spec(w.shape)]
    return pl.pallas_call(
        functools.partial(_mix_out_kernel, rwkv),
        grid=(b, l // tm),
        in_specs=in_specs,
        out_specs=pl.BlockSpec((1, tm, d), lambda bi, i: (bi, i, 0)),
        out_shape=jax.ShapeDtypeStruct((b, l, d), F32),
        compiler_params=_params("parallel", "parallel"),
        name="mix_out",
    )(*acts, *consts, x, mod4, w)


def _ffn_kernel(final, x_ref, sh_ref, sc_ref, gate_ref, g_ref, w1_ref, w3_ref, w2_ref, fg_ref, o_ref):
    x = x_ref[0]
    h = _rms_mod(x, g_ref[...], sc_ref[...], sh_ref[...]).astype(BF16)
    a = _dot(h, w1_ref[...])
    u = (a * jax.nn.sigmoid(a)) * _dot(h, w3_ref[...])
    out = x + gate_ref[...] * _dot(u.astype(BF16), w2_ref[...])
    if final:
        ms = jnp.mean(out * out, axis=-1, keepdims=True)
        out = out * lax.rsqrt(ms + NORM_EPS) * fg_ref[...]
    o_ref[0] = out


def ffn(x, mod4, layer, gain, w1, w3, w2, final_gain, tm, final):
    b, l, d = x.shape
    return pl.pallas_call(
        functools.partial(_ffn_kernel, final),
        grid=(b, l // tm),
        in_specs=[pl.BlockSpec((1, tm, d), lambda bi, i: (bi, i, 0)),
                  _mod_spec(d, layer, 3), _mod_spec(d, layer, 4), _mod_spec(d, layer, 5),
                  _const_spec((1, d)), _const_spec(w1.shape), _const_spec(w3.shape), _const_spec(w2.shape),
                  _const_spec((1, d))],
        out_specs=pl.BlockSpec((1, tm, d), lambda bi, i: (bi, i, 0)),
        out_shape=jax.ShapeDtypeStruct((b, l, d), F32),
        compiler_params=_params("parallel", "parallel"),
        name="ffn",
    )(x, mod4, mod4, mod4, gain.reshape(1, d), w1, w3, w2, final_gain.reshape(1, d))


def _conv3(cur, prow, nrow, w_ref, b_ref):
    prev, nxt = _neighbours(cur, prow, nrow)
    return prev * w_ref[0:1, :] + cur * w_ref[1:2, :] + nxt * w_ref[2:3, :] + b_ref[...]


def _hy_filter_kernel(d_tiles, feats_ref, w1_ref, b1_ref, w2_ref, b2_ref, w3_ref, b3_ref, fr_ref,
                      wout_ref, win_ref, o_ref, hid_s):
    j = pl.program_id(0)

    @pl.when(j == 0)
    def _():
        fr = fr_ref[...]
        hid = jnp.sin(fr * (_dot_hi(feats_ref[...], w1_ref[...]) + b1_ref[...]))
        hid = jnp.sin(fr * (_dot_hi(hid, w2_ref[...]) + b2_ref[...]))
        hid_s[...] = jnp.sin(fr * (_dot_hi(hid, w3_ref[...]) + b3_ref[...]))

    filt = _dot_hi(hid_s[...], wout_ref[...]) * win_ref[...]
    backward = (j // d_tiles) % 2 == 1
    row = lax.broadcasted_iota(jnp.int32, filt.shape, 0)
    o_ref[...] = jnp.where(jnp.logical_and(backward, row == 0), 0.0, filt)


def hy_filter(feats, w1, b1, w2, b2, w3, b3, fr, w_out, window, tn):
    l = feats.shape[0]
    hf = w2.shape[0]
    n = w_out.shape[1]
    d = window.shape[1]
    d_tiles = d // tn
    consts = [feats, w1, b1.reshape(1, hf), w2, b2.reshape(1, hf), w3, b3.reshape(1, hf), fr.reshape(1, hf)]
    return pl.pallas_call(
        functools.partial(_hy_filter_kernel, d_tiles),
        grid=(n // tn,),
        in_specs=[_const_spec(a.shape) for a in consts]
        + [pl.BlockSpec((hf, tn), lambda j: (0, j)), pl.BlockSpec((l, tn), lambda j: (0, j % d_tiles))],
        out_specs=pl.BlockSpec((l, tn), lambda j: (0, j)),
        out_shape=jax.ShapeDtypeStruct((l, n), F32),
        scratch_shapes=[pltpu.VMEM((l, hf), F32)],
        compiler_params=_params("arbitrary"),
        name="hy_filter",
    )(*consts, w_out, window)


def _dft_matrices(l):
    n = 2 * l
    lane = min(128, l)
    k = jnp.arange(l, dtype=jnp.int32)[:, None]
    t = jnp.arange(l, dtype=jnp.int32)
    a = ((k * jnp.arange(lane, dtype=jnp.int32)[None, :]) % n).astype(F32) * (2.0 * math.pi / n)
    c = ((k * (lane * jnp.arange(l // lane, dtype=jnp.int32))[None, :]) % n).astype(F32) * (2.0 * math.pi / n)
    ca, sa, cc, sc = jnp.cos(a)[:, None, :], jnp.sin(a)[:, None, :], jnp.cos(c)[:, :, None], jnp.sin(c)[:, :, None]
    cos = (ca * cc - sa * sc).reshape(l, l)
    sin = (sa * cc + ca * sc).reshape(l, l)
    alt = jnp.where(t % 2 == 0, 1.0, -1.0).astype(F32)
    f_re = cos.astype(BF16)
    f_im = jnp.where(k == 0, alt[None, :], -sin).astype(BF16)
    col0 = (t == 0)[None, :]
    g_re = jnp.where(col0, 1.0 / n, (2.0 / n) * cos).astype(BF16)
    g_im = jnp.where(col0, alt[:, None] / n, (-2.0 / n) * sin).astype(BF16)
    return (f_re, f_im), (g_re, g_im)


def _nyquist_slot(i, shape):
    row = lax.broadcasted_iota(jnp.int32, shape, 0)
    return jnp.logical_and(i == 0, row == 0)


def _dft_spec_kernel(fre_ref, fim_ref, ft_ref, fb_ref, bt_ref, bb_ref, o_ref):
    fre, fim = fre_ref[...], fim_ref[...]
    ft, fb, bt, bb = (ref[...].astype(BF16) for ref in (ft_ref, fb_ref, bt_ref, bb_ref))
    af_r, af_i = _dot(fre, ft), _dot(fim, ft)
    bf_r, bf_i = _dot(fre, fb), _dot(fim, fb)
    ab_r, ab_i = _dot(fre, bt), _dot(fim, bt)
    bb_r, bb_i = _dot(fre, bb), _dot(fim, bb)
    nyq = _nyquist_slot(pl.program_id(1), af_r.shape)
    row = lax.broadcasted_iota(jnp.int32, af_r.shape, 0)
    sgn = jnp.where(row % 2 == 0, 1.0, -1.0)
    lag0 = ft_ref[0:1, :]
    o_ref[0, 0] = af_r + ab_r
    o_ref[0, 1] = jnp.where(nyq, af_i + ab_i, af_i - ab_i)
    o_ref[1, 0] = bf_r + sgn * (af_r - lag0)
    o_ref[1, 1] = bf_i + jnp.where(nyq, af_i - lag0, sgn * af_i)
    o_ref[2, 0] = sgn * ab_r + bb_r
    o_ref[2, 1] = jnp.where(nyq, ab_i + bb_i, -(sgn * ab_i + bb_i))


def dft_spec(fwd, taps, d, ft, tn):
    f_re, f_im = fwd
    m = f_re.shape[0]
    d_tiles = d // tn
    orders = taps.shape[1] // (2 * d)
    f_spec = pl.BlockSpec((ft, m), lambda j, i: (i, 0))

    def tap_spec(half, direction):
        return pl.BlockSpec((m, tn), lambda j, i: (half, (j // d_tiles) * 2 * d_tiles + direction * d_tiles
                                                    + j % d_tiles))

    return pl.pallas_call(
        _dft_spec_kernel,
        grid=(orders * d_tiles, m // ft),
        in_specs=[f_spec, f_spec, tap_spec(0, 0), tap_spec(1, 0), tap_spec(0, 1), tap_spec(1, 1)],
        out_specs=pl.BlockSpec((3, 2, ft, tn), lambda j, i: (0, 0, i, j)),
        out_shape=jax.ShapeDtypeStruct((3, 2, m, orders * d), F32),
        compiler_params=_params("parallel", "parallel"),
        name="dft_spec",
    )(f_re, f_im, taps, taps, taps, taps)


def _dft_mul_kernel(has_conv, fre_ref, fim_ref, u_ref, h_ref, *refs):
    o_ref, u_s = refs[-2:]
    i = pl.program_id(2)

    @pl.when(i == 0)
    def _():
        u = u_ref[0]
        if has_conv:
            u = _conv3(u, 0.0, 0.0, *refs[:2])
        u_s[...] = u.astype(BF16)

    m = u_s.shape[0] // 2
    fre, fim = fre_ref[...], fim_ref[...]
    tr, ti = _dot(fre, u_s[0:m]), _dot(fim, u_s[0:m])
    br, bi = _dot(fre, u_s[m:]), _dot(fim, u_s[m:])
    h0r, h0i, hpr, hpi, hmr, hmi = (h_ref[a, p] for a in range(3) for p in range(2))
    head = 2 * SUBLANES
    nyq = _nyquist_slot(i, (head, tr.shape[1]))

    def emit(half, plane, general, packed):
        o_ref[0, half, plane, 0:head] = jnp.where(nyq, packed, general[0:head]).astype(BF16)
        o_ref[0, half, plane, head:] = general[head:].astype(BF16)

    def top(x):
        return x[0:head]

    emit(0, 0, h0r * tr - h0i * ti + hmr * br - hmi * bi, top(h0r) * top(tr) + top(hmr) * top(br))
    emit(0, 1, h0r * ti + h0i * tr + hmr * bi + hmi * br, top(h0i) * top(ti) + top(hmi) * top(bi))
    emit(1, 0, hpr * tr - hpi * ti + h0r * br - h0i * bi, top(hpr) * top(tr) + top(h0r) * top(br))
    emit(1, 1, hpr * ti + hpi * tr + h0r * bi + h0i * br, top(hpi) * top(ti) + top(h0i) * top(bi))


def dft_mul(fwd, u, u_col, spec, spec_col, d, ft, tn, conv=None):
    f_re, f_im = fwd
    m = f_re.shape[0]
    b = u.shape[0]
    d_tiles = d // tn
    f_spec = pl.BlockSpec((ft, m), lambda bi, j, i: (i, 0))
    conv_specs = [] if conv is None else [pl.BlockSpec((3, tn), lambda bi, j, i: (0, u_col * d_tiles + j)),
                                          pl.BlockSpec((1, tn), lambda bi, j, i: (0, u_col * d_tiles + j))]
    return pl.pallas_call(
        functools.partial(_dft_mul_kernel, conv is not None),
        grid=(b, d_tiles, m // ft),
        in_specs=[f_spec, f_spec,
                  pl.BlockSpec((1, 2 * m, tn), lambda bi, j, i: (bi, 0, u_col * d_tiles + j)),
                  pl.BlockSpec((3, 2, ft, tn), lambda bi, j, i: (0, 0, i, spec_col * d_tiles + j))] + conv_specs,
        out_specs=pl.BlockSpec((1, 2, 2, ft, tn), lambda bi, j, i: (bi, 0, 0, i, j)),
        out_shape=jax.ShapeDtypeStruct((b, 2, 2, m, d), BF16),
        scratch_shapes=[pltpu.VMEM((2 * m, tn), BF16)],
        compiler_params=_params("parallel", "parallel", "arbitrary"),
        name="dft_mul",
    )(f_re, f_im, u, spec, *(conv or ()))


def _dft_inv_kernel(conv_u, gre_ref, gim_ref, y_ref, *refs):
    u_refs, (g_ref, gp_ref, gn_ref, gw_ref, gb_ref, skip_ref, o_ref) = refs[:-7], refs[-7:]
    u = u_refs[0][0]
    if conv_u:
        u = _conv3(u, *_halo_rows(u_refs[1], u_refs[2], 2), u_refs[3], u_refs[4])
    gate = _conv3(g_ref[0], *_halo_rows(gp_ref, gn_ref, 2), gw_ref, gb_ref)
    conv = _dot(gre_ref[...], y_ref[0, 0, 0]) + _dot(gim_ref[...], y_ref[0, 0, 1])
    o_ref[0] = gate * (conv + u * skip_ref[...])


def dft_inv(inv, y, u, u_col, gate, gate_col, skip, tm, tn, conv_w, conv_b, conv_u):
    g_re, g_im = inv
    m = g_re.shape[0]
    b, _, _, _, d = y.shape
    d_tiles = d // tn
    per_half = m // tm
    rows8 = tm // SUBLANES
    last8 = 2 * m // SUBLANES - 1
    g_spec = pl.BlockSpec((tm, m), lambda bi, j, i: (i % per_half, 0))

    def tile_specs(col, halo):
        specs = [pl.BlockSpec((1, tm, tn), lambda bi, j, i: (bi, i, col * d_tiles + j))]
        if halo:
            specs += [pl.BlockSpec((1, SUBLANES, tn),
                                   lambda bi, j, i: (bi, jnp.maximum(i * rows8 - 1, 0), col * d_tiles + j)),
                      pl.BlockSpec((1, SUBLANES, tn),
                                   lambda bi, j, i: (bi, jnp.minimum((i + 1) * rows8, last8), col * d_tiles + j)),
                      pl.BlockSpec((3, tn), lambda bi, j, i: (0, col * d_tiles + j)),
                      pl.BlockSpec((1, tn), lambda bi, j, i: (0, col * d_tiles + j))]
        return specs

    u_ops = [u, u, u, conv_w, conv_b] if conv_u else [u]
    return pl.pallas_call(
        functools.partial(_dft_inv_kernel, conv_u),
        grid=(b, d_tiles, 2 * per_half),
        in_specs=[g_spec, g_spec,
                  pl.BlockSpec((1, 1, 2, m, tn), lambda bi, j, i: (bi, i // per_half, 0, 0, j))]
        + tile_specs(u_col, conv_u) + tile_specs(gate_col, True)
        + [pl.BlockSpec((1, tn), lambda bi, j, i: (0, j))],
        out_specs=pl.BlockSpec((1, tm, tn), lambda bi, j, i: (bi, i, j)),
        out_shape=jax.ShapeDtypeStruct((b, 2 * m, d), F32),
        compiler_params=_params("parallel", "parallel", "parallel"),
        name="dft_inv",
    )(g_re, g_im, y, *u_ops, gate, gate, gate, conv_w, conv_b, skip.reshape(1, d))


def _rope_tables(l):
    rows = l // GRID_W
    row = jnp.repeat(jnp.arange(rows), GRID_W).astype(F32)
    col = jnp.tile(jnp.arange(GRID_W), rows).astype(F32)
    half = HEAD_DIM // 2
    inv_freq = ROPE_THETA ** (-jnp.arange(0, half, 2, dtype=F32) / half)
    ang = jnp.concatenate([row[:, None] * inv_freq, col[:, None] * inv_freq], axis=-1)
    cos, sin = jnp.cos(ang), jnp.sin(ang)
    return jnp.concatenate([cos, cos], axis=-1), jnp.concatenate([sin, sin], axis=-1)


def _hyena_tables(l, d):
    t = jnp.linspace(0.0, 1.0, l, dtype=F32)[:, None]
    omega = (2.0 * math.pi / l) * jnp.arange(l, dtype=F32)[:, None]
    bands = jnp.linspace(1e-4, HYENA_BANDS - 1, HYENA_BANDS, dtype=F32)[None, :]
    feats = jnp.concatenate([t, jnp.cos(bands * omega), -jnp.sin(bands * omega)], axis=-1)
    min_decay = math.log(HYENA_TARGET) / HYENA_SLOW_DECAY
    max_decay = math.log(HYENA_TARGET) / HYENA_FAST_DECAY
    deltas = jnp.abs(jnp.linspace(min_decay, max_decay, d, dtype=F32))
    window = jnp.exp(-t * deltas)
    return feats, window


def _to_scan_layout(x):
    return x.reshape(x.shape[0], -1, RWKV_HEAD).transpose(0, 2, 1)


def _from_scan_layout(y):
    return y.transpose(0, 2, 1).reshape(y.shape[0], -1)


def kernel(x, c, mix_w_in, mix_w_out, attn_q_norm, attn_k_norm, rwkv_mu, rwkv_w0, rwkv_w_up, rwkv_a0, rwkv_a_up, rwkv_g_up, rwkv_k_k, rwkv_k_a, rwkv_r_k, rwkv_ln_g, rwkv_ln_b, hy_w_in, hy_conv_w, hy_conv_b, hy_f_w1, hy_f_b1, hy_f_w2, hy_f_b2, hy_f_w3, hy_f_b3, hy_sin_freq, hy_f_out, hy_skip, hy_w_out, ada_w, ada_b, norm_mix, norm_ffn, ffn_w1, ffn_w3, ffn_w2, final_norm):
    b, l, d = x.shape
    depth = ada_w.shape[0]
    tm = min(256, l)
    c_w = RWKV_WIDTH

    mod4 = ada_mod(c, ada_w, ada_b).reshape(depth, b, 1, 6 * d)

    perm = jnp.concatenate([jnp.arange(0, HEAD_DIM, 2), jnp.arange(1, HEAD_DIM, 2)])
    w_in = mix_w_in[0]
    w_q = w_in[:, :ATTN_WIDTH].reshape(d, ATTN_HEADS, HEAD_DIM)[:, :, perm].reshape(d, ATTN_WIDTH)
    w_kv = w_in[:, ATTN_WIDTH:ATTN_WIDTH + 2 * KV_WIDTH]
    w_k = w_kv[:, :KV_WIDTH].reshape(d, ATTN_KV_HEADS, HEAD_DIM)[:, :, perm].reshape(d, KV_WIDTH)
    w_kv = jnp.concatenate([w_k, w_kv[:, KV_WIDTH:]], axis=1)
    w_rw = jnp.pad(w_in[:, ATTN_WIDTH + 2 * KV_WIDTH:], ((0, 0), (0, RWKV_IN_PAD - RWKV_IN)))
    q, kv, rw = proj(x, mod4, 0, norm_mix[0], [w_q.astype(BF16), w_kv.astype(BF16), w_rw.astype(BF16)], tm)

    seg = jnp.kron(jnp.eye(RWKV_HEADS, dtype=BF16), jnp.ones((RWKV_HEAD, RWKV_HEAD), BF16))
    r, z, v, w, kd, bb, bonus, g = rwkv_prep(
        rw, jnp.pad(rwkv_mu[0], (0, RWKV_IN_PAD - RWKV_IN)).reshape(1, RWKV_IN_PAD),
        rwkv_w0[0].reshape(2, 1, c_w), rwkv_w_up[0], rwkv_a0[0].reshape(2, 1, c_w), rwkv_a_up[0],
        jnp.pad(rwkv_g_up[0], ((0, RWKV_IN_PAD - RWKV_IN), (0, 0))),
        rwkv_k_k[0].reshape(1, c_w), rwkv_k_a[0].reshape(1, c_w), rwkv_r_k[0].reshape(1, c_w), seg, tm)
    scan_operands = [_to_scan_layout(a) for a in (r, w, kd, v, z, bb)]

    cos, sin = _rope_tables(l)
    y_attn = attn(q, kv, cos, sin, attn_q_norm[0][perm].reshape(1, HEAD_DIM),
                  attn_k_norm[0][perm].reshape(1, HEAD_DIM), tm)

    y_f, y_b = rwkv_scan(scan_operands, min(SCAN_BLOCK, l))
    x = mix_out([y_attn, (_from_scan_layout(y_f), 0), (_from_scan_layout(y_b), 1), bonus, g],
                [rwkv_ln_g[0].reshape(1, c_w), rwkv_ln_b[0].reshape(1, c_w), seg],
                x, mod4, 0, mix_w_out[0].astype(BF16), tm, True)
    x = ffn(x, mod4, 0, norm_ffn[0], ffn_w1[0].astype(BF16), ffn_w3[0].astype(BF16), ffn_w2[0].astype(BF16),
            final_norm, min(FFN_ROWS, l), False)

    (p3,) = proj(x, mod4, 1, norm_mix[1], [hy_w_in[0].astype(BF16)], tm)
    tn = min(512, d)
    feats, window = _hyena_tables(l, d)
    k_pad = 128
    feats = jnp.pad(feats, ((0, 0), (0, k_pad - feats.shape[1])))
    f_w1 = jnp.pad(hy_f_w1[0], ((0, k_pad - hy_f_w1.shape[1]), (0, 0)))
    taps = hy_filter(feats, f_w1, hy_f_b1[0], hy_f_w2[0], hy_f_b2[0], hy_f_w3[0], hy_f_b3[0],
                     hy_sin_freq[0], hy_f_out[0], window, tn)
    fwd, inv = _dft_matrices(l // 2)
    ft = min(FREQ_TILE, l // 2)
    spec = dft_spec(fwd, taps, d, ft, tn)
    conv_w, conv_b = hy_conv_w[0], hy_conv_b[0].reshape(1, 3 * d)
    y1 = dft_mul(fwd, p3, 0, spec, 0, d, ft, tn, conv=(conv_w, conv_b))
    z1 = dft_inv(inv, y1, p3, 0, p3, 1, hy_skip[0, 0], ft, tn, conv_w, conv_b, True)
    y2 = dft_mul(fwd, z1, 0, spec, 1, d, ft, tn)
    z2 = dft_inv(inv, y2, z1, 0, p3, 2, hy_skip[0, 1], ft, tn, conv_w, conv_b, False)
    x = mix_out([z2], [], x, mod4, 1, hy_w_out[0].astype(BF16), tm, False)
    x = ffn(x, mod4, 1, norm_ffn[1], ffn_w1[1].astype(BF16), ffn_w3[1].astype(BF16), ffn_w2[1].astype(BF16),
            final_norm, min(FFN_ROWS, l), True)
    return x
```

```python
import functools
import math

import jax
import jax.numpy as jnp
from jax import lax
from jax.experimental import pallas as pl
from jax.experimental.pallas import tpu as pltpu

F32 = jnp.float32
BF16 = jnp.bfloat16
HIGHEST = lax.Precision.HIGHEST

GRID_W = 64
HEAD_DIM = 64
ATTN_HEADS = 8
ATTN_KV_HEADS = 2
ATTN_WIDTH = ATTN_HEADS * HEAD_DIM
KV_WIDTH = ATTN_KV_HEADS * HEAD_DIM
ROPE_THETA = 10000.0
RWKV_HEADS = 8
RWKV_HEAD = 64
RWKV_WIDTH = RWKV_HEADS * RWKV_HEAD
DECAY_LORA = 64
AAA_LORA = 64
GATE_LORA = 160
RWKV_IN = 3 * RWKV_WIDTH + 2 * DECAY_LORA + 2 * AAA_LORA + GATE_LORA
RWKV_IN_PAD = 2048
GN_EPS = 64e-5
HYENA_BANDS = 16
HYENA_TARGET = 1e-2
HYENA_FAST_DECAY = 0.3
HYENA_SLOW_DECAY = 1.5
NORM_EPS = 1e-6

V7X_VMEM_BYTES = 64 * 1024 * 1024
VMEM_LIMIT = V7X_VMEM_BYTES - 8 * 1024 * 1024
FREQ_TILE = 512
SCAN_BLOCK = 32
FFN_ROWS = 512
SUBLANES = 8


def _params(*sem):
    return pltpu.CompilerParams(dimension_semantics=sem, vmem_limit_bytes=VMEM_LIMIT)


def _const_spec(shape):
    zeros = (0,) * len(shape)
    return pl.BlockSpec(shape, lambda *_: zeros, pipeline_mode=pl.Buffered(1))


def _dot(a, b):
    return jnp.dot(a, b, preferred_element_type=F32)


def _split(x):
    hi = x.astype(BF16)
    return hi, (x - hi.astype(F32)).astype(BF16)


def _dot_split(a, b):
    ah, al = _split(a)
    bh, bl = _split(b)
    return _dot(ah, bh) + (_dot(ah, bl) + _dot(al, bh))


def _seg_sum(a, seg):
    ah, al = _split(a)
    return _dot(ah, seg) + _dot(al, seg)


def _dot_hi(a, b):
    return jnp.dot(a, b, precision=HIGHEST, preferred_element_type=F32)


def _rms_mod(x, gain, scale, shift):
    ms = jnp.mean(x * x, axis=-1, keepdims=True)
    return (x * lax.rsqrt(ms + NORM_EPS) * gain) * (1.0 + scale) + shift


def _ada_kernel(c_ref, w_ref, b_ref, o_ref):
    c = c_ref[...]
    cond = c * jax.nn.sigmoid(c)
    o_ref[...] = _dot_hi(cond, w_ref[...]) + b_ref[...]


def ada_mod(c, ada_w, ada_b):
    depth, d, n = ada_w.shape
    b = c.shape[0]
    tn = 1536
    return pl.pallas_call(
        _ada_kernel,
        grid=(depth, n // tn),
        in_specs=[pl.BlockSpec((b, d), lambda l, j: (0, 0)),
                  pl.BlockSpec((None, d, tn), lambda l, j: (l, 0, j)),
                  pl.BlockSpec((None, 1, tn), lambda l, j: (l, 0, j))],
        out_specs=pl.BlockSpec((None, b, tn), lambda l, j: (l, 0, j)),
        out_shape=jax.ShapeDtypeStruct((depth, b, n), F32),
        compiler_params=_params("parallel", "parallel"),
        name="ada_mod",
    )(c, ada_w, ada_b.reshape(depth, 1, n))


def _mod_spec(d, layer, chunk):
    return pl.BlockSpec((None, None, 1, d), lambda b, *_: (layer, b, 0, chunk))


def _proj_kernel(n_out, x_ref, sh_ref, sc_ref, g_ref, *refs):
    h = _rms_mod(x_ref[0], g_ref[...], sc_ref[...], sh_ref[...]).astype(BF16)
    for w_ref, o_ref in zip(refs[:n_out], refs[n_out:]):
        o_ref[0] = _dot(h, w_ref[...])


def proj(x, mod4, layer, gain, weights, tm):
    b, l, d = x.shape
    n_out = len(weights)
    return pl.pallas_call(
        functools.partial(_proj_kernel, n_out),
        grid=(b, l // tm),
        in_specs=[pl.BlockSpec((1, tm, d), lambda bi, i: (bi, i, 0)),
                  _mod_spec(d, layer, 0), _mod_spec(d, layer, 1),
                  _const_spec((1, d))] + [_const_spec(w.shape) for w in weights],
        out_specs=[pl.BlockSpec((1, tm, w.shape[1]), lambda bi, i: (bi, i, 0)) for w in weights],
        out_shape=[jax.ShapeDtypeStruct((b, l, w.shape[1]), F32) for w in weights],
        compiler_params=_params("parallel", "parallel"),
        name="proj",
    )(x, mod4, mod4, gain.reshape(1, d), *weights)


def _headnorm_rope(x, gain, cos, sin):
    ms = jnp.mean(x * x, axis=-1, keepdims=True)
    y = x * lax.rsqrt(ms + NORM_EPS) * gain
    half = HEAD_DIM // 2
    rot = jnp.concatenate([-y[:, half:], y[:, :half]], axis=-1)
    return y * cos + rot * sin


def _attn_kernel(q_ref, kv_ref, cq_ref, sq_ref, ck_ref, sk_ref, qn_ref, kn_ref, o_ref, kt_s, v_s):
    @pl.when(pl.program_id(1) == 0)
    def _():
        kv = kv_ref[0]
        ones_col = (lax.broadcasted_iota(jnp.int32, (kv.shape[0], HEAD_DIM), 1) == 0).astype(F32)
        for hk in range(ATTN_KV_HEADS):
            k = kv[:, HEAD_DIM * hk:HEAD_DIM * (hk + 1)]
            kt_s[hk] = _headnorm_rope(k, kn_ref[...], ck_ref[...], sk_ref[...]).T.astype(BF16)
            v = kv[:, KV_WIDTH + HEAD_DIM * hk:KV_WIDTH + HEAD_DIM * (hk + 1)]
            v_s[hk] = jnp.concatenate([v, ones_col], axis=-1).astype(BF16)

    q = q_ref[0]
    group = ATTN_HEADS // ATTN_KV_HEADS
    outs = []
    for h in range(ATTN_HEADS):
        qh = _headnorm_rope(q[:, HEAD_DIM * h:HEAD_DIM * (h + 1)], qn_ref[...], cq_ref[...], sq_ref[...])
        qh = (qh * (HEAD_DIM ** -0.5 * math.log2(math.e))).astype(BF16)
        s = _dot(qh, kt_s[h // group])
        p = jnp.exp2(s - jnp.max(s, axis=-1, keepdims=True)).astype(BF16)
        o = _dot(p, v_s[h // group])
        outs.append(o[:, :HEAD_DIM] / o[:, HEAD_DIM:HEAD_DIM + 1])
    o_ref[0] = jnp.concatenate(outs, axis=-1)


def attn(q, kv, cos, sin, qn, kn, tq):
    b, l, _ = q.shape
    return pl.pallas_call(
        _attn_kernel,
        grid=(b, l // tq),
        in_specs=[pl.BlockSpec((1, tq, ATTN_WIDTH), lambda bi, i: (bi, i, 0)),
                  pl.BlockSpec((1, l, 2 * KV_WIDTH), lambda bi, i: (bi, 0, 0)),
                  pl.BlockSpec((tq, HEAD_DIM), lambda bi, i: (i, 0)),
                  pl.BlockSpec((tq, HEAD_DIM), lambda bi, i: (i, 0)),
                  _const_spec((l, HEAD_DIM)), _const_spec((l, HEAD_DIM)),
                  _const_spec((1, HEAD_DIM)), _const_spec((1, HEAD_DIM))],
        out_specs=pl.BlockSpec((1, tq, ATTN_WIDTH), lambda bi, i: (bi, i, 0)),
        out_shape=jax.ShapeDtypeStruct((b, l, ATTN_WIDTH), F32),
        scratch_shapes=[pltpu.VMEM((ATTN_KV_HEADS, HEAD_DIM, l), BF16),
                        pltpu.VMEM((ATTN_KV_HEADS, l, 2 * HEAD_DIM), BF16)],
        compiler_params=_params("parallel", "arbitrary"),
        name="attn",
    )(q, kv, cos, sin, cos, sin, qn, kn)


def _halo_specs(tt, w, l):
    nb = l // SUBLANES
    per = tt // SUBLANES
    main = pl.BlockSpec((1, tt, w), lambda bi, i: (bi, i, 0))
    prev = pl.BlockSpec((1, SUBLANES, w), lambda bi, i: (bi, jnp.maximum(i * per - 1, 0), 0))
    nxt = pl.BlockSpec((1, SUBLANES, w), lambda bi, i: (bi, jnp.minimum((i + 1) * per, nb - 1), 0))
    return [main, prev, nxt]


def _neighbours(cur, prow, nrow):
    tt = cur.shape[0]
    row = lax.broadcasted_iota(jnp.int32, cur.shape, 0)
    prev = jnp.where(row == 0, prow, pltpu.roll(cur, 1, 0))
    nxt = jnp.where(row == tt - 1, nrow, pltpu.roll(cur, tt - 1, 0))
    return prev, nxt


def _halo_rows(xp_ref, xn_ref, axis):
    i = pl.program_id(axis)
    last = pl.num_programs(axis) - 1
    prow = jnp.where(i > 0, xp_ref[0, SUBLANES - 1:SUBLANES, :], 0.0)
    nrow = jnp.where(i < last, xn_ref[0, 0:1, :], 0.0)
    return prow, nrow


def _rwkv_prep_kernel(x_ref, xp_ref, xn_ref, mu_ref, w0_ref, wup_ref, a0_ref, aup_ref, gup_ref,
                      kk_ref, ka_ref, rk_ref, seg_ref,
                      r_o, z_o, v_o, w_o, kd_o, b_o, bonus_o, g_o):
    cur = x_ref[0]
    prev, nxt = _neighbours(cur, *_halo_rows(xp_ref, xn_ref, 1))
    ps = cur + (0.5 * (prev + nxt) - cur) * mu_ref[...]
    c = RWKV_WIDTH
    r = ps[:, 0:c]
    k = ps[:, c:2 * c]
    v = ps[:, 2 * c:3 * c]
    o_a = 3 * c + 2 * DECAY_LORA
    o_g = o_a + 2 * AAA_LORA
    kk = k * kk_ref[...]
    kk = kk * lax.rsqrt(jnp.maximum(_seg_sum(kk * kk, seg_ref[...]), 1e-24))
    for d in range(2):
        cols = slice(c * d, c * (d + 1))
        r_o[:, cols] = r
        v_o[:, cols] = v
        z_o[:, cols] = -kk
        w_lo = ps[:, 3 * c + DECAY_LORA * d:3 * c + DECAY_LORA * (d + 1)]
        a_lo = ps[:, o_a + AAA_LORA * d:o_a + AAA_LORA * (d + 1)]
        decay = jnp.exp(-math.exp(-0.5) * jax.nn.sigmoid(w0_ref[d] + _dot_split(jnp.tanh(w_lo), wup_ref[d])))
        a = jax.nn.sigmoid(a0_ref[d] + _dot_split(a_lo, aup_ref[d]))
        w_o[:, cols] = decay
        kd_o[:, cols] = k * (1.0 + (a - 1.0) * ka_ref[...])
        b_o[:, cols] = kk * a
    bonus_o[0] = _seg_sum(r * k * rk_ref[...], seg_ref[...]) * v
    g_o[0] = _dot_split(jax.nn.sigmoid(ps[:, o_g:RWKV_IN_PAD]), gup_ref[...])


def rwkv_prep(rw, mu, w0, w_up, a0, a_up, g_up, k_k, k_a, r_k, seg, tt):
    b, l, w = rw.shape
    c = RWKV_WIDTH
    natural = jax.ShapeDtypeStruct((b, l, c), F32)
    time_major = jax.ShapeDtypeStruct((l, b * 2 * c), F32)
    spec_n = pl.BlockSpec((1, tt, c), lambda bi, i: (bi, i, 0))
    spec_t = pl.BlockSpec((tt, 2 * c), lambda bi, i: (i, bi))
    consts = [mu, w0, w_up, a0, a_up, g_up, k_k, k_a, r_k, seg]
    return pl.pallas_call(
        _rwkv_prep_kernel,
        grid=(b, l // tt),
        in_specs=_halo_specs(tt, w, l) + [_const_spec(a.shape) for a in consts],
        out_specs=[spec_t] * 6 + [spec_n] * 2,
        out_shape=[time_major] * 6 + [natural] * 2,
        compiler_params=_params("parallel", "parallel"),
        name="rwkv_prep",
    )(rw, rw, rw, *consts)


SCAN_ROWS = 32
_R, _W, _KD, _V, _Z, _B = range(6)


def _rwkv_scan_kernel(*refs):
    ins, (yf_ref, yb_ref, s_ref, m_ref, sz_ref) = refs[:12], refs[-5:]
    tb, n, lanes = ins[0].shape
    fwd_lane = (lax.broadcasted_iota(jnp.int32, (n, lanes), 1) // RWKV_HEADS) % 2 == 0

    def stage(which, t):
        m_ref[which] = jnp.where(fwd_lane, ins[2 * which][t], ins[2 * which + 1][tb - 1 - t])

    @pl.when(pl.program_id(0) == 0)
    def _():
        s_ref[...] = jnp.zeros_like(s_ref)

    halves = [slice(h * SCAN_ROWS, (h + 1) * SCAN_ROWS) for h in range(n // SCAN_ROWS)]

    stage(_Z, 0)
    for rows in halves:
        acc = s_ref[0, rows, :] * m_ref[_Z, 0:1, :]
        for k in range(1, n):
            acc = acc + s_ref[k, rows, :] * m_ref[_Z, k:k + 1, :]
        sz_ref[rows, :] = acc

    def step(t, carry):
        for which in (_R, _W, _KD, _V, _B):
            stage(which, t)
        stage(_Z, jnp.minimum(t + 1, tb - 1))
        for rows in halves:
            sz = sz_ref[rows, :]
            vt = m_ref[_V, rows, :]
            y = None
            sz_next = None
            for k in range(n):
                sk = (s_ref[k, rows, :] * m_ref[_W, k:k + 1, :] + sz * m_ref[_B, k:k + 1, :]
                      + vt * m_ref[_KD, k:k + 1, :])
                s_ref[k, rows, :] = sk
                yk = sk * m_ref[_R, k:k + 1, :]
                zk = sk * m_ref[_Z, k:k + 1, :]
                y = yk if y is None else y + yk
                sz_next = zk if sz_next is None else sz_next + zk
            sz_ref[rows, :] = sz_next
            yf_ref[t, rows, :] = y
            yb_ref[tb - 1 - t, rows, :] = y
        return carry

    lax.fori_loop(0, tb, step, 0)


def rwkv_scan(arrays, tb, after=()):
    l, n, lanes = arrays[0].shape
    nt = l // tb
    fwd = pl.BlockSpec((tb, n, lanes), lambda i: (i, 0, 0))
    rev = pl.BlockSpec((tb, n, lanes), lambda i: (nt - 1 - i, 0, 0))
    out = jax.ShapeDtypeStruct((l, n, lanes), F32)
    anchors = [pl.BlockSpec((1,) * (a.ndim - 2) + (SUBLANES, 128), lambda i, nd=a.ndim: (0,) * nd) for a in after]
    return pl.pallas_call(
        _rwkv_scan_kernel,
        grid=(nt,),
        in_specs=[fwd, rev] * 6 + anchors,
        out_specs=[fwd, rev],
        out_shape=[out, out],
        scratch_shapes=[pltpu.VMEM((n, n, lanes), F32), pltpu.VMEM((6, n, lanes), F32),
                        pltpu.VMEM((n, lanes), F32)],
        compiler_params=_params("arbitrary"),
        name="rwkv_scan",
    )(*[a for a in arrays for _ in range(2)], *after)


def _mix_out_kernel(rwkv, *refs):
    if rwkv:
        (ya_ref, yf_ref, yb_ref, bonus_ref, g_ref, lng_ref, lnb_ref, seg_ref,
         x_ref, gate_ref, w_ref, o_ref) = refs
        y = yf_ref[...] + yb_ref[...]
        inv_n = 1.0 / RWKV_HEAD
        mean = _seg_sum(y, seg_ref[...]) * inv_n
        cen = y - mean
        var = _seg_sum(cen * cen, seg_ref[...]) * inv_n
        yn = cen * lax.rsqrt(var + GN_EPS) * lng_ref[...] + lnb_ref[...]
        yr = (yn + bonus_ref[0]) * g_ref[0]
        a = jnp.concatenate([ya_ref[0], yr], axis=-1)
    else:
        a_ref, x_ref, gate_ref, w_ref, o_ref = refs
        a = a_ref[0]
    o_ref[0] = x_ref[0] + gate_ref[...] * _dot(a.astype(BF16), w_ref[...])


def mix_out(acts, consts, x, mod4, layer, w, tm, rwkv):
    b, l, d = x.shape
    in_specs = []
    for i_act, a in enumerate(acts):
        if isinstance(a, tuple):
            arr, direction = a
            width = arr.shape[1] // (2 * b)
            in_specs.append(pl.BlockSpec((tm, width), lambda bi, i, dr=direction: (i, 2 * bi + dr)))
            acts[i_act] = arr
        else:
            in_specs.append(pl.BlockSpec((1, tm, a.shape[2]), lambda bi, i: (bi, i, 0)))
    in_specs += [_const_spec(a.shape) for a in consts]
    in_specs += [pl.BlockSpec((1, tm, d), lambda bi, i: (bi, i, 0)), _mod_spec(d, layer, 2), _const_spec(w.shape)]
    return pl.pallas_call(
        functools.partial(_mix_out_kernel, rwkv),
        grid=(b, l // tm),
        in_specs=in_specs,
        out_specs=pl.BlockSpec((1, tm, d), lambda bi, i: (bi, i, 0)),
        out_shape=jax.ShapeDtypeStruct((b, l, d), F32),
        compiler_params=_params("parallel", "parallel"),
        name="mix_out",
    )(*acts, *consts, x, mod4, w)


def _ffn_kernel(final, x_ref, sh_ref, sc_ref, gate_ref, g_ref, w1_ref, w3_ref, w2_ref, fg_ref, o_ref):
    x = x_ref[0]
    h = _rms_mod(x, g_ref[...], sc_ref[...], sh_ref[...]).astype(BF16)
    a = _dot(h, w1_ref[...])
    u = (a * jax.nn.sigmoid(a)) * _dot(h, w3_ref[...])
    out = x + gate_ref[...] * _dot(u.astype(BF16), w2_ref[...])
    if final:
        ms = jnp.mean(out * out, axis=-1, keepdims=True)
        out = out * lax.rsqrt(ms + NORM_EPS) * fg_ref[...]
    o_ref[0] = out


def ffn(x, mod4, layer, gain, w1, w3, w2, final_gain, tm, final):
    b, l, d = x.shape
    return pl.pallas_call(
        functools.partial(_ffn_kernel, final),
        grid=(b, l // tm),
        in_specs=[pl.BlockSpec((1, tm, d), lambda bi, i: (bi, i, 0)),
                  _mod_spec(d, layer, 3), _mod_spec(d, layer, 4), _mod_spec(d, layer, 5),
                  _const_spec((1, d)), _const_spec(w1.shape), _const_spec(w3.shape), _const_spec(w2.shape),
                  _const_spec((1, d))],
        out_specs=pl.BlockSpec((1, tm, d), lambda bi, i: (bi, i, 0)),
        out_shape=jax.ShapeDtypeStruct((b, l, d), F32),
        compiler_params=_params("parallel", "parallel"),
        name="ffn",
    )(x, mod4, mod4, mod4, gain.reshape(1, d), w1, w3, w2, final_gain.reshape(1, d))


def _conv3(cur, prow, nrow, w_ref, b_ref):
    prev, nxt = _neighbours(cur, prow, nrow)
    return prev * w_ref[0:1, :] + cur * w_ref[1:2, :] + nxt * w_ref[2:3, :] + b_ref[...]


def _hy_filter_kernel(d_tiles, feats_ref, w1_ref, b1_ref, w2_ref, b2_ref, w3_ref, b3_ref, fr_ref,
                      wout_ref, win_ref, o_ref, hid_s):
    j = pl.program_id(0)

    @pl.when(j == 0)
    def _():
        fr = fr_ref[...]
        hid = jnp.sin(fr * (_dot_hi(feats_ref[...], w1_ref[...]) + b1_ref[...]))
        hid = jnp.sin(fr * (_dot_hi(hid, w2_ref[...]) + b2_ref[...]))
        hid_s[...] = jnp.sin(fr * (_dot_hi(hid, w3_ref[...]) + b3_ref[...]))

    filt = _dot_hi(hid_s[...], wout_ref[...]) * win_ref[...]
    backward = (j // d_tiles) % 2 == 1
    row = lax.broadcasted_iota(jnp.int32, filt.shape, 0)
    o_ref[...] = jnp.where(jnp.logical_and(backward, row == 0), 0.0, filt)


def hy_filter(feats, w1, b1, w2, b2, w3, b3, fr, w_out, window, tn):
    l = feats.shape[0]
    hf = w2.shape[0]
    n = w_out.shape[1]
    d = window.shape[1]
    d_tiles = d // tn
    consts = [feats, w1, b1.reshape(1, hf), w2, b2.reshape(1, hf), w3, b3.reshape(1, hf), fr.reshape(1, hf)]
    return pl.pallas_call(
        functools.partial(_hy_filter_kernel, d_tiles),
        grid=(n // tn,),
        in_specs=[_const_spec(a.shape) for a in consts]
        + [pl.BlockSpec((hf, tn), lambda j: (0, j)), pl.BlockSpec((l, tn), lambda j: (0, j % d_tiles))],
        out_specs=pl.BlockSpec((l, tn), lambda j: (0, j)),
        out_shape=jax.ShapeDtypeStruct((l, n), F32),
        scratch_shapes=[pltpu.VMEM((l, hf), F32)],
        compiler_params=_params("arbitrary"),
        name="hy_filter",
    )(*consts, w_out, window)


def _dft_matrices(l):
    n = 2 * l
    lane = min(128, l)
    k = jnp.arange(l, dtype=jnp.int32)[:, None]
    t = jnp.arange(l, dtype=jnp.int32)
    a = ((k * jnp.arange(lane, dtype=jnp.int32)[None, :]) % n).astype(F32) * (2.0 * math.pi / n)
    c = ((k * (lane * jnp.arange(l // lane, dtype=jnp.int32))[None, :]) % n).astype(F32) * (2.0 * math.pi / n)
    ca, sa, cc, sc = jnp.cos(a)[:, None, :], jnp.sin(a)[:, None, :], jnp.cos(c)[:, :, None], jnp.sin(c)[:, :, None]
    cos = (ca * cc - sa * sc).reshape(l, l)
    sin = (sa * cc + ca * sc).reshape(l, l)
    alt = jnp.where(t % 2 == 0, 1.0, -1.0).astype(F32)
    f_re = cos.astype(BF16)
    f_im = jnp.where(k == 0, alt[None, :], -sin).astype(BF16)
    col0 = (t == 0)[None, :]
    g_re = jnp.where(col0, 1.0 / n, (2.0 / n) * cos).astype(BF16)
    g_im = jnp.where(col0, alt[:, None] / n, (-2.0 / n) * sin).astype(BF16)
    return (f_re, f_im), (g_re, g_im)


def _nyquist_slot(i, shape):
    row = lax.broadcasted_iota(jnp.int32, shape, 0)
    return jnp.logical_and(i == 0, row == 0)


def _dft_spec_kernel(fre_ref, fim_ref, ft_ref, fb_ref, bt_ref, bb_ref, o_ref):
    fre, fim = fre_ref[...], fim_ref[...]
    ft, fb, bt, bb = (ref[...].astype(BF16) for ref in (ft_ref, fb_ref, bt_ref, bb_ref))
    af_r, af_i = _dot(fre, ft), _dot(fim, ft)
    bf_r, bf_i = _dot(fre, fb), _dot(fim, fb)
    ab_r, ab_i = _dot(fre, bt), _dot(fim, bt)
    bb_r, bb_i = _dot(fre, bb), _dot(fim, bb)
    nyq = _nyquist_slot(pl.program_id(1), af_r.shape)
    row = lax.broadcasted_iota(jnp.int32, af_r.shape, 0)
    sgn = jnp.where(row % 2 == 0, 1.0, -1.0)
    lag0 = ft_ref[0:1, :]
    o_ref[0, 0] = af_r + ab_r
    o_ref[0, 1] = jnp.where(nyq, af_i + ab_i, af_i - ab_i)
    o_ref[1, 0] = bf_r + sgn * (af_r - lag0)
    o_ref[1, 1] = bf_i + jnp.where(nyq, af_i - lag0, sgn * af_i)
    o_ref[2, 0] = sgn * ab_r + bb_r
    o_ref[2, 1] = jnp.where(nyq, ab_i + bb_i, -(sgn * ab_i + bb_i))


def dft_spec(fwd, taps, d, ft, tn):
    f_re, f_im = fwd
    m = f_re.shape[0]
    d_tiles = d // tn
    orders = taps.shape[1] // (2 * d)
    f_spec = pl.BlockSpec((ft, m), lambda j, i: (i, 0))

    def tap_spec(half, direction):
        return pl.BlockSpec((m, tn), lambda j, i: (half, (j // d_tiles) * 2 * d_tiles + direction * d_tiles
                                                    + j % d_tiles))

    return pl.pallas_call(
        _dft_spec_kernel,
        grid=(orders * d_tiles, m // ft),
        in_specs=[f_spec, f_spec, tap_spec(0, 0), tap_spec(1, 0), tap_spec(0, 1), tap_spec(1, 1)],
        out_specs=pl.BlockSpec((3, 2, ft, tn), lambda j, i: (0, 0, i, j)),
        out_shape=jax.ShapeDtypeStruct((3, 2, m, orders * d), F32),
        compiler_params=_params("parallel", "parallel"),
        name="dft_spec",
    )(f_re, f_im, taps, taps, taps, taps)


def _dft_mul_kernel(has_conv, fre_ref, fim_ref, u_ref, h_ref, *refs):
    o_ref, u_s = refs[-2:]
    i = pl.program_id(2)

    @pl.when(i == 0)
    def _():
        u = u_ref[0]
        if has_conv:
            u = _conv3(u, 0.0, 0.0, *refs[:2])
        u_s[...] = u.astype(BF16)

    m = u_s.shape[0] // 2
    fre, fim = fre_ref[...], fim_ref[...]
    tr, ti = _dot(fre, u_s[0:m]), _dot(fim, u_s[0:m])
    br, bi = _dot(fre, u_s[m:]), _dot(fim, u_s[m:])
    h0r, h0i, hpr, hpi, hmr, hmi = (h_ref[a, p] for a in range(3) for p in range(2))
    head = 2 * SUBLANES
    nyq = _nyquist_slot(i, (head, tr.shape[1]))

    def emit(half, plane, general, packed):
        o_ref[0, half, plane, 0:head] = jnp.where(nyq, packed, general[0:head]).astype(BF16)
        o_ref[0, half, plane, head:] = general[head:].astype(BF16)

    def top(x):
        return x[0:head]

    emit(0, 0, h0r * tr - h0i * ti + hmr * br - hmi * bi, top(h0r) * top(tr) + top(hmr) * top(br))
    emit(0, 1, h0r * ti + h0i * tr + hmr * bi + hmi * br, top(h0i) * top(ti) + top(hmi) * top(bi))
    emit(1, 0, hpr * tr - hpi * ti + h0r * br - h0i * bi, top(hpr) * top(tr) + top(h0r) * top(br))
    emit(1, 1, hpr * ti + hpi * tr + h0r * bi + h0i * br, top(hpi) * top(ti) + top(h0i) * top(bi))


def dft_mul(fwd, u, u_col, spec, spec_col, d, ft, tn, conv=None):
    f_re, f_im = fwd
    m = f_re.shape[0]
    b = u.shape[0]
    d_tiles = d // tn
    f_spec = pl.BlockSpec((ft, m), lambda bi, j, i: (i, 0))
    conv_specs = [] if conv is None else [pl.BlockSpec((3, tn), lambda bi, j, i: (0, u_col * d_tiles + j)),
                                          pl.BlockSpec((1, tn), lambda bi, j, i: (0, u_col * d_tiles + j))]
    return pl.pallas_call(
        functools.partial(_dft_mul_kernel, conv is not None),
        grid=(b, d_tiles, m // ft),
        in_specs=[f_spec, f_spec,
                  pl.BlockSpec((1, 2 * m, tn), lambda bi, j, i: (bi, 0, u_col * d_tiles + j)),
                  pl.BlockSpec((3, 2, ft, tn), lambda bi, j, i: (0, 0, i, spec_col * d_tiles + j))] + conv_specs,
        out_specs=pl.BlockSpec((1, 2, 2, ft, tn), lambda bi, j, i: (bi, 0, 0, i, j)),
        out_shape=jax.ShapeDtypeStruct((b, 2, 2, m, d), BF16),
        scratch_shapes=[pltpu.VMEM((2 * m, tn), BF16)],
        compiler_params=_params("parallel", "parallel", "arbitrary"),
        name="dft_mul",
    )(f_re, f_im, u, spec, *(conv or ()))


def _dft_inv_kernel(conv_u, gre_ref, gim_ref, y_ref, *refs):
    u_refs, (g_ref, gp_ref, gn_ref, gw_ref, gb_ref, skip_ref, o_ref) = refs[:-7], refs[-7:]
    u = u_refs[0][0]
    if conv_u:
        u = _conv3(u, *_halo_rows(u_refs[1], u_refs[2], 2), u_refs[3], u_refs[4])
    gate = _conv3(g_ref[0], *_halo_rows(gp_ref, gn_ref, 2), gw_ref, gb_ref)
    conv = _dot(gre_ref[...], y_ref[0, 0, 0]) + _dot(gim_ref[...], y_ref[0, 0, 1])
    o_ref[0] = gate * (conv + u * skip_ref[...])


def dft_inv(inv, y, u, u_col, gate, gate_col, skip, tm, tn, conv_w, conv_b, conv_u):
    g_re, g_im = inv
    m = g_re.shape[0]
    b, _, _, _, d = y.shape
    d_tiles = d // tn
    per_half = m // tm
    rows8 = tm // SUBLANES
    last8 = 2 * m // SUBLANES - 1
    g_spec = pl.BlockSpec((tm, m), lambda bi, j, i: (i % per_half, 0))

    def tile_specs(col, halo):
        specs = [pl.BlockSpec((1, tm, tn), lambda bi, j, i: (bi, i, col * d_tiles + j))]
        if halo:
            specs += [pl.BlockSpec((1, SUBLANES, tn),
                                   lambda bi, j, i: (bi, jnp.maximum(i * rows8 - 1, 0), col * d_tiles + j)),
                      pl.BlockSpec((1, SUBLANES, tn),
                                   lambda bi, j, i: (bi, jnp.minimum((i + 1) * rows8, last8), col * d_tiles + j)),
                      pl.BlockSpec((3, tn), lambda bi, j, i: (0, col * d_tiles + j)),
                      pl.BlockSpec((1, tn), lambda bi, j, i: (0, col * d_tiles + j))]
        return specs

    u_ops = [u, u, u, conv_w, conv_b] if conv_u else [u]
    return pl.pallas_call(
        functools.partial(_dft_inv_kernel, conv_u),
        grid=(b, d_tiles, 2 * per_half),
        in_specs=[g_spec, g_spec,
                  pl.BlockSpec((1, 1, 2, m, tn), lambda bi, j, i: (bi, i // per_half, 0, 0, j))]
        + tile_specs(u_col, conv_u) + tile_specs(gate_col, True)
        + [pl.BlockSpec((1, tn), lambda bi, j, i: (0, j))],
        out_specs=pl.BlockSpec((1, tm, tn), lambda bi, j, i: (bi, i, j)),
        out_shape=jax.ShapeDtypeStruct((b, 2 * m, d), F32),
        compiler_params=_params("parallel", "parallel", "parallel"),
        name="dft_inv",
    )(g_re, g_im, y, *u_ops, gate, gate, gate, conv_w, conv_b, skip.reshape(1, d))


def _rope_tables(l):
    rows = l // GRID_W
    row = jnp.repeat(jnp.arange(rows), GRID_W).astype(F32)
    col = jnp.tile(jnp.arange(GRID_W), rows).astype(F32)
    half = HEAD_DIM // 2
    inv_freq = ROPE_THETA ** (-jnp.arange(0, half, 2, dtype=F32) / half)
    ang = jnp.concatenate([row[:, None] * inv_freq, col[:, None] * inv_freq], axis=-1)
    cos, sin = jnp.cos(ang), jnp.sin(ang)
    return jnp.concatenate([cos, cos], axis=-1), jnp.concatenate([sin, sin], axis=-1)


def _hyena_tables(l, d):
    t = jnp.linspace(0.0, 1.0, l, dtype=F32)[:, None]
    omega = (2.0 * math.pi / l) * jnp.arange(l, dtype=F32)[:, None]
    bands = jnp.linspace(1e-4, HYENA_BANDS - 1, HYENA_BANDS, dtype=F32)[None, :]
    feats = jnp.concatenate([t, jnp.cos(bands * omega), -jnp.sin(bands * omega)], axis=-1)
    min_decay = math.log(HYENA_TARGET) / HYENA_SLOW_DECAY
    max_decay = math.log(HYENA_TARGET) / HYENA_FAST_DECAY
    deltas = jnp.abs(jnp.linspace(min_decay, max_decay, d, dtype=F32))
    window = jnp.exp(-t * deltas)
    return feats, window


def _to_scan_layout(x):
    return x.reshape(x.shape[0], -1, RWKV_HEAD).transpose(0, 2, 1)


def _from_scan_layout(y):
    return y.transpose(0, 2, 1).reshape(y.shape[0], -1)


def kernel(x, c, mix_w_in, mix_w_out, attn_q_norm, attn_k_norm, rwkv_mu, rwkv_w0, rwkv_w_up, rwkv_a0, rwkv_a_up, rwkv_g_up, rwkv_k_k, rwkv_k_a, rwkv_r_k, rwkv_ln_g, rwkv_ln_b, hy_w_in, hy_conv_w, hy_conv_b, hy_f_w1, hy_f_b1, hy_f_w2, hy_f_b2, hy_f_w3, hy_f_b3, hy_sin_freq, hy_f_out, hy_skip, hy_w_out, ada_w, ada_b, norm_mix, norm_ffn, ffn_w1, ffn_w3, ffn_w2, final_norm):
    b, l, d = x.shape
    depth = ada_w.shape[0]
    tm = min(256, l)
    tp = min(FFN_ROWS, l)
    c_w = RWKV_WIDTH

    mod4 = ada_mod(c, ada_w, ada_b).reshape(depth, b, 1, 6 * d)

    perm = jnp.concatenate([jnp.arange(0, HEAD_DIM, 2), jnp.arange(1, HEAD_DIM, 2)])
    w_in = mix_w_in[0]
    w_q = w_in[:, :ATTN_WIDTH].reshape(d, ATTN_HEADS, HEAD_DIM)[:, :, perm].reshape(d, ATTN_WIDTH)
    w_kv = w_in[:, ATTN_WIDTH:ATTN_WIDTH + 2 * KV_WIDTH]
    w_k = w_kv[:, :KV_WIDTH].reshape(d, ATTN_KV_HEADS, HEAD_DIM)[:, :, perm].reshape(d, KV_WIDTH)
    w_kv = jnp.concatenate([w_k, w_kv[:, KV_WIDTH:]], axis=1)
    w_rw = jnp.pad(w_in[:, ATTN_WIDTH + 2 * KV_WIDTH:], ((0, 0), (0, RWKV_IN_PAD - RWKV_IN)))
    q, kv, rw = proj(x, mod4, 0, norm_mix[0], [w_q.astype(BF16), w_kv.astype(BF16), w_rw.astype(BF16)], tp)

    seg = jnp.kron(jnp.eye(RWKV_HEADS, dtype=BF16), jnp.ones((RWKV_HEAD, RWKV_HEAD), BF16))
    r, z, v, w, kd, bb, bonus, g = rwkv_prep(
        rw, jnp.pad(rwkv_mu[0], (0, RWKV_IN_PAD - RWKV_IN)).reshape(1, RWKV_IN_PAD),
        rwkv_w0[0].reshape(2, 1, c_w), rwkv_w_up[0], rwkv_a0[0].reshape(2, 1, c_w), rwkv_a_up[0],
        jnp.pad(rwkv_g_up[0], ((0, RWKV_IN_PAD - RWKV_IN), (0, 0))),
        rwkv_k_k[0].reshape(1, c_w), rwkv_k_a[0].reshape(1, c_w), rwkv_r_k[0].reshape(1, c_w), seg, tm)
    scan_operands = [_to_scan_layout(a) for a in (r, w, kd, v, z, bb)]

    cos, sin = _rope_tables(l)
    y_attn = attn(q, kv, cos, sin, attn_q_norm[0][perm].reshape(1, HEAD_DIM),
                  attn_k_norm[0][perm].reshape(1, HEAD_DIM), tm)
    tn = min(512, d)
    feats, window = _hyena_tables(l, d)
    k_pad = 128
    feats = jnp.pad(feats, ((0, 0), (0, k_pad - feats.shape[1])))
    f_w1 = jnp.pad(hy_f_w1[0], ((0, k_pad - hy_f_w1.shape[1]), (0, 0)))
    taps = hy_filter(feats, f_w1, hy_f_b1[0], hy_f_w2[0], hy_f_b2[0], hy_f_w3[0], hy_f_b3[0],
                     hy_sin_freq[0], hy_f_out[0], window, tn)
    fwd, inv = _dft_matrices(l // 2)
    ft = min(FREQ_TILE, l // 2)
    spec = dft_spec(fwd, taps, d, ft, tn)

    y_f, y_b = rwkv_scan(scan_operands, min(SCAN_BLOCK, l), after=(y_attn, spec))
    x = mix_out([y_attn, (_from_scan_layout(y_f), 0), (_from_scan_layout(y_b), 1), bonus, g],
                [rwkv_ln_g[0].reshape(1, c_w), rwkv_ln_b[0].reshape(1, c_w), seg],
                x, mod4, 0, mix_w_out[0].astype(BF16), tm, True)
    x = ffn(x, mod4, 0, norm_ffn[0], ffn_w1[0].astype(BF16), ffn_w3[0].astype(BF16), ffn_w2[0].astype(BF16),
            final_norm, min(FFN_ROWS, l), False)

    (p3,) = proj(x, mod4, 1, norm_mix[1], [hy_w_in[0].astype(BF16)], tp)
    conv_w, conv_b = hy_conv_w[0], hy_conv_b[0].reshape(1, 3 * d)
    y1 = dft_mul(fwd, p3, 0, spec, 0, d, ft, tn, conv=(conv_w, conv_b))
    z1 = dft_inv(inv, y1, p3, 0, p3, 1, hy_skip[0, 0], ft, tn, conv_w, conv_b, True)
    y2 = dft_mul(fwd, z1, 0, spec, 1, d, ft, tn)
    z2 = dft_inv(inv, y2, z1, 0, p3, 2, hy_skip[0, 1], ft, tn, conv_w, conv_b, False)
    x = mix_out([z2], [], x, mod4, 1, hy_w_out[0].astype(BF16), tp, False)
    x = ffn(x, mod4, 1, norm_ffn[1], ffn_w1[1].astype(BF16), ffn_w3[1].astype(BF16), ffn_w2[1].astype(BF16),
            final_norm, min(FFN_ROWS, l), True)
    return x
```

```python
import functools
import math

import jax
import jax.numpy as jnp
from jax import lax
from jax.experimental import pallas as pl
from jax.experimental.pallas import tpu as pltpu

F32 = jnp.float32
BF16 = jnp.bfloat16
HIGHEST = lax.Precision.HIGHEST

GRID_W = 64
HEAD_DIM = 64
ATTN_HEADS = 8
ATTN_KV_HEADS = 2
ATTN_WIDTH = ATTN_HEADS * HEAD_DIM
KV_WIDTH = ATTN_KV_HEADS * HEAD_DIM
ROPE_THETA = 10000.0
RWKV_HEADS = 8
RWKV_HEAD = 64
RWKV_WIDTH = RWKV_HEADS * RWKV_HEAD
DECAY_LORA = 64
AAA_LORA = 64
GATE_LORA = 160
RWKV_IN = 3 * RWKV_WIDTH + 2 * DECAY_LORA + 2 * AAA_LORA + GATE_LORA
RWKV_IN_PAD = 2048
GN_EPS = 64e-5
HYENA_BANDS = 16
HYENA_TARGET = 1e-2
HYENA_FAST_DECAY = 0.3
HYENA_SLOW_DECAY = 1.5
NORM_EPS = 1e-6

V7X_VMEM_BYTES = 64 * 1024 * 1024
VMEM_LIMIT = V7X_VMEM_BYTES - 8 * 1024 * 1024
FREQ_TILE = 512
SCAN_BLOCK = 32
FFN_ROWS = 512
SUBLANES = 8


def _params(*sem):
    return pltpu.CompilerParams(dimension_semantics=sem, vmem_limit_bytes=VMEM_LIMIT)


def _const_spec(shape):
    zeros = (0,) * len(shape)
    return pl.BlockSpec(shape, lambda *_: zeros, pipeline_mode=pl.Buffered(1))


def _dot(a, b):
    return jnp.dot(a, b, preferred_element_type=F32)


def _split(x):
    hi = x.astype(BF16)
    return hi, (x - hi.astype(F32)).astype(BF16)


def _dot_split(a, b):
    ah, al = _split(a)
    bh, bl = _split(b)
    return _dot(ah, bh) + (_dot(ah, bl) + _dot(al, bh))


def _seg_sum(a, seg):
    ah, al = _split(a)
    return _dot(ah, seg) + _dot(al, seg)


def _dot_hi(a, b):
    return jnp.dot(a, b, precision=HIGHEST, preferred_element_type=F32)


def _rms_mod(x, gain, scale, shift):
    ms = jnp.mean(x * x, axis=-1, keepdims=True)
    return (x * lax.rsqrt(ms + NORM_EPS) * gain) * (1.0 + scale) + shift


def _ada_kernel(c_ref, w_ref, b_ref, o_ref):
    c = c_ref[...]
    cond = c * jax.nn.sigmoid(c)
    o_ref[...] = _dot_hi(cond, w_ref[...]) + b_ref[...]


def ada_mod(c, ada_w, ada_b):
    depth, d, n = ada_w.shape
    b = c.shape[0]
    tn = 1536
    return pl.pallas_call(
        _ada_kernel,
        grid=(depth, n // tn),
        in_specs=[pl.BlockSpec((b, d), lambda l, j: (0, 0)),
                  pl.BlockSpec((None, d, tn), lambda l, j: (l, 0, j)),
                  pl.BlockSpec((None, 1, tn), lambda l, j: (l, 0, j))],
        out_specs=pl.BlockSpec((None, b, tn), lambda l, j: (l, 0, j)),
        out_shape=jax.ShapeDtypeStruct((depth, b, n), F32),
        compiler_params=_params("parallel", "parallel"),
        name="ada_mod",
    )(c, ada_w, ada_b.reshape(depth, 1, n))


def _mod_spec(d, layer, chunk):
    return pl.BlockSpec((None, None, 1, d), lambda b, *_: (layer, b, 0, chunk))


def _proj_kernel(n_out, x_ref, sh_ref, sc_ref, g_ref, *refs):
    h = _rms_mod(x_ref[0], g_ref[...], sc_ref[...], sh_ref[...]).astype(BF16)
    for w_ref, o_ref in zip(refs[:n_out], refs[n_out:]):
        o_ref[0] = _dot(h, w_ref[...])


def proj(x, mod4, layer, gain, weights, tm):
    b, l, d = x.shape
    n_out = len(weights)
    return pl.pallas_call(
        functools.partial(_proj_kernel, n_out),
        grid=(b, l // tm),
        in_specs=[pl.BlockSpec((1, tm, d), lambda bi, i: (bi, i, 0)),
                  _mod_spec(d, layer, 0), _mod_spec(d, layer, 1),
                  _const_spec((1, d))] + [_const_spec(w.shape) for w in weights],
        out_specs=[pl.BlockSpec((1, tm, w.shape[1]), lambda bi, i: (bi, i, 0)) for w in weights],
        out_shape=[jax.ShapeDtypeStruct((b, l, w.shape[1]), F32) for w in weights],
        compiler_params=_params("parallel", "parallel"),
        name="proj",
    )(x, mod4, mod4, gain.reshape(1, d), *weights)


def _headnorm_rope(x, gain, cos, sin):
    ms = jnp.mean(x * x, axis=-1, keepdims=True)
    y = x * lax.rsqrt(ms + NORM_EPS) * gain
    half = HEAD_DIM // 2
    rot = jnp.concatenate([-y[:, half:], y[:, :half]], axis=-1)
    return y * cos + rot * sin


def _attn_kernel(q_ref, kv_ref, cq_ref, sq_ref, ck_ref, sk_ref, qn_ref, kn_ref, o_ref, kt_s, v_s):
    @pl.when(pl.program_id(1) == 0)
    def _():
        kv = kv_ref[0]
        ones_col = (lax.broadcasted_iota(jnp.int32, (kv.shape[0], HEAD_DIM), 1) == 0).astype(F32)
        for hk in range(ATTN_KV_HEADS):
            k = kv[:, HEAD_DIM * hk:HEAD_DIM * (hk + 1)]
            kt_s[hk] = _headnorm_rope(k, kn_ref[...], ck_ref[...], sk_ref[...]).T.astype(BF16)
            v = kv[:, KV_WIDTH + HEAD_DIM * hk:KV_WIDTH + HEAD_DIM * (hk + 1)]
            v_s[hk] = jnp.concatenate([v, ones_col], axis=-1).astype(BF16)

    q = q_ref[0]
    group = ATTN_HEADS // ATTN_KV_HEADS
    outs = []
    for h in range(ATTN_HEADS):
        qh = _headnorm_rope(q[:, HEAD_DIM * h:HEAD_DIM * (h + 1)], qn_ref[...], cq_ref[...], sq_ref[...])
        qh = (qh * (HEAD_DIM ** -0.5 * math.log2(math.e))).astype(BF16)
        s = _dot(qh, kt_s[h // group])
        p = jnp.exp2(s - jnp.max(s, axis=-1, keepdims=True)).astype(BF16)
        o = _dot(p, v_s[h // group])
        outs.append(o[:, :HEAD_DIM] / o[:, HEAD_DIM:HEAD_DIM + 1])
    o_ref[0] = jnp.concatenate(outs, axis=-1)


def attn(q, kv, cos, sin, qn, kn, tq, b0, nb):
    _, l, _ = q.shape
    b = nb
    return pl.pallas_call(
        _attn_kernel,
        grid=(nb, l // tq),
        in_specs=[pl.BlockSpec((1, tq, ATTN_WIDTH), lambda bi, i: (b0 + bi, i, 0)),
                  pl.BlockSpec((1, l, 2 * KV_WIDTH), lambda bi, i: (b0 + bi, 0, 0)),
                  pl.BlockSpec((tq, HEAD_DIM), lambda bi, i: (i, 0)),
                  pl.BlockSpec((tq, HEAD_DIM), lambda bi, i: (i, 0)),
                  _const_spec((l, HEAD_DIM)), _const_spec((l, HEAD_DIM)),
                  _const_spec((1, HEAD_DIM)), _const_spec((1, HEAD_DIM))],
        out_specs=pl.BlockSpec((1, tq, ATTN_WIDTH), lambda bi, i: (bi, i, 0)),
        out_shape=jax.ShapeDtypeStruct((b, l, ATTN_WIDTH), F32),
        scratch_shapes=[pltpu.VMEM((ATTN_KV_HEADS, HEAD_DIM, l), BF16),
                        pltpu.VMEM((ATTN_KV_HEADS, l, 2 * HEAD_DIM), BF16)],
        compiler_params=_params("parallel", "arbitrary"),
        name="attn",
    )(q, kv, cos, sin, cos, sin, qn, kn)


def _halo_specs(tt, w, l):
    nb = l // SUBLANES
    per = tt // SUBLANES
    main = pl.BlockSpec((1, tt, w), lambda bi, i: (bi, i, 0))
    prev = pl.BlockSpec((1, SUBLANES, w), lambda bi, i: (bi, jnp.maximum(i * per - 1, 0), 0))
    nxt = pl.BlockSpec((1, SUBLANES, w), lambda bi, i: (bi, jnp.minimum((i + 1) * per, nb - 1), 0))
    return [main, prev, nxt]


def _neighbours(cur, prow, nrow):
    tt = cur.shape[0]
    row = lax.broadcasted_iota(jnp.int32, cur.shape, 0)
    prev = jnp.where(row == 0, prow, pltpu.roll(cur, 1, 0))
    nxt = jnp.where(row == tt - 1, nrow, pltpu.roll(cur, tt - 1, 0))
    return prev, nxt


def _halo_rows(xp_ref, xn_ref, axis):
    i = pl.program_id(axis)
    last = pl.num_programs(axis) - 1
    prow = jnp.where(i > 0, xp_ref[0, SUBLANES - 1:SUBLANES, :], 0.0)
    nrow = jnp.where(i < last, xn_ref[0, 0:1, :], 0.0)
    return prow, nrow


def _rwkv_prep_kernel(x_ref, xp_ref, xn_ref, mu_ref, w0_ref, wup_ref, a0_ref, aup_ref, gup_ref,
                      kk_ref, ka_ref, rk_ref, seg_ref,
                      r_o, z_o, v_o, w_o, kd_o, b_o, bonus_o, g_o):
    cur = x_ref[0]
    prev, nxt = _neighbours(cur, *_halo_rows(xp_ref, xn_ref, 1))
    ps = cur + (0.5 * (prev + nxt) - cur) * mu_ref[...]
    c = RWKV_WIDTH
    r = ps[:, 0:c]
    k = ps[:, c:2 * c]
    v = ps[:, 2 * c:3 * c]
    o_a = 3 * c + 2 * DECAY_LORA
    o_g = o_a + 2 * AAA_LORA
    kk = k * kk_ref[...]
    kk = kk * lax.rsqrt(jnp.maximum(_seg_sum(kk * kk, seg_ref[...]), 1e-24))
    for d in range(2):
        cols = slice(c * d, c * (d + 1))
        r_o[:, cols] = r
        v_o[:, cols] = v
        z_o[:, cols] = -kk
        w_lo = ps[:, 3 * c + DECAY_LORA * d:3 * c + DECAY_LORA * (d + 1)]
        a_lo = ps[:, o_a + AAA_LORA * d:o_a + AAA_LORA * (d + 1)]
        decay = jnp.exp(-math.exp(-0.5) * jax.nn.sigmoid(w0_ref[d] + _dot_split(jnp.tanh(w_lo), wup_ref[d])))
        a = jax.nn.sigmoid(a0_ref[d] + _dot_split(a_lo, aup_ref[d]))
        w_o[:, cols] = decay
        kd_o[:, cols] = k * (1.0 + (a - 1.0) * ka_ref[...])
        b_o[:, cols] = kk * a
    bonus_o[0] = _seg_sum(r * k * rk_ref[...], seg_ref[...]) * v
    g_o[0] = _dot_split(jax.nn.sigmoid(ps[:, o_g:RWKV_IN_PAD]), gup_ref[...])


def rwkv_prep(rw, mu, w0, w_up, a0, a_up, g_up, k_k, k_a, r_k, seg, tt):
    b, l, w = rw.shape
    c = RWKV_WIDTH
    natural = jax.ShapeDtypeStruct((b, l, c), F32)
    time_major = jax.ShapeDtypeStruct((l, b * 2 * c), F32)
    spec_n = pl.BlockSpec((1, tt, c), lambda bi, i: (bi, i, 0))
    spec_t = pl.BlockSpec((tt, 2 * c), lambda bi, i: (i, bi))
    consts = [mu, w0, w_up, a0, a_up, g_up, k_k, k_a, r_k, seg]
    return pl.pallas_call(
        _rwkv_prep_kernel,
        grid=(b, l // tt),
        in_specs=_halo_specs(tt, w, l) + [_const_spec(a.shape) for a in consts],
        out_specs=[spec_t] * 6 + [spec_n] * 2,
        out_shape=[time_major] * 6 + [natural] * 2,
        compiler_params=_params("parallel", "parallel"),
        name="rwkv_prep",
    )(rw, rw, rw, *consts)


SCAN_ROWS = 32
_R, _W, _KD, _V, _Z, _B = range(6)


def _rwkv_scan_kernel(*refs):
    ins, (yf_ref, yb_ref, s_ref, m_ref, sz_ref) = refs[:12], refs[-5:]
    tb, n, lanes = ins[0].shape
    fwd_lane = (lax.broadcasted_iota(jnp.int32, (n, lanes), 1) // RWKV_HEADS) % 2 == 0

    def stage(which, t):
        m_ref[which] = jnp.where(fwd_lane, ins[2 * which][t], ins[2 * which + 1][tb - 1 - t])

    @pl.when(pl.program_id(0) == 0)
    def _():
        s_ref[...] = jnp.zeros_like(s_ref)

    halves = [slice(h * SCAN_ROWS, (h + 1) * SCAN_ROWS) for h in range(n // SCAN_ROWS)]

    stage(_Z, 0)
    for rows in halves:
        acc = s_ref[0, rows, :] * m_ref[_Z, 0:1, :]
        for k in range(1, n):
            acc = acc + s_ref[k, rows, :] * m_ref[_Z, k:k + 1, :]
        sz_ref[rows, :] = acc

    def step(t, carry):
        for which in (_R, _W, _KD, _V, _B):
            stage(which, t)
        stage(_Z, jnp.minimum(t + 1, tb - 1))
        for rows in halves:
            sz = sz_ref[rows, :]
            vt = m_ref[_V, rows, :]
            y = None
            sz_next = None
            for k in range(n):
                sk = (s_ref[k, rows, :] * m_ref[_W, k:k + 1, :] + sz * m_ref[_B, k:k + 1, :]
                      + vt * m_ref[_KD, k:k + 1, :])
                s_ref[k, rows, :] = sk
                yk = sk * m_ref[_R, k:k + 1, :]
                zk = sk * m_ref[_Z, k:k + 1, :]
                y = yk if y is None else y + yk
                sz_next = zk if sz_next is None else sz_next + zk
            sz_ref[rows, :] = sz_next
            yf_ref[t, rows, :] = y
            yb_ref[tb - 1 - t, rows, :] = y
        return carry

    lax.fori_loop(0, tb, step, 0)


def rwkv_scan(arrays, tb, after=()):
    l, n, lanes = arrays[0].shape
    nt = l // tb
    fwd = pl.BlockSpec((tb, n, lanes), lambda i: (i, 0, 0))
    rev = pl.BlockSpec((tb, n, lanes), lambda i: (nt - 1 - i, 0, 0))
    out = jax.ShapeDtypeStruct((l, n, lanes), F32)
    anchors = [pl.BlockSpec((1,) * (a.ndim - 2) + (SUBLANES, 128), lambda i, nd=a.ndim: (0,) * nd) for a in after]
    return pl.pallas_call(
        _rwkv_scan_kernel,
        grid=(nt,),
        in_specs=[fwd, rev] * 6 + anchors,
        out_specs=[fwd, rev],
        out_shape=[out, out],
        scratch_shapes=[pltpu.VMEM((n, n, lanes), F32), pltpu.VMEM((6, n, lanes), F32),
                        pltpu.VMEM((n, lanes), F32)],
        compiler_params=_params("arbitrary"),
        name="rwkv_scan",
    )(*[a for a in arrays for _ in range(2)], *after)


def _mix_out_kernel(rwkv, *refs):
    if rwkv:
        (ya0_ref, ya1_ref, yf_ref, yb_ref, bonus_ref, g_ref, lng_ref, lnb_ref, seg_ref,
         x_ref, gate_ref, w_ref, o_ref) = refs
        first_half = pl.program_id(0) < pl.num_programs(0) // 2
        y_attn = jnp.where(first_half, ya0_ref[0], ya1_ref[0])
        y = yf_ref[...] + yb_ref[...]
        inv_n = 1.0 / RWKV_HEAD
        mean = _seg_sum(y, seg_ref[...]) * inv_n
        cen = y - mean
        var = _seg_sum(cen * cen, seg_ref[...]) * inv_n
        yn = cen * lax.rsqrt(var + GN_EPS) * lng_ref[...] + lnb_ref[...]
        yr = (yn + bonus_ref[0]) * g_ref[0]
        a = jnp.concatenate([y_attn, yr], axis=-1)
    else:
        a_ref, x_ref, gate_ref, w_ref, o_ref = refs
        a = a_ref[0]
    o_ref[0] = x_ref[0] + gate_ref[...] * _dot(a.astype(BF16), w_ref[...])


def mix_out(acts, consts, x, mod4, layer, w, tm, rwkv):
    b, l, d = x.shape
    in_specs = []
    for i_act, a in enumerate(acts):
        if not isinstance(a, tuple) and a.shape[0] == b // 2:
            in_specs.append(pl.BlockSpec((1, tm, a.shape[2]), lambda bi, i: (bi % (b // 2), i, 0)))
        elif isinstance(a, tuple):
            arr, direction = a
            width = arr.shape[1] // (2 * b)
            in_specs.append(pl.BlockSpec((tm, width), lambda bi, i, dr=direction: (i, 2 * bi + dr)))
            acts[i_act] = arr
        else:
            in_specs.append(pl.BlockSpec((1, tm, a.shape[2]), lambda bi, i: (bi, i, 0)))
    in_specs += [_const_spec(a.shape) for a in consts]
    in_specs += [pl.BlockSpec((1, tm, d), lambda bi, i: (bi, i, 0)), _mod_spec(d, layer, 2), _const_spec(w.shape)]
    return pl.pallas_call(
        functools.partial(_mix_out_kernel, rwkv),
        grid=(b, l // tm),
        in_specs=in_specs,
        out_specs=pl.BlockSpec((1, tm, d), lambda bi, i: (bi, i, 0)),
        out_shape=jax.ShapeDtypeStruct((b, l, d), F32),
        compiler_params=_params("parallel", "parallel"),
        name="mix_out",
    )(*acts, *consts, x, mod4, w)


def _ffn_kernel(final, x_ref, sh_ref, sc_ref, gate_ref, g_ref, w1_ref, w3_ref, w2_ref, fg_ref, o_ref):
    x = x_ref[0]
    h = _rms_mod(x, g_ref[...], sc_ref[...], sh_ref[...]).astype(BF16)
    a = _dot(h, w1_ref[...])
    u = (a * jax.nn.sigmoid(a)) * _dot(h, w3_ref[...])
    out = x + gate_ref[...] * _dot(u.astype(BF16), w2_ref[...])
    if final:
        ms = jnp.mean(out * out, axis=-1, keepdims=True)
        out = out * lax.rsqrt(ms + NORM_EPS) * fg_ref[...]
    o_ref[0] = out


def ffn(x, mod4, layer, gain, w1, w3, w2, final_gain, tm, final):
    b, l, d = x.shape
    return pl.pallas_call(
        functools.partial(_ffn_kernel, final),
        grid=(b, l // tm),
        in_specs=[pl.BlockSpec((1, tm, d), lambda bi, i: (bi, i, 0)),
                  _mod_spec(d, layer, 3), _mod_spec(d, layer, 4), _mod_spec(d, layer, 5),
                  _const_spec((1, d)), _const_spec(w1.shape), _const_spec(w3.shape), _const_spec(w2.shape),
                  _const_spec((1, d))],
        out_specs=pl.BlockSpec((1, tm, d), lambda bi, i: (bi, i, 0)),
        out_shape=jax.ShapeDtypeStruct((b, l, d), F32),
        compiler_params=_params("parallel", "parallel"),
        name="ffn",
    )(x, mod4, mod4, mod4, gain.reshape(1, d), w1, w3, w2, final_gain.reshape(1, d))


def _conv3(cur, prow, nrow, w_ref, b_ref):
    prev, nxt = _neighbours(cur, prow, nrow)
    return prev * w_ref[0:1, :] + cur * w_ref[1:2, :] + nxt * w_ref[2:3, :] + b_ref[...]


def _hy_filter_kernel(d_tiles, feats_ref, w1_ref, b1_ref, w2_ref, b2_ref, w3_ref, b3_ref, fr_ref,
                      wout_ref, win_ref, o_ref, hid_s):
    j = pl.program_id(0)

    @pl.when(j == 0)
    def _():
        fr = fr_ref[...]
        hid = jnp.sin(fr * (_dot_hi(feats_ref[...], w1_ref[...]) + b1_ref[...]))
        hid = jnp.sin(fr * (_dot_hi(hid, w2_ref[...]) + b2_ref[...]))
        hid_s[...] = jnp.sin(fr * (_dot_hi(hid, w3_ref[...]) + b3_ref[...]))

    filt = _dot_hi(hid_s[...], wout_ref[...]) * win_ref[...]
    backward = (j // d_tiles) % 2 == 1
    row = lax.broadcasted_iota(jnp.int32, filt.shape, 0)
    o_ref[...] = jnp.where(jnp.logical_and(backward, row == 0), 0.0, filt)


def hy_filter(feats, w1, b1, w2, b2, w3, b3, fr, w_out, window, tn):
    l = feats.shape[0]
    hf = w2.shape[0]
    n = w_out.shape[1]
    d = window.shape[1]
    d_tiles = d // tn
    consts = [feats, w1, b1.reshape(1, hf), w2, b2.reshape(1, hf), w3, b3.reshape(1, hf), fr.reshape(1, hf)]
    return pl.pallas_call(
        functools.partial(_hy_filter_kernel, d_tiles),
        grid=(n // tn,),
        in_specs=[_const_spec(a.shape) for a in consts]
        + [pl.BlockSpec((hf, tn), lambda j: (0, j)), pl.BlockSpec((l, tn), lambda j: (0, j % d_tiles))],
        out_specs=pl.BlockSpec((l, tn), lambda j: (0, j)),
        out_shape=jax.ShapeDtypeStruct((l, n), F32),
        scratch_shapes=[pltpu.VMEM((l, hf), F32)],
        compiler_params=_params("arbitrary"),
        name="hy_filter",
    )(*consts, w_out, window)


def _dft_matrices(l):
    n = 2 * l
    lane = min(128, l)
    k = jnp.arange(l, dtype=jnp.int32)[:, None]
    t = jnp.arange(l, dtype=jnp.int32)
    a = ((k * jnp.arange(lane, dtype=jnp.int32)[None, :]) % n).astype(F32) * (2.0 * math.pi / n)
    c = ((k * (lane * jnp.arange(l // lane, dtype=jnp.int32))[None, :]) % n).astype(F32) * (2.0 * math.pi / n)
    ca, sa, cc, sc = jnp.cos(a)[:, None, :], jnp.sin(a)[:, None, :], jnp.cos(c)[:, :, None], jnp.sin(c)[:, :, None]
    cos = (ca * cc - sa * sc).reshape(l, l)
    sin = (sa * cc + ca * sc).reshape(l, l)
    alt = jnp.where(t % 2 == 0, 1.0, -1.0).astype(F32)
    f_re = cos.astype(BF16)
    f_im = jnp.where(k == 0, alt[None, :], -sin).astype(BF16)
    col0 = (t == 0)[None, :]
    g_re = jnp.where(col0, 1.0 / n, (2.0 / n) * cos).astype(BF16)
    g_im = jnp.where(col0, alt[:, None] / n, (-2.0 / n) * sin).astype(BF16)
    return (f_re, f_im), (g_re, g_im)


def _nyquist_slot(i, shape):
    row = lax.broadcasted_iota(jnp.int32, shape, 0)
    return jnp.logical_and(i == 0, row == 0)


def _dft_spec_kernel(fre_ref, fim_ref, ft_ref, fb_ref, bt_ref, bb_ref, o_ref):
    fre, fim = fre_ref[...], fim_ref[...]
    ft, fb, bt, bb = (ref[...].astype(BF16) for ref in (ft_ref, fb_ref, bt_ref, bb_ref))
    af_r, af_i = _dot(fre, ft), _dot(fim, ft)
    bf_r, bf_i = _dot(fre, fb), _dot(fim, fb)
    ab_r, ab_i = _dot(fre, bt), _dot(fim, bt)
    bb_r, bb_i = _dot(fre, bb), _dot(fim, bb)
    nyq = _nyquist_slot(pl.program_id(1), af_r.shape)
    row = lax.broadcasted_iota(jnp.int32, af_r.shape, 0)
    sgn = jnp.where(row % 2 == 0, 1.0, -1.0)
    lag0 = ft_ref[0:1, :]
    o_ref[0, 0] = af_r + ab_r
    o_ref[0, 1] = jnp.where(nyq, af_i + ab_i, af_i - ab_i)
    o_ref[1, 0] = bf_r + sgn * (af_r - lag0)
    o_ref[1, 1] = bf_i + jnp.where(nyq, af_i - lag0, sgn * af_i)
    o_ref[2, 0] = sgn * ab_r + bb_r
    o_ref[2, 1] = jnp.where(nyq, ab_i + bb_i, -(sgn * ab_i + bb_i))


def dft_spec(fwd, taps, d, ft, tn):
    f_re, f_im = fwd
    m = f_re.shape[0]
    d_tiles = d // tn
    orders = taps.shape[1] // (2 * d)
    f_spec = pl.BlockSpec((ft, m), lambda j, i: (i, 0))

    def tap_spec(half, direction):
        return pl.BlockSpec((m, tn), lambda j, i: (half, (j // d_tiles) * 2 * d_tiles + direction * d_tiles
                                                    + j % d_tiles))

    return pl.pallas_call(
        _dft_spec_kernel,
        grid=(orders * d_tiles, m // ft),
        in_specs=[f_spec, f_spec, tap_spec(0, 0), tap_spec(1, 0), tap_spec(0, 1), tap_spec(1, 1)],
        out_specs=pl.BlockSpec((3, 2, ft, tn), lambda j, i: (0, 0, i, j)),
        out_shape=jax.ShapeDtypeStruct((3, 2, m, orders * d), F32),
        compiler_params=_params("parallel", "parallel"),
        name="dft_spec",
    )(f_re, f_im, taps, taps, taps, taps)


def _dft_mul_kernel(has_conv, fre_ref, fim_ref, u_ref, h_ref, *refs):
    o_ref, u_s = refs[-2:]
    i = pl.program_id(2)

    @pl.when(i == 0)
    def _():
        u = u_ref[0]
        if has_conv:
            u = _conv3(u, 0.0, 0.0, *refs[:2])
        u_s[...] = u.astype(BF16)

    m = u_s.shape[0] // 2
    fre, fim = fre_ref[...], fim_ref[...]
    tr, ti = _dot(fre, u_s[0:m]), _dot(fim, u_s[0:m])
    br, bi = _dot(fre, u_s[m:]), _dot(fim, u_s[m:])
    h0r, h0i, hpr, hpi, hmr, hmi = (h_ref[a, p] for a in range(3) for p in range(2))
    head = 2 * SUBLANES
    nyq = _nyquist_slot(i, (head, tr.shape[1]))

    def emit(half, plane, general, packed):
        o_ref[0, half, plane, 0:head] = jnp.where(nyq, packed, general[0:head]).astype(BF16)
        o_ref[0, half, plane, head:] = general[head:].astype(BF16)

    def top(x):
        return x[0:head]

    emit(0, 0, h0r * tr - h0i * ti + hmr * br - hmi * bi, top(h0r) * top(tr) + top(hmr) * top(br))
    emit(0, 1, h0r * ti + h0i * tr + hmr * bi + hmi * br, top(h0i) * top(ti) + top(hmi) * top(bi))
    emit(1, 0, hpr * tr - hpi * ti + h0r * br - h0i * bi, top(hpr) * top(tr) + top(h0r) * top(br))
    emit(1, 1, hpr * ti + hpi * tr + h0r * bi + h0i * br, top(hpi) * top(ti) + top(h0i) * top(bi))


def dft_mul(fwd, u, u_col, spec, spec_col, d, ft, tn, conv=None):
    f_re, f_im = fwd
    m = f_re.shape[0]
    b = u.shape[0]
    d_tiles = d // tn
    f_spec = pl.BlockSpec((ft, m), lambda bi, j, i: (i, 0))
    conv_specs = [] if conv is None else [pl.BlockSpec((3, tn), lambda bi, j, i: (0, u_col * d_tiles + j)),
                                          pl.BlockSpec((1, tn), lambda bi, j, i: (0, u_col * d_tiles + j))]
    return pl.pallas_call(
        functools.partial(_dft_mul_kernel, conv is not None),
        grid=(b, d_tiles, m // ft),
        in_specs=[f_spec, f_spec,
                  pl.BlockSpec((1, 2 * m, tn), lambda bi, j, i: (bi, 0, u_col * d_tiles + j)),
                  pl.BlockSpec((3, 2, ft, tn), lambda bi, j, i: (0, 0, i, spec_col * d_tiles + j))] + conv_specs,
        out_specs=pl.BlockSpec((1, 2, 2, ft, tn), lambda bi, j, i: (bi, 0, 0, i, j)),
        out_shape=jax.ShapeDtypeStruct((b, 2, 2, m, d), BF16),
        scratch_shapes=[pltpu.VMEM((2 * m, tn), BF16)],
        compiler_params=_params("parallel", "parallel", "arbitrary"),
        name="dft_mul",
    )(f_re, f_im, u, spec, *(conv or ()))


def _dft_inv_kernel(conv_u, gre_ref, gim_ref, y_ref, *refs):
    u_refs, (g_ref, gp_ref, gn_ref, gw_ref, gb_ref, skip_ref, o_ref) = refs[:-7], refs[-7:]
    u = u_refs[0][0]
    if conv_u:
        u = _conv3(u, *_halo_rows(u_refs[1], u_refs[2], 2), u_refs[3], u_refs[4])
    gate = _conv3(g_ref[0], *_halo_rows(gp_ref, gn_ref, 2), gw_ref, gb_ref)
    conv = _dot(gre_ref[...], y_ref[0, 0, 0]) + _dot(gim_ref[...], y_ref[0, 0, 1])
    o_ref[0] = gate * (conv + u * skip_ref[...])


def dft_inv(inv, y, u, u_col, gate, gate_col, skip, tm, tn, conv_w, conv_b, conv_u):
    g_re, g_im = inv
    m = g_re.shape[0]
    b, _, _, _, d = y.shape
    d_tiles = d // tn
    per_half = m // tm
    rows8 = tm // SUBLANES
    last8 = 2 * m // SUBLANES - 1
    g_spec = pl.BlockSpec((tm, m), lambda bi, j, i: (i % per_half, 0))

    def tile_specs(col, halo):
        specs = [pl.BlockSpec((1, tm, tn), lambda bi, j, i: (bi, i, col * d_tiles + j))]
        if halo:
            specs += [pl.BlockSpec((1, SUBLANES, tn),
                                   lambda bi, j, i: (bi, jnp.maximum(i * rows8 - 1, 0), col * d_tiles + j)),
                      pl.BlockSpec((1, SUBLANES, tn),
                                   lambda bi, j, i: (bi, jnp.minimum((i + 1) * rows8, last8), col * d_tiles + j)),
                      pl.BlockSpec((3, tn), lambda bi, j, i: (0, col * d_tiles + j)),
                      pl.BlockSpec((1, tn), lambda bi, j, i: (0, col * d_tiles + j))]
        return specs

    u_ops = [u, u, u, conv_w, conv_b] if conv_u else [u]
    return pl.pallas_call(
        functools.partial(_dft_inv_kernel, conv_u),
        grid=(b, d_tiles, 2 * per_half),
        in_specs=[g_spec, g_spec,
                  pl.BlockSpec((1, 1, 2, m, tn), lambda bi, j, i: (bi, i // per_half, 0, 0, j))]
        + tile_specs(u_col, conv_u) + tile_specs(gate_col, True)
        + [pl.BlockSpec((1, tn), lambda bi, j, i: (0, j))],
        out_specs=pl.BlockSpec((1, tm, tn), lambda bi, j, i: (bi, i, j)),
        out_shape=jax.ShapeDtypeStruct((b, 2 * m, d), F32),
        compiler_params=_params("parallel", "parallel", "parallel"),
        name="dft_inv",
    )(g_re, g_im, y, *u_ops, gate, gate, gate, conv_w, conv_b, skip.reshape(1, d))


def _rope_tables(l):
    rows = l // GRID_W
    row = jnp.repeat(jnp.arange(rows), GRID_W).astype(F32)
    col = jnp.tile(jnp.arange(GRID_W), rows).astype(F32)
    half = HEAD_DIM // 2
    inv_freq = ROPE_THETA ** (-jnp.arange(0, half, 2, dtype=F32) / half)
    ang = jnp.concatenate([row[:, None] * inv_freq, col[:, None] * inv_freq], axis=-1)
    cos, sin = jnp.cos(ang), jnp.sin(ang)
    return jnp.concatenate([cos, cos], axis=-1), jnp.concatenate([sin, sin], axis=-1)


def _hyena_tables(l, d):
    t = jnp.linspace(0.0, 1.0, l, dtype=F32)[:, None]
    omega = (2.0 * math.pi / l) * jnp.arange(l, dtype=F32)[:, None]
    bands = jnp.linspace(1e-4, HYENA_BANDS - 1, HYENA_BANDS, dtype=F32)[None, :]
    feats = jnp.concatenate([t, jnp.cos(bands * omega), -jnp.sin(bands * omega)], axis=-1)
    min_decay = math.log(HYENA_TARGET) / HYENA_SLOW_DECAY
    max_decay = math.log(HYENA_TARGET) / HYENA_FAST_DECAY
    deltas = jnp.abs(jnp.linspace(min_decay, max_decay, d, dtype=F32))
    window = jnp.exp(-t * deltas)
    return feats, window


def _from_scan_layout(y):
    return y.transpose(0, 2, 1).reshape(y.shape[0], -1)


def kernel(x, c, mix_w_in, mix_w_out, attn_q_norm, attn_k_norm, rwkv_mu, rwkv_w0, rwkv_w_up, rwkv_a0, rwkv_a_up, rwkv_g_up, rwkv_k_k, rwkv_k_a, rwkv_r_k, rwkv_ln_g, rwkv_ln_b, hy_w_in, hy_conv_w, hy_conv_b, hy_f_w1, hy_f_b1, hy_f_w2, hy_f_b2, hy_f_w3, hy_f_b3, hy_sin_freq, hy_f_out, hy_skip, hy_w_out, ada_w, ada_b, norm_mix, norm_ffn, ffn_w1, ffn_w3, ffn_w2, final_norm):
    b, l, d = x.shape
    depth = ada_w.shape[0]
    tm = min(256, l)
    tp = min(FFN_ROWS, l)
    c_w = RWKV_WIDTH

    mod4 = ada_mod(c, ada_w, ada_b).reshape(depth, b, 1, 6 * d)

    perm = jnp.concatenate([jnp.arange(0, HEAD_DIM, 2), jnp.arange(1, HEAD_DIM, 2)])
    w_in = mix_w_in[0]
    w_q = w_in[:, :ATTN_WIDTH].reshape(d, ATTN_HEADS, HEAD_DIM)[:, :, perm].reshape(d, ATTN_WIDTH)
    w_kv = w_in[:, ATTN_WIDTH:ATTN_WIDTH + 2 * KV_WIDTH]
    w_k = w_kv[:, :KV_WIDTH].reshape(d, ATTN_KV_HEADS, HEAD_DIM)[:, :, perm].reshape(d, KV_WIDTH)
    w_kv = jnp.concatenate([w_k, w_kv[:, KV_WIDTH:]], axis=1)
    w_rw = jnp.pad(w_in[:, ATTN_WIDTH + 2 * KV_WIDTH:], ((0, 0), (0, RWKV_IN_PAD - RWKV_IN)))
    q, kv, rw = proj(x, mod4, 0, norm_mix[0], [w_q.astype(BF16), w_kv.astype(BF16), w_rw.astype(BF16)], tp)

    seg = jnp.kron(jnp.eye(RWKV_HEADS, dtype=BF16), jnp.ones((RWKV_HEAD, RWKV_HEAD), BF16))
    r, z, v, w, kd, bb, bonus, g = rwkv_prep(
        rw, jnp.pad(rwkv_mu[0], (0, RWKV_IN_PAD - RWKV_IN)).reshape(1, RWKV_IN_PAD),
        rwkv_w0[0].reshape(2, 1, c_w), rwkv_w_up[0], rwkv_a0[0].reshape(2, 1, c_w), rwkv_a_up[0],
        jnp.pad(rwkv_g_up[0], ((0, RWKV_IN_PAD - RWKV_IN), (0, 0))),
        rwkv_k_k[0].reshape(1, c_w), rwkv_k_a[0].reshape(1, c_w), rwkv_r_k[0].reshape(1, c_w), seg, tm)
    stage1 = [a.T for a in (r, w, kd, v, z, bb)]
    cos, sin = _rope_tables(l)
    qn, kn = attn_q_norm[0][perm].reshape(1, HEAD_DIM), attn_k_norm[0][perm].reshape(1, HEAD_DIM)
    half = b // 2
    y_attn0 = attn(q, kv, cos, sin, qn, kn, tm, 0, half)
    tn = min(512, d)
    feats, window = _hyena_tables(l, d)
    k_pad = 128
    feats = jnp.pad(feats, ((0, 0), (0, k_pad - feats.shape[1])))
    f_w1 = jnp.pad(hy_f_w1[0], ((0, k_pad - hy_f_w1.shape[1]), (0, 0)))
    taps = hy_filter(feats, f_w1, hy_f_b1[0], hy_f_w2[0], hy_f_b2[0], hy_f_w3[0], hy_f_b3[0],
                     hy_sin_freq[0], hy_f_out[0], window, tn)
    stage1, y_attn0, taps = lax.optimization_barrier((stage1, y_attn0, taps))
    scan_operands = [a.reshape(-1, RWKV_HEAD, l).transpose(2, 1, 0) for a in stage1]
    y_attn1 = attn(q, kv, cos, sin, qn, kn, tm, half, b - half)
    fwd, inv = _dft_matrices(l // 2)
    ft = min(FREQ_TILE, l // 2)
    spec = dft_spec(fwd, taps, d, ft, tn)

    y_f, y_b = rwkv_scan(scan_operands, min(SCAN_BLOCK, l), after=(y_attn1, spec))
    x = mix_out([y_attn0, y_attn1, (_from_scan_layout(y_f), 0), (_from_scan_layout(y_b), 1), bonus, g],
                [rwkv_ln_g[0].reshape(1, c_w), rwkv_ln_b[0].reshape(1, c_w), seg],
                x, mod4, 0, mix_w_out[0].astype(BF16), tm, True)
    x = ffn(x, mod4, 0, norm_ffn[0], ffn_w1[0].astype(BF16), ffn_w3[0].astype(BF16), ffn_w2[0].astype(BF16),
            final_norm, min(FFN_ROWS, l), False)

    (p3,) = proj(x, mod4, 1, norm_mix[1], [hy_w_in[0].astype(BF16)], tp)
    conv_w, conv_b = hy_conv_w[0], hy_conv_b[0].reshape(1, 3 * d)
    y1 = dft_mul(fwd, p3, 0, spec, 0, d, ft, tn, conv=(conv_w, conv_b))
    z1 = dft_inv(inv, y1, p3, 0, p3, 1, hy_skip[0, 0], ft, tn, conv_w, conv_b, True)
    y2 = dft_mul(fwd, z1, 0, spec, 1, d, ft, tn)
    z2 = dft_inv(inv, y2, z1, 0, p3, 2, hy_skip[0, 1], ft, tn, conv_w, conv_b, False)
    x = mix_out([z2], [], x, mod4, 1, hy_w_out[0].astype(BF16), tp, False)
    x = ffn(x, mod4, 1, norm_ffn[1], ffn_w1[1].astype(BF16), ffn_w3[1].astype(BF16), ffn_w2[1].astype(BF16),
            final_norm, min(FFN_ROWS, l), True)
    return x
```

```python
import functools
import math

import jax
import jax.numpy as jnp
from jax import lax
from jax.experimental import pallas as pl
from jax.experimental.pallas import tpu as pltpu

F32 = jnp.float32
BF16 = jnp.bfloat16
HIGHEST = lax.Precision.HIGHEST

GRID_W = 64
HEAD_DIM = 64
ATTN_HEADS = 8
ATTN_KV_HEADS = 2
ATTN_WIDTH = ATTN_HEADS * HEAD_DIM
KV_WIDTH = ATTN_KV_HEADS * HEAD_DIM
ROPE_THETA = 10000.0
RWKV_HEADS = 8
RWKV_HEAD = 64
RWKV_WIDTH = RWKV_HEADS * RWKV_HEAD
DECAY_LORA = 64
AAA_LORA = 64
GATE_LORA = 160
RWKV_IN = 3 * RWKV_WIDTH + 2 * DECAY_LORA + 2 * AAA_LORA + GATE_LORA
RWKV_IN_PAD = 2048
GN_EPS = 64e-5
HYENA_BANDS = 16
HYENA_TARGET = 1e-2
HYENA_FAST_DECAY = 0.3
HYENA_SLOW_DECAY = 1.5
NORM_EPS = 1e-6

V7X_VMEM_BYTES = 64 * 1024 * 1024
VMEM_LIMIT = V7X_VMEM_BYTES - 8 * 1024 * 1024
FREQ_TILE = 512
SCAN_BLOCK = 32
FFN_ROWS = 512
SUBLANES = 8


def _params(*sem):
    return pltpu.CompilerParams(dimension_semantics=sem, vmem_limit_bytes=VMEM_LIMIT)


def _const_spec(shape):
    zeros = (0,) * len(shape)
    return pl.BlockSpec(shape, lambda *_: zeros, pipeline_mode=pl.Buffered(1))


def _dot(a, b):
    return jnp.dot(a, b, preferred_element_type=F32)


def _split(x):
    hi = x.astype(BF16)
    return hi, (x - hi.astype(F32)).astype(BF16)


def _dot_split(a, b):
    ah, al = _split(a)
    bh, bl = _split(b)
    return _dot(ah, bh) + (_dot(ah, bl) + _dot(al, bh))


def _seg_sum(a, seg):
    ah, al = _split(a)
    return _dot(ah, seg) + _dot(al, seg)


def _dot_hi(a, b):
    return jnp.dot(a, b, precision=HIGHEST, preferred_element_type=F32)


def _rms_mod(x, gain, scale, shift):
    ms = jnp.mean(x * x, axis=-1, keepdims=True)
    return (x * lax.rsqrt(ms + NORM_EPS) * gain) * (1.0 + scale) + shift


def _ada_kernel(c_ref, w_ref, b_ref, o_ref):
    c = c_ref[...]
    cond = c * jax.nn.sigmoid(c)
    o_ref[...] = _dot_hi(cond, w_ref[...]) + b_ref[...]


def ada_mod(c, ada_w, ada_b):
    depth, d, n = ada_w.shape
    b = c.shape[0]
    tn = 1536
    return pl.pallas_call(
        _ada_kernel,
        grid=(depth, n // tn),
        in_specs=[pl.BlockSpec((b, d), lambda l, j: (0, 0)),
                  pl.BlockSpec((None, d, tn), lambda l, j: (l, 0, j)),
                  pl.BlockSpec((None, 1, tn), lambda l, j: (l, 0, j))],
        out_specs=pl.BlockSpec((None, b, tn), lambda l, j: (l, 0, j)),
        out_shape=jax.ShapeDtypeStruct((depth, b, n), F32),
        compiler_params=_params("parallel", "parallel"),
        name="ada_mod",
    )(c, ada_w, ada_b.reshape(depth, 1, n))


def _mod_spec(d, layer, chunk):
    return pl.BlockSpec((None, None, 1, d), lambda b, *_: (layer, b, 0, chunk))


def _proj_kernel(n_out, x_ref, sh_ref, sc_ref, g_ref, *refs):
    h = _rms_mod(x_ref[0], g_ref[...], sc_ref[...], sh_ref[...]).astype(BF16)
    for w_ref, o_ref in zip(refs[:n_out], refs[n_out:]):
        o_ref[0] = _dot(h, w_ref[...])


def proj(x, mod4, layer, gain, weights, tm):
    b, l, d = x.shape
    n_out = len(weights)
    return pl.pallas_call(
        functools.partial(_proj_kernel, n_out),
        grid=(b, l // tm),
        in_specs=[pl.BlockSpec((1, tm, d), lambda bi, i: (bi, i, 0)),
                  _mod_spec(d, layer, 0), _mod_spec(d, layer, 1),
                  _const_spec((1, d))] + [_const_spec(w.shape) for w in weights],
        out_specs=[pl.BlockSpec((1, tm, w.shape[1]), lambda bi, i: (bi, i, 0)) for w in weights],
        out_shape=[jax.ShapeDtypeStruct((b, l, w.shape[1]), F32) for w in weights],
        compiler_params=_params("parallel", "parallel"),
        name="proj",
    )(x, mod4, mod4, gain.reshape(1, d), *weights)


def _headnorm_rope(x, gain, cos, sin):
    ms = jnp.mean(x * x, axis=-1, keepdims=True)
    y = x * lax.rsqrt(ms + NORM_EPS) * gain
    half = HEAD_DIM // 2
    rot = jnp.concatenate([-y[:, half:], y[:, :half]], axis=-1)
    return y * cos + rot * sin


def _attn_kernel(q_ref, kv_ref, cq_ref, sq_ref, ck_ref, sk_ref, qn_ref, kn_ref, *refs):
    o_ref, kt_s, v_s = refs[-3:]
    @pl.when(pl.program_id(1) == 0)
    def _():
        kv = kv_ref[0]
        ones_col = (lax.broadcasted_iota(jnp.int32, (kv.shape[0], HEAD_DIM), 1) == 0).astype(F32)
        for hk in range(ATTN_KV_HEADS):
            k = kv[:, HEAD_DIM * hk:HEAD_DIM * (hk + 1)]
            kt_s[hk] = _headnorm_rope(k, kn_ref[...], ck_ref[...], sk_ref[...]).T.astype(BF16)
            v = kv[:, KV_WIDTH + HEAD_DIM * hk:KV_WIDTH + HEAD_DIM * (hk + 1)]
            v_s[hk] = jnp.concatenate([v, ones_col], axis=-1).astype(BF16)

    q = q_ref[0]
    group = ATTN_HEADS // ATTN_KV_HEADS
    outs = []
    for h in range(ATTN_HEADS):
        qh = _headnorm_rope(q[:, HEAD_DIM * h:HEAD_DIM * (h + 1)], qn_ref[...], cq_ref[...], sq_ref[...])
        qh = (qh * (HEAD_DIM ** -0.5 * math.log2(math.e))).astype(BF16)
        s = _dot(qh, kt_s[h // group])
        p = jnp.exp2(s - jnp.max(s, axis=-1, keepdims=True)).astype(BF16)
        o = _dot(p, v_s[h // group])
        outs.append(o[:, :HEAD_DIM] / o[:, HEAD_DIM:HEAD_DIM + 1])
    o_ref[0] = jnp.concatenate(outs, axis=-1)


def _anchor_specs(after):
    return [pl.BlockSpec((1,) * (a.ndim - 2) + (SUBLANES, 128), lambda *_, nd=a.ndim: (0,) * nd) for a in after]


def attn(q, kv, cos, sin, qn, kn, tq, b0, nb, after=()):
    _, l, _ = q.shape
    b = nb
    return pl.pallas_call(
        _attn_kernel,
        grid=(nb, l // tq),
        in_specs=[pl.BlockSpec((1, tq, ATTN_WIDTH), lambda bi, i: (b0 + bi, i, 0)),
                  pl.BlockSpec((1, l, 2 * KV_WIDTH), lambda bi, i: (b0 + bi, 0, 0)),
                  pl.BlockSpec((tq, HEAD_DIM), lambda bi, i: (i, 0)),
                  pl.BlockSpec((tq, HEAD_DIM), lambda bi, i: (i, 0)),
                  _const_spec((l, HEAD_DIM)), _const_spec((l, HEAD_DIM)),
                  _const_spec((1, HEAD_DIM)), _const_spec((1, HEAD_DIM))] + _anchor_specs(after),
        out_specs=pl.BlockSpec((1, tq, ATTN_WIDTH), lambda bi, i: (bi, i, 0)),
        out_shape=jax.ShapeDtypeStruct((b, l, ATTN_WIDTH), F32),
        scratch_shapes=[pltpu.VMEM((ATTN_KV_HEADS, HEAD_DIM, l), BF16),
                        pltpu.VMEM((ATTN_KV_HEADS, l, 2 * HEAD_DIM), BF16)],
        compiler_params=_params("parallel", "arbitrary"),
        name="attn",
    )(q, kv, cos, sin, cos, sin, qn, kn, *after)


def _halo_specs(tt, w, l):
    nb = l // SUBLANES
    per = tt // SUBLANES
    main = pl.BlockSpec((1, tt, w), lambda bi, i: (bi, i, 0))
    prev = pl.BlockSpec((1, SUBLANES, w), lambda bi, i: (bi, jnp.maximum(i * per - 1, 0), 0))
    nxt = pl.BlockSpec((1, SUBLANES, w), lambda bi, i: (bi, jnp.minimum((i + 1) * per, nb - 1), 0))
    return [main, prev, nxt]


def _neighbours(cur, prow, nrow):
    tt = cur.shape[0]
    row = lax.broadcasted_iota(jnp.int32, cur.shape, 0)
    prev = jnp.where(row == 0, prow, pltpu.roll(cur, 1, 0))
    nxt = jnp.where(row == tt - 1, nrow, pltpu.roll(cur, tt - 1, 0))
    return prev, nxt


def _halo_rows(xp_ref, xn_ref, axis):
    i = pl.program_id(axis)
    last = pl.num_programs(axis) - 1
    prow = jnp.where(i > 0, xp_ref[0, SUBLANES - 1:SUBLANES, :], 0.0)
    nrow = jnp.where(i < last, xn_ref[0, 0:1, :], 0.0)
    return prow, nrow


def _rwkv_prep_kernel(x_ref, xp_ref, xn_ref, mu_ref, w0_ref, wup_ref, a0_ref, aup_ref, gup_ref,
                      kk_ref, ka_ref, rk_ref, seg_ref,
                      r_o, z_o, v_o, w_o, kd_o, b_o, bonus_o, g_o):
    cur = x_ref[0]
    prev, nxt = _neighbours(cur, *_halo_rows(xp_ref, xn_ref, 1))
    ps = cur + (0.5 * (prev + nxt) - cur) * mu_ref[...]
    c = RWKV_WIDTH
    r = ps[:, 0:c]
    k = ps[:, c:2 * c]
    v = ps[:, 2 * c:3 * c]
    o_a = 3 * c + 2 * DECAY_LORA
    o_g = o_a + 2 * AAA_LORA
    kk = k * kk_ref[...]
    kk = kk * lax.rsqrt(jnp.maximum(_seg_sum(kk * kk, seg_ref[...]), 1e-24))
    for d in range(2):
        cols = slice(c * d, c * (d + 1))
        r_o[:, cols] = r
        v_o[:, cols] = v
        z_o[:, cols] = -kk
        w_lo = ps[:, 3 * c + DECAY_LORA * d:3 * c + DECAY_LORA * (d + 1)]
        a_lo = ps[:, o_a + AAA_LORA * d:o_a + AAA_LORA * (d + 1)]
        decay = jnp.exp(-math.exp(-0.5) * jax.nn.sigmoid(w0_ref[d] + _dot_split(jnp.tanh(w_lo), wup_ref[d])))
        a = jax.nn.sigmoid(a0_ref[d] + _dot_split(a_lo, aup_ref[d]))
        w_o[:, cols] = decay
        kd_o[:, cols] = k * (1.0 + (a - 1.0) * ka_ref[...])
        b_o[:, cols] = kk * a
    bonus_o[0] = _seg_sum(r * k * rk_ref[...], seg_ref[...]) * v
    g_o[0] = _dot_split(jax.nn.sigmoid(ps[:, o_g:RWKV_IN_PAD]), gup_ref[...])


def rwkv_prep(rw, mu, w0, w_up, a0, a_up, g_up, k_k, k_a, r_k, seg, tt):
    b, l, w = rw.shape
    c = RWKV_WIDTH
    natural = jax.ShapeDtypeStruct((b, l, c), F32)
    time_major = jax.ShapeDtypeStruct((l, b * 2 * c), F32)
    spec_n = pl.BlockSpec((1, tt, c), lambda bi, i: (bi, i, 0))
    spec_t = pl.BlockSpec((tt, 2 * c), lambda bi, i: (i, bi))
    consts = [mu, w0, w_up, a0, a_up, g_up, k_k, k_a, r_k, seg]
    return pl.pallas_call(
        _rwkv_prep_kernel,
        grid=(b, l // tt),
        in_specs=_halo_specs(tt, w, l) + [_const_spec(a.shape) for a in consts],
        out_specs=[spec_t] * 6 + [spec_n] * 2,
        out_shape=[time_major] * 6 + [natural] * 2,
        compiler_params=_params("parallel", "parallel"),
        name="rwkv_prep",
    )(rw, rw, rw, *consts)


SCAN_ROWS = 32
_R, _W, _KD, _V, _Z, _B = range(6)


def _rwkv_scan_kernel(*refs):
    ins, (yf_ref, yb_ref, s_ref, m_ref, sz_ref) = refs[:12], refs[-5:]
    tb, n, lanes = ins[0].shape
    fwd_lane = (lax.broadcasted_iota(jnp.int32, (n, lanes), 1) // RWKV_HEADS) % 2 == 0

    def stage(which, t):
        m_ref[which] = jnp.where(fwd_lane, ins[2 * which][t], ins[2 * which + 1][tb - 1 - t])

    @pl.when(pl.program_id(0) == 0)
    def _():
        s_ref[...] = jnp.zeros_like(s_ref)

    halves = [slice(h * SCAN_ROWS, (h + 1) * SCAN_ROWS) for h in range(n // SCAN_ROWS)]

    stage(_Z, 0)
    for rows in halves:
        acc = s_ref[0, rows, :] * m_ref[_Z, 0:1, :]
        for k in range(1, n):
            acc = acc + s_ref[k, rows, :] * m_ref[_Z, k:k + 1, :]
        sz_ref[rows, :] = acc

    def step(t, carry):
        for which in (_R, _W, _KD, _V, _B):
            stage(which, t)
        stage(_Z, jnp.minimum(t + 1, tb - 1))
        for rows in halves:
            sz = sz_ref[rows, :]
            vt = m_ref[_V, rows, :]
            y = None
            sz_next = None
            for k in range(n):
                sk = (s_ref[k, rows, :] * m_ref[_W, k:k + 1, :] + sz * m_ref[_B, k:k + 1, :]
                      + vt * m_ref[_KD, k:k + 1, :])
                s_ref[k, rows, :] = sk
                yk = sk * m_ref[_R, k:k + 1, :]
                zk = sk * m_ref[_Z, k:k + 1, :]
                y = yk if y is None else y + yk
                sz_next = zk if sz_next is None else sz_next + zk
            sz_ref[rows, :] = sz_next
            yf_ref[t, rows, :] = y
            yb_ref[tb - 1 - t, rows, :] = y
        return carry

    lax.fori_loop(0, tb, step, 0)


def rwkv_scan(arrays, tb, after=()):
    l, n, lanes = arrays[0].shape
    nt = l // tb
    fwd = pl.BlockSpec((tb, n, lanes), lambda i: (i, 0, 0))
    rev = pl.BlockSpec((tb, n, lanes), lambda i: (nt - 1 - i, 0, 0))
    out = jax.ShapeDtypeStruct((l, n, lanes), F32)
    return pl.pallas_call(
        _rwkv_scan_kernel,
        grid=(nt,),
        in_specs=[fwd, rev] * 6 + _anchor_specs(after),
        out_specs=[fwd, rev],
        out_shape=[out, out],
        scratch_shapes=[pltpu.VMEM((n, n, lanes), F32), pltpu.VMEM((6, n, lanes), F32),
                        pltpu.VMEM((n, lanes), F32)],
        compiler_params=_params("arbitrary"),
        name="rwkv_scan",
    )(*[a for a in arrays for _ in range(2)], *after)


def _mix_out_kernel(rwkv, *refs):
    if rwkv:
        (ya0_ref, ya1_ref, yf_ref, yb_ref, bonus_ref, g_ref, lng_ref, lnb_ref, seg_ref,
         x_ref, gate_ref, w_ref, o_ref) = refs
        first_half = pl.program_id(0) < pl.num_programs(0) // 2
        y_attn = jnp.where(first_half, ya0_ref[0], ya1_ref[0])
        y = yf_ref[...] + yb_ref[...]
        inv_n = 1.0 / RWKV_HEAD
        mean = _seg_sum(y, seg_ref[...]) * inv_n
        cen = y - mean
        var = _seg_sum(cen * cen, seg_ref[...]) * inv_n
        yn = cen * lax.rsqrt(var + GN_EPS) * lng_ref[...] + lnb_ref[...]
        yr = (yn + bonus_ref[0]) * g_ref[0]
        a = jnp.concatenate([y_attn, yr], axis=-1)
    else:
        a_ref, x_ref, gate_ref, w_ref, o_ref = refs
        a = a_ref[0]
    o_ref[0] = x_ref[0] + gate_ref[...] * _dot(a.astype(BF16), w_ref[...])


def mix_out(acts, consts, x, mod4, layer, w, tm, rwkv):
    b, l, d = x.shape
    in_specs = []
    for i_act, a in enumerate(acts):
        if not isinstance(a, tuple) and a.shape[0] == b // 2:
            in_specs.append(pl.BlockSpec((1, tm, a.shape[2]), lambda bi, i: (bi % (b // 2), i, 0)))
        elif isinstance(a, tuple):
            arr, direction = a
            width = arr.shape[1] // (2 * b)
            in_specs.append(pl.BlockSpec((tm, width), lambda bi, i, dr=direction: (i, 2 * bi + dr)))
            acts[i_act] = arr
        else:
            in_specs.append(pl.BlockSpec((1, tm, a.shape[2]), lambda bi, i: (bi, i, 0)))
    in_specs += [_const_spec(a.shape) for a in consts]
    in_specs += [pl.BlockSpec((1, tm, d), lambda bi, i: (bi, i, 0)), _mod_spec(d, layer, 2), _const_spec(w.shape)]
    return pl.pallas_call(
        functools.partial(_mix_out_kernel, rwkv),
        grid=(b, l // tm),
        in_specs=in_specs,
        out_specs=pl.BlockSpec((1, tm, d), lambda bi, i: (bi, i, 0)),
        out_shape=jax.ShapeDtypeStruct((b, l, d), F32),
        compiler_params=_params("parallel", "parallel"),
        name="mix_out",
    )(*acts, *consts, x, mod4, w)


def _ffn_kernel(final, x_ref, sh_ref, sc_ref, gate_ref, g_ref, w1_ref, w3_ref, w2_ref, fg_ref, o_ref):
    x = x_ref[0]
    h = _rms_mod(x, g_ref[...], sc_ref[...], sh_ref[...]).astype(BF16)
    a = _dot(h, w1_ref[...])
    u = (a * jax.nn.sigmoid(a)) * _dot(h, w3_ref[...])
    out = x + gate_ref[...] * _dot(u.astype(BF16), w2_ref[...])
    if final:
        ms = jnp.mean(out * out, axis=-1, keepdims=True)
        out = out * lax.rsqrt(ms + NORM_EPS) * fg_ref[...]
    o_ref[0] = out


def ffn(x, mod4, layer, gain, w1, w3, w2, final_gain, tm, final):
    b, l, d = x.shape
    return pl.pallas_call(
        functools.partial(_ffn_kernel, final),
        grid=(b, l // tm),
        in_specs=[pl.BlockSpec((1, tm, d), lambda bi, i: (bi, i, 0)),
                  _mod_spec(d, layer, 3), _mod_spec(d, layer, 4), _mod_spec(d, layer, 5),
                  _const_spec((1, d)), _const_spec(w1.shape), _const_spec(w3.shape), _const_spec(w2.shape),
                  _const_spec((1, d))],
        out_specs=pl.BlockSpec((1, tm, d), lambda bi, i: (bi, i, 0)),
        out_shape=jax.ShapeDtypeStruct((b, l, d), F32),
        compiler_params=_params("parallel", "parallel"),
        name="ffn",
    )(x, mod4, mod4, mod4, gain.reshape(1, d), w1, w3, w2, final_gain.reshape(1, d))


def _conv3(cur, prow, nrow, w_ref, b_ref):
    prev, nxt = _neighbours(cur, prow, nrow)
    return prev * w_ref[0:1, :] + cur * w_ref[1:2, :] + nxt * w_ref[2:3, :] + b_ref[...]


def _hy_filter_kernel(d_tiles, feats_ref, w1_ref, b1_ref, w2_ref, b2_ref, w3_ref, b3_ref, fr_ref,
                      wout_ref, win_ref, o_ref, hid_s):
    j = pl.program_id(0)

    @pl.when(j == 0)
    def _():
        fr = fr_ref[...]
        hid = jnp.sin(fr * (_dot_hi(feats_ref[...], w1_ref[...]) + b1_ref[...]))
        hid = jnp.sin(fr * (_dot_hi(hid, w2_ref[...]) + b2_ref[...]))
        hid_s[...] = jnp.sin(fr * (_dot_hi(hid, w3_ref[...]) + b3_ref[...]))

    filt = _dot_hi(hid_s[...], wout_ref[...]) * win_ref[...]
    backward = (j // d_tiles) % 2 == 1
    row = lax.broadcasted_iota(jnp.int32, filt.shape, 0)
    o_ref[...] = jnp.where(jnp.logical_and(backward, row == 0), 0.0, filt)


def hy_filter(feats, w1, b1, w2, b2, w3, b3, fr, w_out, window, tn):
    l = feats.shape[0]
    hf = w2.shape[0]
    n = w_out.shape[1]
    d = window.shape[1]
    d_tiles = d // tn
    consts = [feats, w1, b1.reshape(1, hf), w2, b2.reshape(1, hf), w3, b3.reshape(1, hf), fr.reshape(1, hf)]
    return pl.pallas_call(
        functools.partial(_hy_filter_kernel, d_tiles),
        grid=(n // tn,),
        in_specs=[_const_spec(a.shape) for a in consts]
        + [pl.BlockSpec((hf, tn), lambda j: (0, j)), pl.BlockSpec((l, tn), lambda j: (0, j % d_tiles))],
        out_specs=pl.BlockSpec((l, tn), lambda j: (0, j)),
        out_shape=jax.ShapeDtypeStruct((l, n), F32),
        scratch_shapes=[pltpu.VMEM((l, hf), F32)],
        compiler_params=_params("arbitrary"),
        name="hy_filter",
    )(*consts, w_out, window)


def _dft_matrices(l):
    n = 2 * l
    lane = min(128, l)
    k = jnp.arange(l, dtype=jnp.int32)[:, None]
    t = jnp.arange(l, dtype=jnp.int32)
    a = ((k * jnp.arange(lane, dtype=jnp.int32)[None, :]) % n).astype(F32) * (2.0 * math.pi / n)
    c = ((k * (lane * jnp.arange(l // lane, dtype=jnp.int32))[None, :]) % n).astype(F32) * (2.0 * math.pi / n)
    ca, sa, cc, sc = jnp.cos(a)[:, None, :], jnp.sin(a)[:, None, :], jnp.cos(c)[:, :, None], jnp.sin(c)[:, :, None]
    cos = (ca * cc - sa * sc).reshape(l, l)
    sin = (sa * cc + ca * sc).reshape(l, l)
    alt = jnp.where(t % 2 == 0, 1.0, -1.0).astype(F32)
    f_re = cos.astype(BF16)
    f_im = jnp.where(k == 0, alt[None, :], -sin).astype(BF16)
    col0 = (t == 0)[None, :]
    g_re = jnp.where(col0, 1.0 / n, (2.0 / n) * cos).astype(BF16)
    g_im = jnp.where(col0, alt[:, None] / n, (-2.0 / n) * sin).astype(BF16)
    return (f_re, f_im), (g_re, g_im)


def _nyquist_slot(i, shape):
    row = lax.broadcasted_iota(jnp.int32, shape, 0)
    return jnp.logical_and(i == 0, row == 0)


def _dft_spec_kernel(fre_ref, fim_ref, ft_ref, fb_ref, bt_ref, bb_ref, o_ref):
    fre, fim = fre_ref[...], fim_ref[...]
    ft, fb, bt, bb = (ref[...].astype(BF16) for ref in (ft_ref, fb_ref, bt_ref, bb_ref))
    af_r, af_i = _dot(fre, ft), _dot(fim, ft)
    bf_r, bf_i = _dot(fre, fb), _dot(fim, fb)
    ab_r, ab_i = _dot(fre, bt), _dot(fim, bt)
    bb_r, bb_i = _dot(fre, bb), _dot(fim, bb)
    nyq = _nyquist_slot(pl.program_id(1), af_r.shape)
    row = lax.broadcasted_iota(jnp.int32, af_r.shape, 0)
    sgn = jnp.where(row % 2 == 0, 1.0, -1.0)
    lag0 = ft_ref[0:1, :]
    o_ref[0, 0] = (af_r + ab_r).astype(BF16)
    o_ref[0, 1] = jnp.where(nyq, af_i + ab_i, af_i - ab_i).astype(BF16)
    o_ref[1, 0] = (bf_r + sgn * (af_r - lag0)).astype(BF16)
    o_ref[1, 1] = (bf_i + jnp.where(nyq, af_i - lag0, sgn * af_i)).astype(BF16)
    o_ref[2, 0] = (sgn * ab_r + bb_r).astype(BF16)
    o_ref[2, 1] = jnp.where(nyq, ab_i + bb_i, -(sgn * ab_i + bb_i)).astype(BF16)


def dft_spec(fwd, taps, d, ft, tn):
    f_re, f_im = fwd
    m = f_re.shape[0]
    d_tiles = d // tn
    orders = taps.shape[1] // (2 * d)
    f_spec = pl.BlockSpec((ft, m), lambda j, i: (i, 0))

    def tap_spec(half, direction):
        return pl.BlockSpec((m, tn), lambda j, i: (half, (j // d_tiles) * 2 * d_tiles + direction * d_tiles
                                                    + j % d_tiles))

    return pl.pallas_call(
        _dft_spec_kernel,
        grid=(orders * d_tiles, m // ft),
        in_specs=[f_spec, f_spec, tap_spec(0, 0), tap_spec(1, 0), tap_spec(0, 1), tap_spec(1, 1)],
        out_specs=pl.BlockSpec((3, 2, ft, tn), lambda j, i: (0, 0, i, j)),
        out_shape=jax.ShapeDtypeStruct((3, 2, m, orders * d), BF16),
        compiler_params=_params("parallel", "parallel"),
        name="dft_spec",
    )(f_re, f_im, taps, taps, taps, taps)


def _dft_mul_kernel(has_conv, fre_ref, fim_ref, u_ref, h_ref, *refs):
    o_ref, u_s = refs[-2:]
    i = pl.program_id(2)

    @pl.when(i == 0)
    def _():
        u = u_ref[0]
        if has_conv:
            u = _conv3(u, 0.0, 0.0, *refs[:2])
        u_s[...] = u.astype(BF16)

    m = u_s.shape[0] // 2
    ft = h_ref.shape[2]
    rows = pl.ds(pl.multiple_of(i * ft, ft), ft)
    fre, fim = fre_ref[rows, :], fim_ref[rows, :]
    tr, ti = _dot(fre, u_s[0:m]), _dot(fim, u_s[0:m])
    br, bi = _dot(fre, u_s[m:]), _dot(fim, u_s[m:])
    h0r, h0i, hpr, hpi, hmr, hmi = (h_ref[a, p].astype(F32) for a in range(3) for p in range(2))
    head = 2 * SUBLANES
    nyq = _nyquist_slot(i, (head, tr.shape[1]))

    def emit(half, plane, general, packed):
        o_ref[0, half, plane, 0:head] = jnp.where(nyq, packed, general[0:head]).astype(BF16)
        o_ref[0, half, plane, head:] = general[head:].astype(BF16)

    def top(x):
        return x[0:head]

    emit(0, 0, h0r * tr - h0i * ti + hmr * br - hmi * bi, top(h0r) * top(tr) + top(hmr) * top(br))
    emit(0, 1, h0r * ti + h0i * tr + hmr * bi + hmi * br, top(h0i) * top(ti) + top(hmi) * top(bi))
    emit(1, 0, hpr * tr - hpi * ti + h0r * br - h0i * bi, top(hpr) * top(tr) + top(h0r) * top(br))
    emit(1, 1, hpr * ti + hpi * tr + h0r * bi + h0i * br, top(hpi) * top(ti) + top(h0i) * top(bi))


def dft_mul(fwd, u, u_col, spec, spec_col, d, ft, tn, conv=None):
    f_re, f_im = fwd
    m = f_re.shape[0]
    b = u.shape[0]
    d_tiles = d // tn
    f_spec = _const_spec((m, m))
    conv_specs = [] if conv is None else [pl.BlockSpec((3, tn), lambda bi, j, i: (0, u_col * d_tiles + j)),
                                          pl.BlockSpec((1, tn), lambda bi, j, i: (0, u_col * d_tiles + j))]
    return pl.pallas_call(
        functools.partial(_dft_mul_kernel, conv is not None),
        grid=(b, d_tiles, m // ft),
        in_specs=[f_spec, f_spec,
                  pl.BlockSpec((1, 2 * m, tn), lambda bi, j, i: (bi, 0, u_col * d_tiles + j)),
                  pl.BlockSpec((3, 2, ft, tn), lambda bi, j, i: (0, 0, i, spec_col * d_tiles + j))] + conv_specs,
        out_specs=pl.BlockSpec((1, 2, 2, ft, tn), lambda bi, j, i: (bi, 0, 0, i, j)),
        out_shape=jax.ShapeDtypeStruct((b, 2, 2, m, d), BF16),
        scratch_shapes=[pltpu.VMEM((2 * m, tn), BF16)],
        compiler_params=_params("parallel", "parallel", "arbitrary"),
        name="dft_mul",
    )(f_re, f_im, u, spec, *(conv or ()))


def _dft_inv_kernel(conv_u, per_half, gre_ref, gim_ref, y_ref, *refs):
    u_refs, (g_ref, gp_ref, gn_ref, gw_ref, gb_ref, skip_ref, o_ref) = refs[:-7], refs[-7:]
    u = u_refs[0][0]
    if conv_u:
        u = _conv3(u, *_halo_rows(u_refs[1], u_refs[2], 2), u_refs[3], u_refs[4])
    gate = _conv3(g_ref[0], *_halo_rows(gp_ref, gn_ref, 2), gw_ref, gb_ref)
    tm = o_ref.shape[1]
    rows = pl.ds(pl.multiple_of((pl.program_id(2) % per_half) * tm, tm), tm)
    conv = _dot(gre_ref[rows, :], y_ref[0, 0, 0]) + _dot(gim_ref[rows, :], y_ref[0, 0, 1])
    o_ref[0] = gate * (conv + u * skip_ref[...])


def dft_inv(inv, y, u, u_col, gate, gate_col, skip, tm, tn, conv_w, conv_b, conv_u):
    g_re, g_im = inv
    m = g_re.shape[0]
    b, _, _, _, d = y.shape
    d_tiles = d // tn
    per_half = m // tm
    rows8 = tm // SUBLANES
    last8 = 2 * m // SUBLANES - 1
    g_spec = _const_spec((m, m))

    def tile_specs(col, halo):
        specs = [pl.BlockSpec((1, tm, tn), lambda bi, j, i: (bi, i, col * d_tiles + j))]
        if halo:
            specs += [pl.BlockSpec((1, SUBLANES, tn),
                                   lambda bi, j, i: (bi, jnp.maximum(i * rows8 - 1, 0), col * d_tiles + j)),
                      pl.BlockSpec((1, SUBLANES, tn),
                                   lambda bi, j, i: (bi, jnp.minimum((i + 1) * rows8, last8), col * d_tiles + j)),
                      pl.BlockSpec((3, tn), lambda bi, j, i: (0, col * d_tiles + j)),
                      pl.BlockSpec((1, tn), lambda bi, j, i: (0, col * d_tiles + j))]
        return specs

    u_ops = [u, u, u, conv_w, conv_b] if conv_u else [u]
    return pl.pallas_call(
        functools.partial(_dft_inv_kernel, conv_u, per_half),
        grid=(b, d_tiles, 2 * per_half),
        in_specs=[g_spec, g_spec,
                  pl.BlockSpec((1, 1, 2, m, tn), lambda bi, j, i: (bi, i // per_half, 0, 0, j))]
        + tile_specs(u_col, conv_u) + tile_specs(gate_col, True)
        + [pl.BlockSpec((1, tn), lambda bi, j, i: (0, j))],
        out_specs=pl.BlockSpec((1, tm, tn), lambda bi, j, i: (bi, i, j)),
        out_shape=jax.ShapeDtypeStruct((b, 2 * m, d), F32),
        compiler_params=_params("parallel", "parallel", "parallel"),
        name="dft_inv",
    )(g_re, g_im, y, *u_ops, gate, gate, gate, conv_w, conv_b, skip.reshape(1, d))


def _rope_tables(l):
    rows = l // GRID_W
    row = jnp.repeat(jnp.arange(rows), GRID_W).astype(F32)
    col = jnp.tile(jnp.arange(GRID_W), rows).astype(F32)
    half = HEAD_DIM // 2
    inv_freq = ROPE_THETA ** (-jnp.arange(0, half, 2, dtype=F32) / half)
    ang = jnp.concatenate([row[:, None] * inv_freq, col[:, None] * inv_freq], axis=-1)
    cos, sin = jnp.cos(ang), jnp.sin(ang)
    return jnp.concatenate([cos, cos], axis=-1), jnp.concatenate([sin, sin], axis=-1)


def _hyena_tables(l, d):
    t = jnp.linspace(0.0, 1.0, l, dtype=F32)[:, None]
    omega = (2.0 * math.pi / l) * jnp.arange(l, dtype=F32)[:, None]
    bands = jnp.linspace(1e-4, HYENA_BANDS - 1, HYENA_BANDS, dtype=F32)[None, :]
    feats = jnp.concatenate([t, jnp.cos(bands * omega), -jnp.sin(bands * omega)], axis=-1)
    min_decay = math.log(HYENA_TARGET) / HYENA_SLOW_DECAY
    max_decay = math.log(HYENA_TARGET) / HYENA_FAST_DECAY
    deltas = jnp.abs(jnp.linspace(min_decay, max_decay, d, dtype=F32))
    window = jnp.exp(-t * deltas)
    return feats, window


def _from_scan_layout(y):
    return y.transpose(0, 2, 1).reshape(y.shape[0], -1)


def kernel(x, c, mix_w_in, mix_w_out, attn_q_norm, attn_k_norm, rwkv_mu, rwkv_w0, rwkv_w_up, rwkv_a0, rwkv_a_up, rwkv_g_up, rwkv_k_k, rwkv_k_a, rwkv_r_k, rwkv_ln_g, rwkv_ln_b, hy_w_in, hy_conv_w, hy_conv_b, hy_f_w1, hy_f_b1, hy_f_w2, hy_f_b2, hy_f_w3, hy_f_b3, hy_sin_freq, hy_f_out, hy_skip, hy_w_out, ada_w, ada_b, norm_mix, norm_ffn, ffn_w1, ffn_w3, ffn_w2, final_norm):
    b, l, d = x.shape
    depth = ada_w.shape[0]
    tm = min(256, l)
    tp = min(FFN_ROWS, l)
    c_w = RWKV_WIDTH

    mod4 = ada_mod(c, ada_w, ada_b).reshape(depth, b, 1, 6 * d)

    perm = jnp.concatenate([jnp.arange(0, HEAD_DIM, 2), jnp.arange(1, HEAD_DIM, 2)])
    w_in = mix_w_in[0]
    w_q = w_in[:, :ATTN_WIDTH].reshape(d, ATTN_HEADS, HEAD_DIM)[:, :, perm].reshape(d, ATTN_WIDTH)
    w_kv = w_in[:, ATTN_WIDTH:ATTN_WIDTH + 2 * KV_WIDTH]
    w_k = w_kv[:, :KV_WIDTH].reshape(d, ATTN_KV_HEADS, HEAD_DIM)[:, :, perm].reshape(d, KV_WIDTH)
    w_kv = jnp.concatenate([w_k, w_kv[:, KV_WIDTH:]], axis=1)
    w_rw = jnp.pad(w_in[:, ATTN_WIDTH + 2 * KV_WIDTH:], ((0, 0), (0, RWKV_IN_PAD - RWKV_IN)))
    q, kv, rw = proj(x, mod4, 0, norm_mix[0], [w_q.astype(BF16), w_kv.astype(BF16), w_rw.astype(BF16)], tp)

    seg = jnp.kron(jnp.eye(RWKV_HEADS, dtype=BF16), jnp.ones((RWKV_HEAD, RWKV_HEAD), BF16))
    r, z, v, w, kd, bb, bonus, g = rwkv_prep(
        rw, jnp.pad(rwkv_mu[0], (0, RWKV_IN_PAD - RWKV_IN)).reshape(1, RWKV_IN_PAD),
        rwkv_w0[0].reshape(2, 1, c_w), rwkv_w_up[0], rwkv_a0[0].reshape(2, 1, c_w), rwkv_a_up[0],
        jnp.pad(rwkv_g_up[0], ((0, RWKV_IN_PAD - RWKV_IN), (0, 0))),
        rwkv_k_k[0].reshape(1, c_w), rwkv_k_a[0].reshape(1, c_w), rwkv_r_k[0].reshape(1, c_w), seg, tm)
    stage1 = [a.T for a in (r, w, kd, v, z, bb)]
    cos, sin = _rope_tables(l)
    qn, kn = attn_q_norm[0][perm].reshape(1, HEAD_DIM), attn_k_norm[0][perm].reshape(1, HEAD_DIM)
    half = b // 2
    y_attn0 = attn(q, kv, cos, sin, qn, kn, tm, 0, half, after=(bonus,))
    tn = min(512, d)
    feats, window = _hyena_tables(l, d)
    k_pad = 128
    feats = jnp.pad(feats, ((0, 0), (0, k_pad - feats.shape[1])))
    f_w1 = jnp.pad(hy_f_w1[0], ((0, k_pad - hy_f_w1.shape[1]), (0, 0)))
    taps = hy_filter(feats, f_w1, hy_f_b1[0], hy_f_w2[0], hy_f_b2[0], hy_f_w3[0], hy_f_b3[0],
                     hy_sin_freq[0], hy_f_out[0], window, tn)
    stage1, y_attn0, taps = lax.optimization_barrier((stage1, y_attn0, taps))
    scan_operands = [a.reshape(-1, RWKV_HEAD, l).transpose(2, 1, 0) for a in stage1]
    y_attn1 = attn(q, kv, cos, sin, qn, kn, tm, half, b - half)
    fwd, inv = _dft_matrices(l // 2)
    ft = min(FREQ_TILE, l // 2)
    spec = dft_spec(fwd, taps, d, ft, tn)

    y_f, y_b = rwkv_scan(scan_operands, min(SCAN_BLOCK, l), after=(y_attn1, spec))
    fwd_lane = (jnp.arange(y_f.shape[2]) // RWKV_HEADS) % 2 == 0
    y_t = _from_scan_layout(jnp.where(fwd_lane[None, None, :], y_f, y_b))
    x = mix_out([y_attn0, y_attn1, (y_t, 0), (y_t, 1), bonus, g],
                [rwkv_ln_g[0].reshape(1, c_w), rwkv_ln_b[0].reshape(1, c_w), seg],
                x, mod4, 0, mix_w_out[0].astype(BF16), tm, True)
    x = ffn(x, mod4, 0, norm_ffn[0], ffn_w1[0].astype(BF16), ffn_w3[0].astype(BF16), ffn_w2[0].astype(BF16),
            final_norm, min(FFN_ROWS, l), False)

    (p3,) = proj(x, mod4, 1, norm_mix[1], [hy_w_in[0].astype(BF16)], tp)
    conv_w, conv_b = hy_conv_w[0], hy_conv_b[0].reshape(1, 3 * d)
    y1 = dft_mul(fwd, p3, 0, spec, 0, d, ft, tn, conv=(conv_w, conv_b))
    z1 = dft_inv(inv, y1, p3, 0, p3, 1, hy_skip[0, 0], ft, tn, conv_w, conv_b, True)
    y2 = dft_mul(fwd, z1, 0, spec, 1, d, ft, tn)
    z2 = dft_inv(inv, y2, z1, 0, p3, 2, hy_skip[0, 1], ft, tn, conv_w, conv_b, False)
    x = mix_out([z2], [], x, mod4, 1, hy_w_out[0].astype(BF16), tp, False)
    x = ffn(x, mod4, 1, norm_ffn[1], ffn_w1[1].astype(BF16), ffn_w3[1].astype(BF16), ffn_w2[1].astype(BF16),
            final_norm, min(FFN_ROWS, l), True)
    return x
```

```python
import functools
import math

import jax
import jax.numpy as jnp
from jax import lax
from jax.experimental import pallas as pl
from jax.experimental.pallas import tpu as pltpu

F32 = jnp.float32
BF16 = jnp.bfloat16
HIGHEST = lax.Precision.HIGHEST

GRID_W = 64
HEAD_DIM = 64
ATTN_HEADS = 8
ATTN_KV_HEADS = 2
ATTN_WIDTH = ATTN_HEADS * HEAD_DIM
KV_WIDTH = ATTN_KV_HEADS * HEAD_DIM
ROPE_THETA = 10000.0
RWKV_HEADS = 8
RWKV_HEAD = 64
RWKV_WIDTH = RWKV_HEADS * RWKV_HEAD
DECAY_LORA = 64
AAA_LORA = 64
GATE_LORA = 160
RWKV_IN = 3 * RWKV_WIDTH + 2 * DECAY_LORA + 2 * AAA_LORA + GATE_LORA
RWKV_IN_PAD = 2048
GN_EPS = 64e-5
HYENA_BANDS = 16
HYENA_TARGET = 1e-2
HYENA_FAST_DECAY = 0.3
HYENA_SLOW_DECAY = 1.5
NORM_EPS = 1e-6

V7X_VMEM_BYTES = 64 * 1024 * 1024
VMEM_LIMIT = V7X_VMEM_BYTES - 8 * 1024 * 1024
FREQ_TILE = 512
SCAN_BLOCK = 32
FFN_ROWS = 512
SUBLANES = 8


def _params(*sem):
    return pltpu.CompilerParams(dimension_semantics=sem, vmem_limit_bytes=VMEM_LIMIT)


def _const_spec(shape):
    zeros = (0,) * len(shape)
    return pl.BlockSpec(shape, lambda *_: zeros, pipeline_mode=pl.Buffered(1))


def _dot(a, b):
    return jnp.dot(a, b, preferred_element_type=F32)


def _split(x):
    hi = x.astype(BF16)
    return hi, (x - hi.astype(F32)).astype(BF16)


def _dot_split(a, b):
    ah, al = _split(a)
    bh, bl = _split(b)
    return _dot(ah, bh) + (_dot(ah, bl) + _dot(al, bh))


def _seg_sum(a, seg):
    ah, al = _split(a)
    return _dot(ah, seg) + _dot(al, seg)


def _dot_hi(a, b):
    return jnp.dot(a, b, precision=HIGHEST, preferred_element_type=F32)


def _rms_mod(x, gain, scale, shift):
    ms = jnp.mean(x * x, axis=-1, keepdims=True)
    return (x * lax.rsqrt(ms + NORM_EPS) * gain) * (1.0 + scale) + shift


def _ada_kernel(c_ref, w_ref, b_ref, o_ref):
    c = c_ref[...]
    cond = c * jax.nn.sigmoid(c)
    o_ref[...] = _dot_hi(cond, w_ref[...]) + b_ref[...]


def ada_mod(c, ada_w, ada_b):
    depth, d, n = ada_w.shape
    b = c.shape[0]
    tn = 1536
    return pl.pallas_call(
        _ada_kernel,
        grid=(depth, n // tn),
        in_specs=[pl.BlockSpec((b, d), lambda l, j: (0, 0)),
                  pl.BlockSpec((None, d, tn), lambda l, j: (l, 0, j)),
                  pl.BlockSpec((None, 1, tn), lambda l, j: (l, 0, j))],
        out_specs=pl.BlockSpec((None, b, tn), lambda l, j: (l, 0, j)),
        out_shape=jax.ShapeDtypeStruct((depth, b, n), F32),
        compiler_params=_params("parallel", "parallel"),
        name="ada_mod",
    )(c, ada_w, ada_b.reshape(depth, 1, n))


def _mod_spec(d, layer, chunk):
    return pl.BlockSpec((None, None, 1, d), lambda b, *_: (layer, b, 0, chunk))


def _proj_kernel(n_out, x_ref, sh_ref, sc_ref, g_ref, *refs):
    h = _rms_mod(x_ref[0], g_ref[...], sc_ref[...], sh_ref[...]).astype(BF16)
    for w_ref, o_ref in zip(refs[:n_out], refs[n_out:]):
        o_ref[0] = _dot(h, w_ref[...])


def proj(x, mod4, layer, gain, weights, tm):
    b, l, d = x.shape
    n_out = len(weights)
    return pl.pallas_call(
        functools.partial(_proj_kernel, n_out),
        grid=(b, l // tm),
        in_specs=[pl.BlockSpec((1, tm, d), lambda bi, i: (bi, i, 0)),
                  _mod_spec(d, layer, 0), _mod_spec(d, layer, 1),
                  _const_spec((1, d))] + [_const_spec(w.shape) for w in weights],
        out_specs=[pl.BlockSpec((1, tm, w.shape[1]), lambda bi, i: (bi, i, 0)) for w in weights],
        out_shape=[jax.ShapeDtypeStruct((b, l, w.shape[1]), F32) for w in weights],
        compiler_params=_params("parallel", "parallel"),
        name="proj",
    )(x, mod4, mod4, gain.reshape(1, d), *weights)


def _headnorm_rope(x, gain, cos, sin):
    ms = jnp.mean(x * x, axis=-1, keepdims=True)
    y = x * lax.rsqrt(ms + NORM_EPS) * gain
    half = HEAD_DIM // 2
    rot = jnp.concatenate([-y[:, half:], y[:, :half]], axis=-1)
    return y * cos + rot * sin


def _attn_kernel(q_ref, kv_ref, cq_ref, sq_ref, ck_ref, sk_ref, qn_ref, kn_ref, *refs):
    o_ref, kt_s, v_s = refs[-3:]
    @pl.when(pl.program_id(1) == 0)
    def _():
        kv = kv_ref[0]
        ones_col = (lax.broadcasted_iota(jnp.int32, (kv.shape[0], HEAD_DIM), 1) == 0).astype(F32)
        for hk in range(ATTN_KV_HEADS):
            k = kv[:, HEAD_DIM * hk:HEAD_DIM * (hk + 1)]
            kt_s[hk] = _headnorm_rope(k, kn_ref[...], ck_ref[...], sk_ref[...]).T.astype(BF16)
            v = kv[:, KV_WIDTH + HEAD_DIM * hk:KV_WIDTH + HEAD_DIM * (hk + 1)]
            v_s[hk] = jnp.concatenate([v, ones_col], axis=-1).astype(BF16)

    q = q_ref[0]
    group = ATTN_HEADS // ATTN_KV_HEADS
    outs = []
    for h in range(ATTN_HEADS):
        qh = _headnorm_rope(q[:, HEAD_DIM * h:HEAD_DIM * (h + 1)], qn_ref[...], cq_ref[...], sq_ref[...])
        qh = (qh * (HEAD_DIM ** -0.5 * math.log2(math.e))).astype(BF16)
        s = _dot(qh, kt_s[h // group])
        p = jnp.exp2(s - jnp.max(s, axis=-1, keepdims=True)).astype(BF16)
        o = _dot(p, v_s[h // group])
        outs.append(o[:, :HEAD_DIM] / o[:, HEAD_DIM:HEAD_DIM + 1])
    o_ref[0] = jnp.concatenate(outs, axis=-1)


def _anchor_specs(after):
    return [pl.BlockSpec((1,) * (a.ndim - 2) + (SUBLANES, 128), lambda *_, nd=a.ndim: (0,) * nd) for a in after]


def attn(q, kv, cos, sin, qn, kn, tq, b0, nb, after=()):
    _, l, _ = q.shape
    b = nb
    return pl.pallas_call(
        _attn_kernel,
        grid=(nb, l // tq),
        in_specs=[pl.BlockSpec((1, tq, ATTN_WIDTH), lambda bi, i: (b0 + bi, i, 0)),
                  pl.BlockSpec((1, l, 2 * KV_WIDTH), lambda bi, i: (b0 + bi, 0, 0)),
                  pl.BlockSpec((tq, HEAD_DIM), lambda bi, i: (i, 0)),
                  pl.BlockSpec((tq, HEAD_DIM), lambda bi, i: (i, 0)),
                  _const_spec((l, HEAD_DIM)), _const_spec((l, HEAD_DIM)),
                  _const_spec((1, HEAD_DIM)), _const_spec((1, HEAD_DIM))] + _anchor_specs(after),
        out_specs=pl.BlockSpec((1, tq, ATTN_WIDTH), lambda bi, i: (bi, i, 0)),
        out_shape=jax.ShapeDtypeStruct((b, l, ATTN_WIDTH), F32),
        scratch_shapes=[pltpu.VMEM((ATTN_KV_HEADS, HEAD_DIM, l), BF16),
                        pltpu.VMEM((ATTN_KV_HEADS, l, 2 * HEAD_DIM), BF16)],
        compiler_params=_params("parallel", "arbitrary"),
        name="attn",
    )(q, kv, cos, sin, cos, sin, qn, kn, *after)


def _halo_specs(tt, w, l):
    nb = l // SUBLANES
    per = tt // SUBLANES
    main = pl.BlockSpec((1, tt, w), lambda bi, i: (bi, i, 0))
    prev = pl.BlockSpec((1, SUBLANES, w), lambda bi, i: (bi, jnp.maximum(i * per - 1, 0), 0))
    nxt = pl.BlockSpec((1, SUBLANES, w), lambda bi, i: (bi, jnp.minimum((i + 1) * per, nb - 1), 0))
    return [main, prev, nxt]


def _neighbours(cur, prow, nrow):
    tt = cur.shape[0]
    row = lax.broadcasted_iota(jnp.int32, cur.shape, 0)
    prev = jnp.where(row == 0, prow, pltpu.roll(cur, 1, 0))
    nxt = jnp.where(row == tt - 1, nrow, pltpu.roll(cur, tt - 1, 0))
    return prev, nxt


def _halo_rows(xp_ref, xn_ref, axis):
    i = pl.program_id(axis)
    last = pl.num_programs(axis) - 1
    prow = jnp.where(i > 0, xp_ref[0, SUBLANES - 1:SUBLANES, :], 0.0)
    nrow = jnp.where(i < last, xn_ref[0, 0:1, :], 0.0)
    return prow, nrow


def _rwkv_prep_kernel(x_ref, xp_ref, xn_ref, mu_ref, w0_ref, wup_ref, a0_ref, aup_ref, gup_ref,
                      kk_ref, ka_ref, rk_ref, seg_ref,
                      r_o, z_o, v_o, w_o, kd_o, b_o, bonus_o, g_o):
    cur = x_ref[0]
    prev, nxt = _neighbours(cur, *_halo_rows(xp_ref, xn_ref, 1))
    ps = cur + (0.5 * (prev + nxt) - cur) * mu_ref[...]
    c = RWKV_WIDTH
    r = ps[:, 0:c]
    k = ps[:, c:2 * c]
    v = ps[:, 2 * c:3 * c]
    o_a = 3 * c + 2 * DECAY_LORA
    o_g = o_a + 2 * AAA_LORA
    kk = k * kk_ref[...]
    kk = kk * lax.rsqrt(jnp.maximum(_seg_sum(kk * kk, seg_ref[...]), 1e-24))
    r_t, v_t, z_t = r.T, v.T, (-kk).T
    for d in range(2):
        rows = slice(c * d, c * (d + 1))
        r_o[rows, :] = r_t
        v_o[rows, :] = v_t
        z_o[rows, :] = z_t
        w_lo = ps[:, 3 * c + DECAY_LORA * d:3 * c + DECAY_LORA * (d + 1)]
        a_lo = ps[:, o_a + AAA_LORA * d:o_a + AAA_LORA * (d + 1)]
        decay = jnp.exp(-math.exp(-0.5) * jax.nn.sigmoid(w0_ref[d] + _dot_split(jnp.tanh(w_lo), wup_ref[d])))
        a = jax.nn.sigmoid(a0_ref[d] + _dot_split(a_lo, aup_ref[d]))
        w_o[rows, :] = decay.T
        kd_o[rows, :] = (k * (1.0 + (a - 1.0) * ka_ref[...])).T
        b_o[rows, :] = (kk * a).T
    bonus_o[0] = _seg_sum(r * k * rk_ref[...], seg_ref[...]) * v
    g_o[0] = _dot_split(jax.nn.sigmoid(ps[:, o_g:RWKV_IN_PAD]), gup_ref[...])


def rwkv_prep(rw, mu, w0, w_up, a0, a_up, g_up, k_k, k_a, r_k, seg, tt):
    b, l, w = rw.shape
    c = RWKV_WIDTH
    natural = jax.ShapeDtypeStruct((b, l, c), F32)
    time_major = jax.ShapeDtypeStruct((b * 2 * c, l), F32)
    spec_n = pl.BlockSpec((1, tt, c), lambda bi, i: (bi, i, 0))
    spec_t = pl.BlockSpec((2 * c, tt), lambda bi, i: (bi, i))
    consts = [mu, w0, w_up, a0, a_up, g_up, k_k, k_a, r_k, seg]
    return pl.pallas_call(
        _rwkv_prep_kernel,
        grid=(b, l // tt),
        in_specs=_halo_specs(tt, w, l) + [_const_spec(a.shape) for a in consts],
        out_specs=[spec_t] * 6 + [spec_n] * 2,
        out_shape=[time_major] * 6 + [natural] * 2,
        compiler_params=_params("parallel", "parallel"),
        name="rwkv_prep",
    )(rw, rw, rw, *consts)


SCAN_ROWS = 32
_R, _W, _KD, _V, _Z, _B = range(6)


def _rwkv_scan_kernel(*refs):
    ins, (yf_ref, yb_ref, s_ref, m_ref, sz_ref) = refs[:12], refs[-5:]
    tb, n, lanes = ins[0].shape
    fwd_lane = (lax.broadcasted_iota(jnp.int32, (n, lanes), 1) // RWKV_HEADS) % 2 == 0

    def stage(which, t):
        m_ref[which] = jnp.where(fwd_lane, ins[2 * which][t], ins[2 * which + 1][tb - 1 - t])

    @pl.when(pl.program_id(0) == 0)
    def _():
        s_ref[...] = jnp.zeros_like(s_ref)

    halves = [slice(h * SCAN_ROWS, (h + 1) * SCAN_ROWS) for h in range(n // SCAN_ROWS)]

    stage(_Z, 0)
    for rows in halves:
        acc = s_ref[0, rows, :] * m_ref[_Z, 0:1, :]
        for k in range(1, n):
            acc = acc + s_ref[k, rows, :] * m_ref[_Z, k:k + 1, :]
        sz_ref[rows, :] = acc

    def step(t, carry):
        for which in (_R, _W, _KD, _V, _B):
            stage(which, t)
        stage(_Z, jnp.minimum(t + 1, tb - 1))
        for rows in halves:
            sz = sz_ref[rows, :]
            vt = m_ref[_V, rows, :]
            y = None
            sz_next = None
            for k in range(n):
                sk = (s_ref[k, rows, :] * m_ref[_W, k:k + 1, :] + sz * m_ref[_B, k:k + 1, :]
                      + vt * m_ref[_KD, k:k + 1, :])
                s_ref[k, rows, :] = sk
                yk = sk * m_ref[_R, k:k + 1, :]
                zk = sk * m_ref[_Z, k:k + 1, :]
                y = yk if y is None else y + yk
                sz_next = zk if sz_next is None else sz_next + zk
            sz_ref[rows, :] = sz_next
            yf_ref[t, rows, :] = y
            yb_ref[tb - 1 - t, rows, :] = y
        return carry

    lax.fori_loop(0, tb, step, 0)


def rwkv_scan(arrays, tb, after=()):
    l, n, lanes = arrays[0].shape
    nt = l // tb
    fwd = pl.BlockSpec((tb, n, lanes), lambda i: (i, 0, 0))
    rev = pl.BlockSpec((tb, n, lanes), lambda i: (nt - 1 - i, 0, 0))
    out = jax.ShapeDtypeStruct((l, n, lanes), F32)
    return pl.pallas_call(
        _rwkv_scan_kernel,
        grid=(nt,),
        in_specs=[fwd, rev] * 6 + _anchor_specs(after),
        out_specs=[fwd, rev],
        out_shape=[out, out],
        scratch_shapes=[pltpu.VMEM((n, n, lanes), F32), pltpu.VMEM((6, n, lanes), F32),
                        pltpu.VMEM((n, lanes), F32)],
        compiler_params=_params("arbitrary"),
        name="rwkv_scan",
    )(*[a for a in arrays for _ in range(2)], *after)


def _mix_out_kernel(rwkv, *refs):
    if rwkv:
        (ya0_ref, ya1_ref, yf_ref, yb_ref, bonus_ref, g_ref, lng_ref, lnb_ref, seg_ref,
         x_ref, gate_ref, w_ref, o_ref) = refs
        first_half = pl.program_id(0) < pl.num_programs(0) // 2
        y_attn = jnp.where(first_half, ya0_ref[0], ya1_ref[0])
        y = yf_ref[...] + yb_ref[...]
        inv_n = 1.0 / RWKV_HEAD
        mean = _seg_sum(y, seg_ref[...]) * inv_n
        cen = y - mean
        var = _seg_sum(cen * cen, seg_ref[...]) * inv_n
        yn = cen * lax.rsqrt(var + GN_EPS) * lng_ref[...] + lnb_ref[...]
        yr = (yn + bonus_ref[0]) * g_ref[0]
        a = jnp.concatenate([y_attn, yr], axis=-1)
    else:
        a_ref, x_ref, gate_ref, w_ref, o_ref = refs
        a = a_ref[0]
    o_ref[0] = x_ref[0] + gate_ref[...] * _dot(a.astype(BF16), w_ref[...])


def mix_out(acts, consts, x, mod4, layer, w, tm, rwkv):
    b, l, d = x.shape
    in_specs = []
    for i_act, a in enumerate(acts):
        if not isinstance(a, tuple) and a.shape[0] == b // 2:
            in_specs.append(pl.BlockSpec((1, tm, a.shape[2]), lambda bi, i: (bi % (b // 2), i, 0)))
        elif isinstance(a, tuple):
            arr, direction = a
            width = arr.shape[1] // (2 * b)
            in_specs.append(pl.BlockSpec((tm, width), lambda bi, i, dr=direction: (i, 2 * bi + dr)))
            acts[i_act] = arr
        else:
            in_specs.append(pl.BlockSpec((1, tm, a.shape[2]), lambda bi, i: (bi, i, 0)))
    in_specs += [_const_spec(a.shape) for a in consts]
    in_specs += [pl.BlockSpec((1, tm, d), lambda bi, i: (bi, i, 0)), _mod_spec(d, layer, 2), _const_spec(w.shape)]
    return pl.pallas_call(
        functools.partial(_mix_out_kernel, rwkv),
        grid=(b, l // tm),
        in_specs=in_specs,
        out_specs=pl.BlockSpec((1, tm, d), lambda bi, i: (bi, i, 0)),
        out_shape=jax.ShapeDtypeStruct((b, l, d), F32),
        compiler_params=_params("parallel", "parallel"),
        name="mix_out",
    )(*acts, *consts, x, mod4, w)


def _ffn_kernel(final, x_ref, sh_ref, sc_ref, gate_ref, g_ref, w1_ref, w3_ref, w2_ref, fg_ref, o_ref):
    x = x_ref[0]
    h = _rms_mod(x, g_ref[...], sc_ref[...], sh_ref[...]).astype(BF16)
    a = _dot(h, w1_ref[...])
    u = (a * jax.nn.sigmoid(a)) * _dot(h, w3_ref[...])
    out = x + gate_ref[...] * _dot(u.astype(BF16), w2_ref[...])
    if final:
        ms = jnp.mean(out * out, axis=-1, keepdims=True)
        out = out * lax.rsqrt(ms + NORM_EPS) * fg_ref[...]
    o_ref[0] = out


def ffn(x, mod4, layer, gain, w1, w3, w2, final_gain, tm, final):
    b, l, d = x.shape
    return pl.pallas_call(
        functools.partial(_ffn_kernel, final),
        grid=(b, l // tm),
        in_specs=[pl.BlockSpec((1, tm, d), lambda bi, i: (bi, i, 0)),
                  _mod_spec(d, layer, 3), _mod_spec(d, layer, 4), _mod_spec(d, layer, 5),
                  _const_spec((1, d)), _const_spec(w1.shape), _const_spec(w3.shape), _const_spec(w2.shape),
                  _const_spec((1, d))],
        out_specs=pl.BlockSpec((1, tm, d), lambda bi, i: (bi, i, 0)),
        out_shape=jax.ShapeDtypeStruct((b, l, d), F32),
        compiler_params=_params("parallel", "parallel"),
        name="ffn",
    )(x, mod4, mod4, mod4, gain.reshape(1, d), w1, w3, w2, final_gain.reshape(1, d))


def _conv3(cur, prow, nrow, w_ref, b_ref):
    prev, nxt = _neighbours(cur, prow, nrow)
    return prev * w_ref[0:1, :] + cur * w_ref[1:2, :] + nxt * w_ref[2:3, :] + b_ref[...]


def _hy_filter_kernel(d_tiles, feats_ref, w1_ref, b1_ref, w2_ref, b2_ref, w3_ref, b3_ref, fr_ref,
                      wout_ref, win_ref, o_ref, hid_s):
    j = pl.program_id(0)

    @pl.when(j == 0)
    def _():
        fr = fr_ref[...]
        hid = jnp.sin(fr * (_dot_hi(feats_ref[...], w1_ref[...]) + b1_ref[...]))
        hid = jnp.sin(fr * (_dot_hi(hid, w2_ref[...]) + b2_ref[...]))
        hid_s[...] = jnp.sin(fr * (_dot_hi(hid, w3_ref[...]) + b3_ref[...]))

    filt = _dot_hi(hid_s[...], wout_ref[...]) * win_ref[...]
    backward = (j // d_tiles) % 2 == 1
    row = lax.broadcasted_iota(jnp.int32, filt.shape, 0)
    o_ref[...] = jnp.where(jnp.logical_and(backward, row == 0), 0.0, filt)


def hy_filter(feats, w1, b1, w2, b2, w3, b3, fr, w_out, window, tn):
    l = feats.shape[0]
    hf = w2.shape[0]
    n = w_out.shape[1]
    d = window.shape[1]
    d_tiles = d // tn
    consts = [feats, w1, b1.reshape(1, hf), w2, b2.reshape(1, hf), w3, b3.reshape(1, hf), fr.reshape(1, hf)]
    return pl.pallas_call(
        functools.partial(_hy_filter_kernel, d_tiles),
        grid=(n // tn,),
        in_specs=[_const_spec(a.shape) for a in consts]
        + [pl.BlockSpec((hf, tn), lambda j: (0, j)), pl.BlockSpec((l, tn), lambda j: (0, j % d_tiles))],
        out_specs=pl.BlockSpec((l, tn), lambda j: (0, j)),
        out_shape=jax.ShapeDtypeStruct((l, n), F32),
        scratch_shapes=[pltpu.VMEM((l, hf), F32)],
        compiler_params=_params("arbitrary"),
        name="hy_filter",
    )(*consts, w_out, window)


def _dft_matrices(l):
    n = 2 * l
    lane = min(128, l)
    k = jnp.arange(l, dtype=jnp.int32)[:, None]
    t = jnp.arange(l, dtype=jnp.int32)
    a = ((k * jnp.arange(lane, dtype=jnp.int32)[None, :]) % n).astype(F32) * (2.0 * math.pi / n)
    c = ((k * (lane * jnp.arange(l // lane, dtype=jnp.int32))[None, :]) % n).astype(F32) * (2.0 * math.pi / n)
    ca, sa, cc, sc = jnp.cos(a)[:, None, :], jnp.sin(a)[:, None, :], jnp.cos(c)[:, :, None], jnp.sin(c)[:, :, None]
    cos = (ca * cc - sa * sc).reshape(l, l)
    sin = (sa * cc + ca * sc).reshape(l, l)
    alt = jnp.where(t % 2 == 0, 1.0, -1.0).astype(F32)
    f_re = cos.astype(BF16)
    f_im = jnp.where(k == 0, alt[None, :], -sin).astype(BF16)
    col0 = (t == 0)[None, :]
    g_re = jnp.where(col0, 1.0 / n, (2.0 / n) * cos).astype(BF16)
    g_im = jnp.where(col0, alt[:, None] / n, (-2.0 / n) * sin).astype(BF16)
    return (f_re, f_im), (g_re, g_im)


def _nyquist_slot(i, shape):
    row = lax.broadcasted_iota(jnp.int32, shape, 0)
    return jnp.logical_and(i == 0, row == 0)


def _dft_spec_kernel(fre_ref, fim_ref, ft_ref, fb_ref, bt_ref, bb_ref, o_ref):
    fre, fim = fre_ref[...], fim_ref[...]
    ft, fb, bt, bb = (ref[...].astype(BF16) for ref in (ft_ref, fb_ref, bt_ref, bb_ref))
    af_r, af_i = _dot(fre, ft), _dot(fim, ft)
    bf_r, bf_i = _dot(fre, fb), _dot(fim, fb)
    ab_r, ab_i = _dot(fre, bt), _dot(fim, bt)
    bb_r, bb_i = _dot(fre, bb), _dot(fim, bb)
    nyq = _nyquist_slot(pl.program_id(1), af_r.shape)
    row = lax.broadcasted_iota(jnp.int32, af_r.shape, 0)
    sgn = jnp.where(row % 2 == 0, 1.0, -1.0)
    lag0 = ft_ref[0:1, :]
    o_ref[0, 0] = (af_r + ab_r).astype(BF16)
    o_ref[0, 1] = jnp.where(nyq, af_i + ab_i, af_i - ab_i).astype(BF16)
    o_ref[1, 0] = (bf_r + sgn * (af_r - lag0)).astype(BF16)
    o_ref[1, 1] = (bf_i + jnp.where(nyq, af_i - lag0, sgn * af_i)).astype(BF16)
    o_ref[2, 0] = (sgn * ab_r + bb_r).astype(BF16)
    o_ref[2, 1] = jnp.where(nyq, ab_i + bb_i, -(sgn * ab_i + bb_i)).astype(BF16)


def dft_spec(fwd, taps, d, ft, tn):
    f_re, f_im = fwd
    m = f_re.shape[0]
    d_tiles = d // tn
    orders = taps.shape[1] // (2 * d)
    f_spec = pl.BlockSpec((ft, m), lambda j, i: (i, 0))

    def tap_spec(half, direction):
        return pl.BlockSpec((m, tn), lambda j, i: (half, (j // d_tiles) * 2 * d_tiles + direction * d_tiles
                                                    + j % d_tiles))

    return pl.pallas_call(
        _dft_spec_kernel,
        grid=(orders * d_tiles, m // ft),
        in_specs=[f_spec, f_spec, tap_spec(0, 0), tap_spec(1, 0), tap_spec(0, 1), tap_spec(1, 1)],
        out_specs=pl.BlockSpec((3, 2, ft, tn), lambda j, i: (0, 0, i, j)),
        out_shape=jax.ShapeDtypeStruct((3, 2, m, orders * d), BF16),
        compiler_params=_params("parallel", "parallel"),
        name="dft_spec",
    )(f_re, f_im, taps, taps, taps, taps)


def _dft_mul_kernel(has_conv, fre_ref, fim_ref, u_ref, h_ref, *refs):
    o_ref, u_s = refs[-2:]
    i = pl.program_id(2)

    @pl.when(i == 0)
    def _():
        u = u_ref[0]
        if has_conv:
            u = _conv3(u, 0.0, 0.0, *refs[:2])
        u_s[...] = u.astype(BF16)

    m = u_s.shape[0] // 2
    ft = h_ref.shape[2]
    rows = pl.ds(pl.multiple_of(i * ft, ft), ft)
    fre, fim = fre_ref[rows, :], fim_ref[rows, :]
    tr, ti = _dot(fre, u_s[0:m]), _dot(fim, u_s[0:m])
    br, bi = _dot(fre, u_s[m:]), _dot(fim, u_s[m:])
    h0r, h0i, hpr, hpi, hmr, hmi = (h_ref[a, p].astype(F32) for a in range(3) for p in range(2))
    head = 2 * SUBLANES
    nyq = _nyquist_slot(i, (head, tr.shape[1]))

    def emit(half, plane, general, packed):
        o_ref[0, half, plane, 0:head] = jnp.where(nyq, packed, general[0:head]).astype(BF16)
        o_ref[0, half, plane, head:] = general[head:].astype(BF16)

    def top(x):
        return x[0:head]

    emit(0, 0, h0r * tr - h0i * ti + hmr * br - hmi * bi, top(h0r) * top(tr) + top(hmr) * top(br))
    emit(0, 1, h0r * ti + h0i * tr + hmr * bi + hmi * br, top(h0i) * top(ti) + top(hmi) * top(bi))
    emit(1, 0, hpr * tr - hpi * ti + h0r * br - h0i * bi, top(hpr) * top(tr) + top(h0r) * top(br))
    emit(1, 1, hpr * ti + hpi * tr + h0r * bi + h0i * br, top(hpi) * top(ti) + top(h0i) * top(bi))


def dft_mul(fwd, u, u_col, spec, spec_col, d, ft, tn, conv=None):
    f_re, f_im = fwd
    m = f_re.shape[0]
    b = u.shape[0]
    d_tiles = d // tn
    f_spec = _const_spec((m, m))
    conv_specs = [] if conv is None else [pl.BlockSpec((3, tn), lambda bi, j, i: (0, u_col * d_tiles + j)),
                                          pl.BlockSpec((1, tn), lambda bi, j, i: (0, u_col * d_tiles + j))]
    return pl.pallas_call(
        functools.partial(_dft_mul_kernel, conv is not None),
        grid=(b, d_tiles, m // ft),
        in_specs=[f_spec, f_spec,
                  pl.BlockSpec((1, 2 * m, tn), lambda bi, j, i: (bi, 0, u_col * d_tiles + j)),
                  pl.BlockSpec((3, 2, ft, tn), lambda bi, j, i: (0, 0, i, spec_col * d_tiles + j))] + conv_specs,
        out_specs=pl.BlockSpec((1, 2, 2, ft, tn), lambda bi, j, i: (bi, 0, 0, i, j)),
        out_shape=jax.ShapeDtypeStruct((b, 2, 2, m, d), BF16),
        scratch_shapes=[pltpu.VMEM((2 * m, tn), BF16)],
        compiler_params=_params("parallel", "parallel", "arbitrary"),
        name="dft_mul",
    )(f_re, f_im, u, spec, *(conv or ()))


def _dft_inv_kernel(conv_u, per_half, gre_ref, gim_ref, y_ref, *refs):
    u_refs, (g_ref, gp_ref, gn_ref, gw_ref, gb_ref, skip_ref, o_ref) = refs[:-7], refs[-7:]
    u = u_refs[0][0]
    if conv_u:
        u = _conv3(u, *_halo_rows(u_refs[1], u_refs[2], 2), u_refs[3], u_refs[4])
    gate = _conv3(g_ref[0], *_halo_rows(gp_ref, gn_ref, 2), gw_ref, gb_ref)
    tm = o_ref.shape[1]
    rows = pl.ds(pl.multiple_of((pl.program_id(2) % per_half) * tm, tm), tm)
    conv = _dot(gre_ref[rows, :], y_ref[0, 0, 0]) + _dot(gim_ref[rows, :], y_ref[0, 0, 1])
    o_ref[0] = gate * (conv + u * skip_ref[...])


def dft_inv(inv, y, u, u_col, gate, gate_col, skip, tm, tn, conv_w, conv_b, conv_u):
    g_re, g_im = inv
    m = g_re.shape[0]
    b, _, _, _, d = y.shape
    d_tiles = d // tn
    per_half = m // tm
    rows8 = tm // SUBLANES
    last8 = 2 * m // SUBLANES - 1
    g_spec = _const_spec((m, m))

    def tile_specs(col, halo):
        specs = [pl.BlockSpec((1, tm, tn), lambda bi, j, i: (bi, i, col * d_tiles + j))]
        if halo:
            specs += [pl.BlockSpec((1, SUBLANES, tn),
                                   lambda bi, j, i: (bi, jnp.maximum(i * rows8 - 1, 0), col * d_tiles + j)),
                      pl.BlockSpec((1, SUBLANES, tn),
                                   lambda bi, j, i: (bi, jnp.minimum((i + 1) * rows8, last8), col * d_tiles + j)),
                      pl.BlockSpec((3, tn), lambda bi, j, i: (0, col * d_tiles + j)),
                      pl.BlockSpec((1, tn), lambda bi, j, i: (0, col * d_tiles + j))]
        return specs

    u_ops = [u, u, u, conv_w, conv_b] if conv_u else [u]
    return pl.pallas_call(
        functools.partial(_dft_inv_kernel, conv_u, per_half),
        grid=(b, d_tiles, 2 * per_half),
        in_specs=[g_spec, g_spec,
                  pl.BlockSpec((1, 1, 2, m, tn), lambda bi, j, i: (bi, i // per_half, 0, 0, j))]
        + tile_specs(u_col, conv_u) + tile_specs(gate_col, True)
        + [pl.BlockSpec((1, tn), lambda bi, j, i: (0, j))],
        out_specs=pl.BlockSpec((1, tm, tn), lambda bi, j, i: (bi, i, j)),
        out_shape=jax.ShapeDtypeStruct((b, 2 * m, d), F32),
        compiler_params=_params("parallel", "parallel", "parallel"),
        name="dft_inv",
    )(g_re, g_im, y, *u_ops, gate, gate, gate, conv_w, conv_b, skip.reshape(1, d))


def _rope_tables(l):
    rows = l // GRID_W
    row = jnp.repeat(jnp.arange(rows), GRID_W).astype(F32)
    col = jnp.tile(jnp.arange(GRID_W), rows).astype(F32)
    half = HEAD_DIM // 2
    inv_freq = ROPE_THETA ** (-jnp.arange(0, half, 2, dtype=F32) / half)
    ang = jnp.concatenate([row[:, None] * inv_freq, col[:, None] * inv_freq], axis=-1)
    cos, sin = jnp.cos(ang), jnp.sin(ang)
    return jnp.concatenate([cos, cos], axis=-1), jnp.concatenate([sin, sin], axis=-1)


def _hyena_tables(l, d):
    t = jnp.linspace(0.0, 1.0, l, dtype=F32)[:, None]
    omega = (2.0 * math.pi / l) * jnp.arange(l, dtype=F32)[:, None]
    bands = jnp.linspace(1e-4, HYENA_BANDS - 1, HYENA_BANDS, dtype=F32)[None, :]
    feats = jnp.concatenate([t, jnp.cos(bands * omega), -jnp.sin(bands * omega)], axis=-1)
    min_decay = math.log(HYENA_TARGET) / HYENA_SLOW_DECAY
    max_decay = math.log(HYENA_TARGET) / HYENA_FAST_DECAY
    deltas = jnp.abs(jnp.linspace(min_decay, max_decay, d, dtype=F32))
    window = jnp.exp(-t * deltas)
    return feats, window


def _from_scan_layout(y):
    return y.transpose(0, 2, 1).reshape(y.shape[0], -1)


def kernel(x, c, mix_w_in, mix_w_out, attn_q_norm, attn_k_norm, rwkv_mu, rwkv_w0, rwkv_w_up, rwkv_a0, rwkv_a_up, rwkv_g_up, rwkv_k_k, rwkv_k_a, rwkv_r_k, rwkv_ln_g, rwkv_ln_b, hy_w_in, hy_conv_w, hy_conv_b, hy_f_w1, hy_f_b1, hy_f_w2, hy_f_b2, hy_f_w3, hy_f_b3, hy_sin_freq, hy_f_out, hy_skip, hy_w_out, ada_w, ada_b, norm_mix, norm_ffn, ffn_w1, ffn_w3, ffn_w2, final_norm):
    b, l, d = x.shape
    depth = ada_w.shape[0]
    tm = min(256, l)
    tp = min(FFN_ROWS, l)
    c_w = RWKV_WIDTH

    mod4 = ada_mod(c, ada_w, ada_b).reshape(depth, b, 1, 6 * d)

    perm = jnp.concatenate([jnp.arange(0, HEAD_DIM, 2), jnp.arange(1, HEAD_DIM, 2)])
    w_in = mix_w_in[0]
    w_q = w_in[:, :ATTN_WIDTH].reshape(d, ATTN_HEADS, HEAD_DIM)[:, :, perm].reshape(d, ATTN_WIDTH)
    w_kv = w_in[:, ATTN_WIDTH:ATTN_WIDTH + 2 * KV_WIDTH]
    w_k = w_kv[:, :KV_WIDTH].reshape(d, ATTN_KV_HEADS, HEAD_DIM)[:, :, perm].reshape(d, KV_WIDTH)
    w_kv = jnp.concatenate([w_k, w_kv[:, KV_WIDTH:]], axis=1)
    w_rw = jnp.pad(w_in[:, ATTN_WIDTH + 2 * KV_WIDTH:], ((0, 0), (0, RWKV_IN_PAD - RWKV_IN)))
    q, kv, rw = proj(x, mod4, 0, norm_mix[0], [w_q.astype(BF16), w_kv.astype(BF16), w_rw.astype(BF16)], tp)

    seg = jnp.kron(jnp.eye(RWKV_HEADS, dtype=BF16), jnp.ones((RWKV_HEAD, RWKV_HEAD), BF16))
    r, z, v, w, kd, bb, bonus, g = rwkv_prep(
        rw, jnp.pad(rwkv_mu[0], (0, RWKV_IN_PAD - RWKV_IN)).reshape(1, RWKV_IN_PAD),
        rwkv_w0[0].reshape(2, 1, c_w), rwkv_w_up[0], rwkv_a0[0].reshape(2, 1, c_w), rwkv_a_up[0],
        jnp.pad(rwkv_g_up[0], ((0, RWKV_IN_PAD - RWKV_IN), (0, 0))),
        rwkv_k_k[0].reshape(1, c_w), rwkv_k_a[0].reshape(1, c_w), rwkv_r_k[0].reshape(1, c_w), seg, tm)
    stage1 = [r, w, kd, v, z, bb]
    cos, sin = _rope_tables(l)
    qn, kn = attn_q_norm[0][perm].reshape(1, HEAD_DIM), attn_k_norm[0][perm].reshape(1, HEAD_DIM)
    half = b // 2
    y_attn0 = attn(q, kv, cos, sin, qn, kn, tm, 0, half)
    tn = min(512, d)
    feats, window = _hyena_tables(l, d)
    k_pad = 128
    feats = jnp.pad(feats, ((0, 0), (0, k_pad - feats.shape[1])))
    f_w1 = jnp.pad(hy_f_w1[0], ((0, k_pad - hy_f_w1.shape[1]), (0, 0)))
    taps = hy_filter(feats, f_w1, hy_f_b1[0], hy_f_w2[0], hy_f_b2[0], hy_f_w3[0], hy_f_b3[0],
                     hy_sin_freq[0], hy_f_out[0], window, tn)
    scan_operands = [a.reshape(-1, RWKV_HEAD, l).transpose(2, 1, 0) for a in stage1]
    y_attn1 = attn(q, kv, cos, sin, qn, kn, tm, half, b - half)
    fwd, inv = _dft_matrices(l // 2)
    ft = min(FREQ_TILE, l // 2)
    spec = dft_spec(fwd, taps, d, ft, tn)

    y_f, y_b = rwkv_scan(scan_operands, min(SCAN_BLOCK, l), after=(y_attn0, y_attn1, spec))
    fwd_lane = (jnp.arange(y_f.shape[2]) // RWKV_HEADS) % 2 == 0
    y_t = _from_scan_layout(jnp.where(fwd_lane[None, None, :], y_f, y_b))
    x = mix_out([y_attn0, y_attn1, (y_t, 0), (y_t, 1), bonus, g],
                [rwkv_ln_g[0].reshape(1, c_w), rwkv_ln_b[0].reshape(1, c_w), seg],
                x, mod4, 0, mix_w_out[0].astype(BF16), tm, True)
    x = ffn(x, mod4, 0, norm_ffn[0], ffn_w1[0].astype(BF16), ffn_w3[0].astype(BF16), ffn_w2[0].astype(BF16),
            final_norm, min(FFN_ROWS, l), False)

    (p3,) = proj(x, mod4, 1, norm_mix[1], [hy_w_in[0].astype(BF16)], tp)
    conv_w, conv_b = hy_conv_w[0], hy_conv_b[0].reshape(1, 3 * d)
    y1 = dft_mul(fwd, p3, 0, spec, 0, d, ft, tn, conv=(conv_w, conv_b))
    z1 = dft_inv(inv, y1, p3, 0, p3, 1, hy_skip[0, 0], ft, tn, conv_w, conv_b, True)
    y2 = dft_mul(fwd, z1, 0, spec, 1, d, ft, tn)
    z2 = dft_inv(inv, y2, z1, 0, p3, 2, hy_skip[0, 1], ft, tn, conv_w, conv_b, False)
    x = mix_out([z2], [], x, mod4, 1, hy_w_out[0].astype(BF16), tp, False)
    x = ffn(x, mod4, 1, norm_ffn[1], ffn_w1[1].astype(BF16), ffn_w3[1].astype(BF16), ffn_w2[1].astype(BF16),
            final_norm, min(FFN_ROWS, l), True)
    return x
```

```python
import functools
import math

import jax
import jax.numpy as jnp
from jax import lax
from jax.experimental import pallas as pl
from jax.experimental.pallas import tpu as pltpu

F32 = jnp.float32
BF16 = jnp.bfloat16
HIGHEST = lax.Precision.HIGHEST

GRID_W = 64
HEAD_DIM = 64
ATTN_HEADS = 8
ATTN_KV_HEADS = 2
ATTN_WIDTH = ATTN_HEADS * HEAD_DIM
KV_WIDTH = ATTN_KV_HEADS * HEAD_DIM
ROPE_THETA = 10000.0
RWKV_HEADS = 8
RWKV_HEAD = 64
RWKV_WIDTH = RWKV_HEADS * RWKV_HEAD
DECAY_LORA = 64
AAA_LORA = 64
GATE_LORA = 160
RWKV_IN = 3 * RWKV_WIDTH + 2 * DECAY_LORA + 2 * AAA_LORA + GATE_LORA
RWKV_IN_PAD = 2048
GN_EPS = 64e-5
HYENA_BANDS = 16
HYENA_TARGET = 1e-2
HYENA_FAST_DECAY = 0.3
HYENA_SLOW_DECAY = 1.5
NORM_EPS = 1e-6

V7X_VMEM_BYTES = 64 * 1024 * 1024
VMEM_LIMIT = V7X_VMEM_BYTES - 8 * 1024 * 1024
FREQ_TILE = 512
SCAN_BLOCK = 32
FFN_ROWS = 512
SUBLANES = 8
MXU_COLS = 256


def _params(*sem):
    return pltpu.CompilerParams(dimension_semantics=sem, vmem_limit_bytes=VMEM_LIMIT)


def _const_spec(shape):
    zeros = (0,) * len(shape)
    return pl.BlockSpec(shape, lambda *_: zeros, pipeline_mode=pl.Buffered(1))


def _dot(a, b):
    return jnp.dot(a, b, preferred_element_type=F32)


def _split(x):
    hi = x.astype(BF16)
    return hi, (x - hi.astype(F32)).astype(BF16)


def _dot_split(a, b):
    ah, al = _split(a)
    bh, bl = _split(b)
    return _dot(ah, bh) + (_dot(ah, bl) + _dot(al, bh))


def _seg_sum(a, seg):
    ah, al = _split(a)
    return _dot(ah, seg) + _dot(al, seg)


def _dot_hi(a, b):
    return jnp.dot(a, b, precision=HIGHEST, preferred_element_type=F32)


def _rms_mod(x, gain, scale, shift):
    ms = jnp.mean(x * x, axis=-1, keepdims=True)
    return (x * lax.rsqrt(ms + NORM_EPS) * gain) * (1.0 + scale) + shift


def _ada_kernel(c_ref, w_ref, b_ref, o_ref):
    c = c_ref[...]
    cond = c * jax.nn.sigmoid(c)
    o_ref[...] = _dot_hi(cond, w_ref[...]) + b_ref[...]


def ada_mod(c, ada_w, ada_b):
    depth, d, n = ada_w.shape
    b = c.shape[0]
    tn = 1536
    return pl.pallas_call(
        _ada_kernel,
        grid=(depth, n // tn),
        in_specs=[pl.BlockSpec((b, d), lambda l, j: (0, 0)),
                  pl.BlockSpec((None, d, tn), lambda l, j: (l, 0, j)),
                  pl.BlockSpec((None, 1, tn), lambda l, j: (l, 0, j))],
        out_specs=pl.BlockSpec((None, b, tn), lambda l, j: (l, 0, j)),
        out_shape=jax.ShapeDtypeStruct((depth, b, n), F32),
        compiler_params=_params("parallel", "parallel"),
        name="ada_mod",
    )(c, ada_w, ada_b.reshape(depth, 1, n))


def _mod_spec(d, layer, chunk):
    return pl.BlockSpec((None, None, 1, d), lambda b, *_: (layer, b, 0, chunk))


def _proj_kernel(n_out, x_ref, sh_ref, sc_ref, g_ref, *refs):
    h = _rms_mod(x_ref[0], g_ref[...], sc_ref[...], sh_ref[...]).astype(BF16)
    for w_ref, o_ref in zip(refs[:n_out], refs[n_out:]):
        o_ref[0] = _dot(h, w_ref[...])


def proj(x, mod4, layer, gain, weights, tm):
    b, l, d = x.shape
    n_out = len(weights)
    return pl.pallas_call(
        functools.partial(_proj_kernel, n_out),
        grid=(b, l // tm),
        in_specs=[pl.BlockSpec((1, tm, d), lambda bi, i: (bi, i, 0)),
                  _mod_spec(d, layer, 0), _mod_spec(d, layer, 1),
                  _const_spec((1, d))] + [_const_spec(w.shape) for w in weights],
        out_specs=[pl.BlockSpec((1, tm, w.shape[1]), lambda bi, i: (bi, i, 0)) for w in weights],
        out_shape=[jax.ShapeDtypeStruct((b, l, w.shape[1]), F32) for w in weights],
        compiler_params=_params("parallel", "parallel"),
        name="proj",
    )(x, mod4, mod4, gain.reshape(1, d), *weights)


def _headnorm_rope(x, gain, cos, sin):
    ms = jnp.mean(x * x, axis=-1, keepdims=True)
    y = x * lax.rsqrt(ms + NORM_EPS) * gain
    half = HEAD_DIM // 2
    rot = jnp.concatenate([-y[:, half:], y[:, :half]], axis=-1)
    return y * cos + rot * sin


def _attn_kernel(q_ref, kv_ref, cq_ref, sq_ref, ck_ref, sk_ref, qn_ref, kn_ref, *refs):
    o_ref, kt_s, v_s = refs[-3:]
    @pl.when(pl.program_id(1) == 0)
    def _():
        kv = kv_ref[0]
        ones_col = (lax.broadcasted_iota(jnp.int32, (kv.shape[0], HEAD_DIM), 1) == 0).astype(F32)
        for hk in range(ATTN_KV_HEADS):
            k = kv[:, HEAD_DIM * hk:HEAD_DIM * (hk + 1)]
            kt_s[hk] = _headnorm_rope(k, kn_ref[...], ck_ref[...], sk_ref[...]).T.astype(BF16)
            v = kv[:, KV_WIDTH + HEAD_DIM * hk:KV_WIDTH + HEAD_DIM * (hk + 1)]
            v_s[hk] = jnp.concatenate([v, ones_col], axis=-1).astype(BF16)

    q = q_ref[0]
    group = ATTN_HEADS // ATTN_KV_HEADS
    outs = []
    for h in range(ATTN_HEADS):
        qh = _headnorm_rope(q[:, HEAD_DIM * h:HEAD_DIM * (h + 1)], qn_ref[...], cq_ref[...], sq_ref[...])
        qh = (qh * (HEAD_DIM ** -0.5 * math.log2(math.e))).astype(BF16)
        s = _dot(qh, kt_s[h // group])
        p = jnp.exp2(s - jnp.max(s, axis=-1, keepdims=True)).astype(BF16)
        o = _dot(p, v_s[h // group])
        outs.append(o[:, :HEAD_DIM] / o[:, HEAD_DIM:HEAD_DIM + 1])
    o_ref[0] = jnp.concatenate(outs, axis=-1)


def _anchor_specs(after):
    return [pl.BlockSpec((1,) * (a.ndim - 2) + (SUBLANES, 128), lambda *_, nd=a.ndim: (0,) * nd) for a in after]


def attn(q, kv, cos, sin, qn, kn, tq, b0, nb, after=()):
    _, l, _ = q.shape
    b = nb
    return pl.pallas_call(
        _attn_kernel,
        grid=(nb, l // tq),
        in_specs=[pl.BlockSpec((1, tq, ATTN_WIDTH), lambda bi, i: (b0 + bi, i, 0)),
                  pl.BlockSpec((1, l, 2 * KV_WIDTH), lambda bi, i: (b0 + bi, 0, 0)),
                  pl.BlockSpec((tq, HEAD_DIM), lambda bi, i: (i, 0)),
                  pl.BlockSpec((tq, HEAD_DIM), lambda bi, i: (i, 0)),
                  _const_spec((l, HEAD_DIM)), _const_spec((l, HEAD_DIM)),
                  _const_spec((1, HEAD_DIM)), _const_spec((1, HEAD_DIM))] + _anchor_specs(after),
        out_specs=pl.BlockSpec((1, tq, ATTN_WIDTH), lambda bi, i: (bi, i, 0)),
        out_shape=jax.ShapeDtypeStruct((b, l, ATTN_WIDTH), F32),
        scratch_shapes=[pltpu.VMEM((ATTN_KV_HEADS, HEAD_DIM, l), BF16),
                        pltpu.VMEM((ATTN_KV_HEADS, l, 2 * HEAD_DIM), BF16)],
        compiler_params=_params("parallel", "arbitrary"),
        name="attn",
    )(q, kv, cos, sin, cos, sin, qn, kn, *after)


def _halo_specs(tt, w, l):
    nb = l // SUBLANES
    per = tt // SUBLANES
    main = pl.BlockSpec((1, tt, w), lambda bi, i: (bi, i, 0))
    prev = pl.BlockSpec((1, SUBLANES, w), lambda bi, i: (bi, jnp.maximum(i * per - 1, 0), 0))
    nxt = pl.BlockSpec((1, SUBLANES, w), lambda bi, i: (bi, jnp.minimum((i + 1) * per, nb - 1), 0))
    return [main, prev, nxt]


def _neighbours(cur, prow, nrow):
    tt = cur.shape[0]
    row = lax.broadcasted_iota(jnp.int32, cur.shape, 0)
    prev = jnp.where(row == 0, prow, pltpu.roll(cur, 1, 0))
    nxt = jnp.where(row == tt - 1, nrow, pltpu.roll(cur, tt - 1, 0))
    return prev, nxt


def _halo_rows(xp_ref, xn_ref, axis):
    i = pl.program_id(axis)
    last = pl.num_programs(axis) - 1
    prow = jnp.where(i > 0, xp_ref[0, SUBLANES - 1:SUBLANES, :], 0.0)
    nrow = jnp.where(i < last, xn_ref[0, 0:1, :], 0.0)
    return prow, nrow


def _rwkv_prep_kernel(x_ref, xp_ref, xn_ref, mu_ref, w0_ref, wup_ref, a0_ref, aup_ref, gup_ref,
                      kk_ref, ka_ref, rk_ref, seg_ref,
                      r_o, z_o, v_o, w_o, kd_o, b_o, bonus_o, g_o):
    cur = x_ref[0]
    prev, nxt = _neighbours(cur, *_halo_rows(xp_ref, xn_ref, 1))
    ps = cur + (0.5 * (prev + nxt) - cur) * mu_ref[...]
    c = RWKV_WIDTH
    r = ps[:, 0:c]
    k = ps[:, c:2 * c]
    v = ps[:, 2 * c:3 * c]
    o_a = 3 * c + 2 * DECAY_LORA
    o_g = o_a + 2 * AAA_LORA
    kk = k * kk_ref[...]
    kk = kk * lax.rsqrt(jnp.maximum(_seg_sum(kk * kk, seg_ref[...]), 1e-24))
    r_t, v_t, z_t = r.T, v.T, (-kk).T
    for d in range(2):
        rows = slice(c * d, c * (d + 1))
        r_o[rows, :] = r_t
        v_o[rows, :] = v_t
        z_o[rows, :] = z_t
        w_lo = ps[:, 3 * c + DECAY_LORA * d:3 * c + DECAY_LORA * (d + 1)]
        a_lo = ps[:, o_a + AAA_LORA * d:o_a + AAA_LORA * (d + 1)]
        decay = jnp.exp(-math.exp(-0.5) * jax.nn.sigmoid(w0_ref[d] + _dot_split(jnp.tanh(w_lo), wup_ref[d])))
        a = jax.nn.sigmoid(a0_ref[d] + _dot_split(a_lo, aup_ref[d]))
        w_o[rows, :] = decay.T
        kd_o[rows, :] = (k * (1.0 + (a - 1.0) * ka_ref[...])).T
        b_o[rows, :] = (kk * a).T
    bonus_o[0] = _seg_sum(r * k * rk_ref[...], seg_ref[...]) * v
    g_o[0] = _dot_split(jax.nn.sigmoid(ps[:, o_g:RWKV_IN_PAD]), gup_ref[...])


def rwkv_prep(rw, mu, w0, w_up, a0, a_up, g_up, k_k, k_a, r_k, seg, tt):
    b, l, w = rw.shape
    c = RWKV_WIDTH
    natural = jax.ShapeDtypeStruct((b, l, c), F32)
    time_major = jax.ShapeDtypeStruct((b * 2 * c, l), F32)
    spec_n = pl.BlockSpec((1, tt, c), lambda bi, i: (bi, i, 0))
    spec_t = pl.BlockSpec((2 * c, tt), lambda bi, i: (bi, i))
    consts = [mu, w0, w_up, a0, a_up, g_up, k_k, k_a, r_k, seg]
    return pl.pallas_call(
        _rwkv_prep_kernel,
        grid=(b, l // tt),
        in_specs=_halo_specs(tt, w, l) + [_const_spec(a.shape) for a in consts],
        out_specs=[spec_t] * 6 + [spec_n] * 2,
        out_shape=[time_major] * 6 + [natural] * 2,
        compiler_params=_params("parallel", "parallel"),
        name="rwkv_prep",
    )(rw, rw, rw, *consts)


SCAN_ROWS = 64
_R, _W, _KD, _V, _Z, _B = range(6)


def _rwkv_scan_kernel(*refs):
    ins, (yf_ref, yb_ref, s_ref, m_ref, sz_ref) = refs[:12], refs[-5:]
    tb, n, lanes = ins[0].shape
    fwd_lane = (lax.broadcasted_iota(jnp.int32, (n, lanes), 1) // RWKV_HEADS) % 2 == 0

    def stage(which, t):
        m_ref[which] = jnp.where(fwd_lane, ins[2 * which][t], ins[2 * which + 1][tb - 1 - t])

    @pl.when(pl.program_id(0) == 0)
    def _():
        s_ref[...] = jnp.zeros_like(s_ref)

    halves = [slice(h * SCAN_ROWS, (h + 1) * SCAN_ROWS) for h in range(n // SCAN_ROWS)]

    stage(_Z, 0)
    for rows in halves:
        acc = s_ref[0, rows, :] * m_ref[_Z, 0:1, :]
        for k in range(1, n):
            acc = acc + s_ref[k, rows, :] * m_ref[_Z, k:k + 1, :]
        sz_ref[rows, :] = acc

    def step(t, carry):
        for which in (_R, _W, _KD, _V, _B):
            stage(which, t)
        stage(_Z, jnp.minimum(t + 1, tb - 1))
        for rows in halves:
            sz = sz_ref[rows, :]
            vt = m_ref[_V, rows, :]
            y = None
            sz_next = None
            for k in range(n):
                sk = (s_ref[k, rows, :] * m_ref[_W, k:k + 1, :] + sz * m_ref[_B, k:k + 1, :]
                      + vt * m_ref[_KD, k:k + 1, :])
                s_ref[k, rows, :] = sk
                yk = sk * m_ref[_R, k:k + 1, :]
                zk = sk * m_ref[_Z, k:k + 1, :]
                y = yk if y is None else y + yk
                sz_next = zk if sz_next is None else sz_next + zk
            sz_ref[rows, :] = sz_next
            yf_ref[t, rows, :] = y
            yb_ref[tb - 1 - t, rows, :] = y
        return carry

    lax.fori_loop(0, tb, step, 0)


def rwkv_scan(arrays, tb, after=()):
    l, n, lanes = arrays[0].shape
    nt = l // tb
    fwd = pl.BlockSpec((tb, n, lanes), lambda i: (i, 0, 0))
    rev = pl.BlockSpec((tb, n, lanes), lambda i: (nt - 1 - i, 0, 0))
    out = jax.ShapeDtypeStruct((l, n, lanes), F32)
    return pl.pallas_call(
        _rwkv_scan_kernel,
        grid=(nt,),
        in_specs=[fwd, rev] * 6 + _anchor_specs(after),
        out_specs=[fwd, rev],
        out_shape=[out, out],
        scratch_shapes=[pltpu.VMEM((n, n, lanes), F32), pltpu.VMEM((6, n, lanes), F32),
                        pltpu.VMEM((n, lanes), F32)],
        compiler_params=_params("arbitrary"),
        name="rwkv_scan",
    )(*[a for a in arrays for _ in range(2)], *after)


def _mix_out_kernel(rwkv, *refs):
    if rwkv:
        (ya0_ref, ya1_ref, yf_ref, yb_ref, bonus_ref, g_ref, lng_ref, lnb_ref, seg_ref,
         x_ref, gate_ref, w_ref, o_ref) = refs
        first_half = pl.program_id(0) < pl.num_programs(0) // 2
        y_attn = jnp.where(first_half, ya0_ref[0], ya1_ref[0])
        y = yf_ref[...] + yb_ref[...]
        inv_n = 1.0 / RWKV_HEAD
        mean = _seg_sum(y, seg_ref[...]) * inv_n
        cen = y - mean
        var = _seg_sum(cen * cen, seg_ref[...]) * inv_n
        yn = cen * lax.rsqrt(var + GN_EPS) * lng_ref[...] + lnb_ref[...]
        yr = (yn + bonus_ref[0]) * g_ref[0]
        a = jnp.concatenate([y_attn, yr], axis=-1)
    else:
        a_ref, x_ref, gate_ref, w_ref, o_ref = refs
        a = a_ref[0]
    o_ref[0] = x_ref[0] + gate_ref[...] * _dot(a.astype(BF16), w_ref[...])


def mix_out(acts, consts, x, mod4, layer, w, tm, rwkv):
    b, l, d = x.shape
    in_specs = []
    for i_act, a in enumerate(acts):
        if not isinstance(a, tuple) and a.shape[0] == b // 2:
            in_specs.append(pl.BlockSpec((1, tm, a.shape[2]), lambda bi, i: (bi % (b // 2), i, 0)))
        elif isinstance(a, tuple):
            arr, direction = a
            width = arr.shape[1] // (2 * b)
            in_specs.append(pl.BlockSpec((tm, width), lambda bi, i, dr=direction: (i, 2 * bi + dr)))
            acts[i_act] = arr
        else:
            in_specs.append(pl.BlockSpec((1, tm, a.shape[2]), lambda bi, i: (bi, i, 0)))
    in_specs += [_const_spec(a.shape) for a in consts]
    in_specs += [pl.BlockSpec((1, tm, d), lambda bi, i: (bi, i, 0)), _mod_spec(d, layer, 2), _const_spec(w.shape)]
    return pl.pallas_call(
        functools.partial(_mix_out_kernel, rwkv),
        grid=(b, l // tm),
        in_specs=in_specs,
        out_specs=pl.BlockSpec((1, tm, d), lambda bi, i: (bi, i, 0)),
        out_shape=jax.ShapeDtypeStruct((b, l, d), F32),
        compiler_params=_params("parallel", "parallel"),
        name="mix_out",
    )(*acts, *consts, x, mod4, w)


def _ffn_kernel(final, x_ref, sh_ref, sc_ref, gate_ref, g_ref, w1_ref, w3_ref, w2_ref, fg_ref, o_ref):
    x = x_ref[0]
    h = _rms_mod(x, g_ref[...], sc_ref[...], sh_ref[...]).astype(BF16)
    a = _dot(h, w1_ref[...])
    u = (a * jax.nn.sigmoid(a)) * _dot(h, w3_ref[...])
    out = x + gate_ref[...] * _dot(u.astype(BF16), w2_ref[...])
    if final:
        ms = jnp.mean(out * out, axis=-1, keepdims=True)
        out = out * lax.rsqrt(ms + NORM_EPS) * fg_ref[...]
    o_ref[0] = out


def ffn(x, mod4, layer, gain, w1, w3, w2, final_gain, tm, final):
    b, l, d = x.shape
    return pl.pallas_call(
        functools.partial(_ffn_kernel, final),
        grid=(b, l // tm),
        in_specs=[pl.BlockSpec((1, tm, d), lambda bi, i: (bi, i, 0)),
                  _mod_spec(d, layer, 3), _mod_spec(d, layer, 4), _mod_spec(d, layer, 5),
                  _const_spec((1, d)), _const_spec(w1.shape), _const_spec(w3.shape), _const_spec(w2.shape),
                  _const_spec((1, d))],
        out_specs=pl.BlockSpec((1, tm, d), lambda bi, i: (bi, i, 0)),
        out_shape=jax.ShapeDtypeStruct((b, l, d), F32),
        compiler_params=_params("parallel", "parallel"),
        name="ffn",
    )(x, mod4, mod4, mod4, gain.reshape(1, d), w1, w3, w2, final_gain.reshape(1, d))


def _conv3(cur, prow, nrow, w_ref, b_ref):
    prev, nxt = _neighbours(cur, prow, nrow)
    return prev * w_ref[0:1, :] + cur * w_ref[1:2, :] + nxt * w_ref[2:3, :] + b_ref[...]


def _hy_filter_kernel(d_tiles, feats_ref, w1_ref, b1_ref, w2_ref, b2_ref, w3_ref, b3_ref, fr_ref,
                      wout_ref, win_ref, o_ref, hid_s):
    j = pl.program_id(0)

    @pl.when(j == 0)
    def _():
        fr = fr_ref[...]
        hid = jnp.sin(fr * (_dot_hi(feats_ref[...], w1_ref[...]) + b1_ref[...]))
        hid = jnp.sin(fr * (_dot_hi(hid, w2_ref[...]) + b2_ref[...]))
        hid_s[...] = jnp.sin(fr * (_dot_hi(hid, w3_ref[...]) + b3_ref[...]))

    filt = _dot_hi(hid_s[...], wout_ref[...]) * win_ref[...]
    backward = (j // d_tiles) % 2 == 1
    row = lax.broadcasted_iota(jnp.int32, filt.shape, 0)
    o_ref[...] = jnp.where(jnp.logical_and(backward, row == 0), 0.0, filt)


def hy_filter(feats, w1, b1, w2, b2, w3, b3, fr, w_out, window, tn):
    l = feats.shape[0]
    hf = w2.shape[0]
    n = w_out.shape[1]
    d = window.shape[1]
    d_tiles = d // tn
    consts = [feats, w1, b1.reshape(1, hf), w2, b2.reshape(1, hf), w3, b3.reshape(1, hf), fr.reshape(1, hf)]
    return pl.pallas_call(
        functools.partial(_hy_filter_kernel, d_tiles),
        grid=(n // tn,),
        in_specs=[_const_spec(a.shape) for a in consts]
        + [pl.BlockSpec((hf, tn), lambda j: (0, j)), pl.BlockSpec((l, tn), lambda j: (0, j % d_tiles))],
        out_specs=pl.BlockSpec((l, tn), lambda j: (0, j)),
        out_shape=jax.ShapeDtypeStruct((l, n), F32),
        scratch_shapes=[pltpu.VMEM((l, hf), F32)],
        compiler_params=_params("arbitrary"),
        name="hy_filter",
    )(*consts, w_out, window)


def _dft_matrices(l):
    n = 2 * l
    lane = min(128, l)
    k = jnp.arange(l, dtype=jnp.int32)[:, None]
    t = jnp.arange(l, dtype=jnp.int32)
    a = ((k * jnp.arange(lane, dtype=jnp.int32)[None, :]) % n).astype(F32) * (2.0 * math.pi / n)
    c = ((k * (lane * jnp.arange(l // lane, dtype=jnp.int32))[None, :]) % n).astype(F32) * (2.0 * math.pi / n)
    ca, sa, cc, sc = jnp.cos(a)[:, None, :], jnp.sin(a)[:, None, :], jnp.cos(c)[:, :, None], jnp.sin(c)[:, :, None]
    cos = (ca * cc - sa * sc).reshape(l, l)
    sin = (sa * cc + ca * sc).reshape(l, l)
    alt = jnp.where(t % 2 == 0, 1.0, -1.0).astype(F32)
    f_re = cos.astype(BF16)
    f_im = jnp.where(k == 0, alt[None, :], -sin).astype(BF16)
    col0 = (t == 0)[None, :]
    g_re = jnp.where(col0, 1.0 / n, (2.0 / n) * cos).astype(BF16)
    g_im = jnp.where(col0, alt[:, None] / n, (-2.0 / n) * sin).astype(BF16)
    return (f_re, f_im), (g_re, g_im)


def _nyquist_slot(i, shape):
    row = lax.broadcasted_iota(jnp.int32, shape, 0)
    return jnp.logical_and(i == 0, row == 0)


def _dft_spec_kernel(fre_ref, fim_ref, ft_ref, fb_ref, bt_ref, bb_ref, o_ref):
    fre, fim = fre_ref[...], fim_ref[...]
    ft, fb, bt, bb = (ref[...].astype(BF16) for ref in (ft_ref, fb_ref, bt_ref, bb_ref))
    af_r, af_i = _dot(fre, ft), _dot(fim, ft)
    bf_r, bf_i = _dot(fre, fb), _dot(fim, fb)
    ab_r, ab_i = _dot(fre, bt), _dot(fim, bt)
    bb_r, bb_i = _dot(fre, bb), _dot(fim, bb)
    nyq = _nyquist_slot(pl.program_id(1), af_r.shape)
    row = lax.broadcasted_iota(jnp.int32, af_r.shape, 0)
    sgn = jnp.where(row % 2 == 0, 1.0, -1.0)
    lag0 = ft_ref[0:1, :]
    o_ref[0, 0] = (af_r + ab_r).astype(BF16)
    o_ref[0, 1] = jnp.where(nyq, af_i + ab_i, af_i - ab_i).astype(BF16)
    o_ref[1, 0] = (bf_r + sgn * (af_r - lag0)).astype(BF16)
    o_ref[1, 1] = (bf_i + jnp.where(nyq, af_i - lag0, sgn * af_i)).astype(BF16)
    o_ref[2, 0] = (sgn * ab_r + bb_r).astype(BF16)
    o_ref[2, 1] = jnp.where(nyq, ab_i + bb_i, -(sgn * ab_i + bb_i)).astype(BF16)


def dft_spec(fwd, taps, d, ft, tn):
    f_re, f_im = fwd
    m = f_re.shape[0]
    d_tiles = d // tn
    orders = taps.shape[1] // (2 * d)
    f_spec = pl.BlockSpec((ft, m), lambda j, i: (i, 0))

    def tap_spec(half, direction):
        return pl.BlockSpec((m, tn), lambda j, i: (half, (j // d_tiles) * 2 * d_tiles + direction * d_tiles
                                                    + j % d_tiles))

    return pl.pallas_call(
        _dft_spec_kernel,
        grid=(orders * d_tiles, m // ft),
        in_specs=[f_spec, f_spec, tap_spec(0, 0), tap_spec(1, 0), tap_spec(0, 1), tap_spec(1, 1)],
        out_specs=pl.BlockSpec((3, 2, ft, tn), lambda j, i: (0, 0, i, j)),
        out_shape=jax.ShapeDtypeStruct((3, 2, m, orders * d), BF16),
        compiler_params=_params("parallel", "parallel"),
        name="dft_spec",
    )(f_re, f_im, taps, taps, taps, taps)


def _dft_mul_kernel(has_conv, fre_ref, fim_ref, u_ref, h_ref, *refs):
    o_ref, u_s = refs[-2:]
    i = pl.program_id(2)

    @pl.when(i == 0)
    def _():
        u = u_ref[0]
        if has_conv:
            u = _conv3(u, 0.0, 0.0, *refs[:2])
        u_s[...] = u.astype(BF16)

    m = u_s.shape[0] // 2
    ft = h_ref.shape[2]
    rows = pl.ds(pl.multiple_of(i * ft, ft), ft)
    fre, fim = fre_ref[rows, :], fim_ref[rows, :]
    head = 2 * SUBLANES
    for c0 in range(0, o_ref.shape[-1], MXU_COLS):
        cols = slice(c0, c0 + MXU_COLS)
        tr, ti = _dot(fre, u_s[0:m, cols]), _dot(fim, u_s[0:m, cols])
        br, bi = _dot(fre, u_s[m:, cols]), _dot(fim, u_s[m:, cols])
        h0r, h0i, hpr, hpi, hmr, hmi = (h_ref[a, p, :, cols].astype(F32) for a in range(3) for p in range(2))
        nyq = _nyquist_slot(i, (head, MXU_COLS))

        def emit(half, plane, general, packed):
            o_ref[0, half, plane, 0:head, cols] = jnp.where(nyq, packed, general[0:head]).astype(BF16)
            o_ref[0, half, plane, head:, cols] = general[head:].astype(BF16)

        def top(x):
            return x[0:head]

        emit(0, 0, h0r * tr - h0i * ti + hmr * br - hmi * bi, top(h0r) * top(tr) + top(hmr) * top(br))
        emit(0, 1, h0r * ti + h0i * tr + hmr * bi + hmi * br, top(h0i) * top(ti) + top(hmi) * top(bi))
        emit(1, 0, hpr * tr - hpi * ti + h0r * br - h0i * bi, top(hpr) * top(tr) + top(h0r) * top(br))
        emit(1, 1, hpr * ti + hpi * tr + h0r * bi + h0i * br, top(hpi) * top(ti) + top(h0i) * top(bi))


def dft_mul(fwd, u, u_col, spec, spec_col, d, ft, tn, conv=None):
    f_re, f_im = fwd
    m = f_re.shape[0]
    b = u.shape[0]
    d_tiles = d // tn
    f_spec = _const_spec((m, m))
    conv_specs = [] if conv is None else [pl.BlockSpec((3, tn), lambda bi, j, i: (0, u_col * d_tiles + j)),
                                          pl.BlockSpec((1, tn), lambda bi, j, i: (0, u_col * d_tiles + j))]
    return pl.pallas_call(
        functools.partial(_dft_mul_kernel, conv is not None),
        grid=(b, d_tiles, m // ft),
        in_specs=[f_spec, f_spec,
                  pl.BlockSpec((1, 2 * m, tn), lambda bi, j, i: (bi, 0, u_col * d_tiles + j)),
                  pl.BlockSpec((3, 2, ft, tn), lambda bi, j, i: (0, 0, i, spec_col * d_tiles + j))] + conv_specs,
        out_specs=pl.BlockSpec((1, 2, 2, ft, tn), lambda bi, j, i: (bi, 0, 0, i, j)),
        out_shape=jax.ShapeDtypeStruct((b, 2, 2, m, d), BF16),
        scratch_shapes=[pltpu.VMEM((2 * m, tn), BF16)],
        compiler_params=_params("parallel", "parallel", "arbitrary"),
        name="dft_mul",
    )(f_re, f_im, u, spec, *(conv or ()))


def _dft_inv_kernel(conv_u, per_half, gre_ref, gim_ref, y_ref, *refs):
    u_refs, (g_ref, gp_ref, gn_ref, gw_ref, gb_ref, skip_ref, o_ref) = refs[:-7], refs[-7:]
    u = u_refs[0][0]
    if conv_u:
        u = _conv3(u, *_halo_rows(u_refs[1], u_refs[2], 2), u_refs[3], u_refs[4])
    gate = _conv3(g_ref[0], *_halo_rows(gp_ref, gn_ref, 2), gw_ref, gb_ref)
    tm = o_ref.shape[1]
    rows = pl.ds(pl.multiple_of((pl.program_id(2) % per_half) * tm, tm), tm)
    gre, gim = gre_ref[rows, :], gim_ref[rows, :]
    for c0 in range(0, o_ref.shape[-1], MXU_COLS):
        cols = slice(c0, c0 + MXU_COLS)
        conv = _dot(gre, y_ref[0, 0, 0, :, cols]) + _dot(gim, y_ref[0, 0, 1, :, cols])
        o_ref[0, :, cols] = gate[:, cols] * (conv + u[:, cols] * skip_ref[:, cols])


def dft_inv(inv, y, u, u_col, gate, gate_col, skip, tm, tn, conv_w, conv_b, conv_u):
    g_re, g_im = inv
    m = g_re.shape[0]
    b, _, _, _, d = y.shape
    d_tiles = d // tn
    per_half = m // tm
    rows8 = tm // SUBLANES
    last8 = 2 * m // SUBLANES - 1
    g_spec = _const_spec((m, m))

    def tile_specs(col, halo):
        specs = [pl.BlockSpec((1, tm, tn), lambda bi, j, i: (bi, i, col * d_tiles + j))]
        if halo:
            specs += [pl.BlockSpec((1, SUBLANES, tn),
                                   lambda bi, j, i: (bi, jnp.maximum(i * rows8 - 1, 0), col * d_tiles + j)),
                      pl.BlockSpec((1, SUBLANES, tn),
                                   lambda bi, j, i: (bi, jnp.minimum((i + 1) * rows8, last8), col * d_tiles + j)),
                      pl.BlockSpec((3, tn), lambda bi, j, i: (0, col * d_tiles + j)),
                      pl.BlockSpec((1, tn), lambda bi, j, i: (0, col * d_tiles + j))]
        return specs

    u_ops = [u, u, u, conv_w, conv_b] if conv_u else [u]
    return pl.pallas_call(
        functools.partial(_dft_inv_kernel, conv_u, per_half),
        grid=(b, d_tiles, 2 * per_half),
        in_specs=[g_spec, g_spec,
                  pl.BlockSpec((1, 1, 2, m, tn), lambda bi, j, i: (bi, i // per_half, 0, 0, j))]
        + tile_specs(u_col, conv_u) + tile_specs(gate_col, True)
        + [pl.BlockSpec((1, tn), lambda bi, j, i: (0, j))],
        out_specs=pl.BlockSpec((1, tm, tn), lambda bi, j, i: (bi, i, j)),
        out_shape=jax.ShapeDtypeStruct((b, 2 * m, d), F32),
        compiler_params=_params("parallel", "parallel", "parallel"),
        name="dft_inv",
    )(g_re, g_im, y, *u_ops, gate, gate, gate, conv_w, conv_b, skip.reshape(1, d))


def _rope_tables(l):
    rows = l // GRID_W
    row = jnp.repeat(jnp.arange(rows), GRID_W).astype(F32)
    col = jnp.tile(jnp.arange(GRID_W), rows).astype(F32)
    half = HEAD_DIM // 2
    inv_freq = ROPE_THETA ** (-jnp.arange(0, half, 2, dtype=F32) / half)
    ang = jnp.concatenate([row[:, None] * inv_freq, col[:, None] * inv_freq], axis=-1)
    cos, sin = jnp.cos(ang), jnp.sin(ang)
    return jnp.concatenate([cos, cos], axis=-1), jnp.concatenate([sin, sin], axis=-1)


def _hyena_tables(l, d):
    t = jnp.linspace(0.0, 1.0, l, dtype=F32)[:, None]
    omega = (2.0 * math.pi / l) * jnp.arange(l, dtype=F32)[:, None]
    bands = jnp.linspace(1e-4, HYENA_BANDS - 1, HYENA_BANDS, dtype=F32)[None, :]
    feats = jnp.concatenate([t, jnp.cos(bands * omega), -jnp.sin(bands * omega)], axis=-1)
    min_decay = math.log(HYENA_TARGET) / HYENA_SLOW_DECAY
    max_decay = math.log(HYENA_TARGET) / HYENA_FAST_DECAY
    deltas = jnp.abs(jnp.linspace(min_decay, max_decay, d, dtype=F32))
    window = jnp.exp(-t * deltas)
    return feats, window


def _from_scan_layout(y):
    return y.transpose(0, 2, 1).reshape(y.shape[0], -1)


def kernel(x, c, mix_w_in, mix_w_out, attn_q_norm, attn_k_norm, rwkv_mu, rwkv_w0, rwkv_w_up, rwkv_a0, rwkv_a_up, rwkv_g_up, rwkv_k_k, rwkv_k_a, rwkv_r_k, rwkv_ln_g, rwkv_ln_b, hy_w_in, hy_conv_w, hy_conv_b, hy_f_w1, hy_f_b1, hy_f_w2, hy_f_b2, hy_f_w3, hy_f_b3, hy_sin_freq, hy_f_out, hy_skip, hy_w_out, ada_w, ada_b, norm_mix, norm_ffn, ffn_w1, ffn_w3, ffn_w2, final_norm):
    b, l, d = x.shape
    depth = ada_w.shape[0]
    tm = min(256, l)
    tp = min(FFN_ROWS, l)
    c_w = RWKV_WIDTH

    mod4 = ada_mod(c, ada_w, ada_b).reshape(depth, b, 1, 6 * d)

    perm = jnp.concatenate([jnp.arange(0, HEAD_DIM, 2), jnp.arange(1, HEAD_DIM, 2)])
    w_in = mix_w_in[0]
    w_q = w_in[:, :ATTN_WIDTH].reshape(d, ATTN_HEADS, HEAD_DIM)[:, :, perm].reshape(d, ATTN_WIDTH)
    w_kv = w_in[:, ATTN_WIDTH:ATTN_WIDTH + 2 * KV_WIDTH]
    w_k = w_kv[:, :KV_WIDTH].reshape(d, ATTN_KV_HEADS, HEAD_DIM)[:, :, perm].reshape(d, KV_WIDTH)
    w_kv = jnp.concatenate([w_k, w_kv[:, KV_WIDTH:]], axis=1)
    w_rw = jnp.pad(w_in[:, ATTN_WIDTH + 2 * KV_WIDTH:], ((0, 0), (0, RWKV_IN_PAD - RWKV_IN)))
    q, kv, rw = proj(x, mod4, 0, norm_mix[0], [w_q.astype(BF16), w_kv.astype(BF16), w_rw.astype(BF16)], tp)

    seg = jnp.kron(jnp.eye(RWKV_HEADS, dtype=BF16), jnp.ones((RWKV_HEAD, RWKV_HEAD), BF16))
    r, z, v, w, kd, bb, bonus, g = rwkv_prep(
        rw, jnp.pad(rwkv_mu[0], (0, RWKV_IN_PAD - RWKV_IN)).reshape(1, RWKV_IN_PAD),
        rwkv_w0[0].reshape(2, 1, c_w), rwkv_w_up[0], rwkv_a0[0].reshape(2, 1, c_w), rwkv_a_up[0],
        jnp.pad(rwkv_g_up[0], ((0, RWKV_IN_PAD - RWKV_IN), (0, 0))),
        rwkv_k_k[0].reshape(1, c_w), rwkv_k_a[0].reshape(1, c_w), rwkv_r_k[0].reshape(1, c_w), seg, tm)
    stage1 = [r, w, kd, v, z, bb]
    cos, sin = _rope_tables(l)
    qn, kn = attn_q_norm[0][perm].reshape(1, HEAD_DIM), attn_k_norm[0][perm].reshape(1, HEAD_DIM)
    half = b // 2
    y_attn0 = attn(q, kv, cos, sin, qn, kn, tm, 0, half)
    tn = min(512, d)
    feats, window = _hyena_tables(l, d)
    k_pad = 128
    feats = jnp.pad(feats, ((0, 0), (0, k_pad - feats.shape[1])))
    f_w1 = jnp.pad(hy_f_w1[0], ((0, k_pad - hy_f_w1.shape[1]), (0, 0)))
    taps = hy_filter(feats, f_w1, hy_f_b1[0], hy_f_w2[0], hy_f_b2[0], hy_f_w3[0], hy_f_b3[0],
                     hy_sin_freq[0], hy_f_out[0], window, tn)
    scan_operands = [a.reshape(-1, RWKV_HEAD, l).transpose(2, 1, 0) for a in stage1]
    y_attn1 = attn(q, kv, cos, sin, qn, kn, tm, half, b - half)
    fwd, inv = _dft_matrices(l // 2)
    ft = min(FREQ_TILE, l // 2)
    spec = dft_spec(fwd, taps, d, ft, tn)

    y_f, y_b = rwkv_scan(scan_operands, min(SCAN_BLOCK, l), after=(y_attn0, y_attn1, spec))
    fwd_lane = (jnp.arange(y_f.shape[2]) // RWKV_HEADS) % 2 == 0
    y_t = _from_scan_layout(jnp.where(fwd_lane[None, None, :], y_f, y_b))
    x = mix_out([y_attn0, y_attn1, (y_t, 0), (y_t, 1), bonus, g],
                [rwkv_ln_g[0].reshape(1, c_w), rwkv_ln_b[0].reshape(1, c_w), seg],
                x, mod4, 0, mix_w_out[0].astype(BF16), tp, True)
    x = ffn(x, mod4, 0, norm_ffn[0], ffn_w1[0].astype(BF16), ffn_w3[0].astype(BF16), ffn_w2[0].astype(BF16),
            final_norm, min(FFN_ROWS, l), False)

    (p3,) = proj(x, mod4, 1, norm_mix[1], [hy_w_in[0].astype(BF16)], tp)
    conv_w, conv_b = hy_conv_w[0], hy_conv_b[0].reshape(1, 3 * d)
    y1 = dft_mul(fwd, p3, 0, spec, 0, d, ft, tn, conv=(conv_w, conv_b))
    z1 = dft_inv(inv, y1, p3, 0, p3, 1, hy_skip[0, 0], ft, tn, conv_w, conv_b, True)
    y2 = dft_mul(fwd, z1, 0, spec, 1, d, ft, tn)
    z2 = dft_inv(inv, y2, z1, 0, p3, 2, hy_skip[0, 1], ft, tn, conv_w, conv_b, False)
    x = mix_out([z2], [], x, mod4, 1, hy_w_out[0].astype(BF16), tp, False)
    x = ffn(x, mod4, 1, norm_ffn[1], ffn_w1[1].astype(BF16), ffn_w3[1].astype(BF16), ffn_w2[1].astype(BF16),
            final_norm, min(FFN_ROWS, l), True)
    return x
```

```python
import functools
import math

import jax
import jax.numpy as jnp
from jax import lax
from jax.experimental import pallas as pl
from jax.experimental.pallas import tpu as pltpu

F32 = jnp.float32
BF16 = jnp.bfloat16
HIGHEST = lax.Precision.HIGHEST

GRID_W = 64
HEAD_DIM = 64
ATTN_HEADS = 8
ATTN_KV_HEADS = 2
ATTN_WIDTH = ATTN_HEADS * HEAD_DIM
KV_WIDTH = ATTN_KV_HEADS * HEAD_DIM
ROPE_THETA = 10000.0
RWKV_HEADS = 8
RWKV_HEAD = 64
RWKV_WIDTH = RWKV_HEADS * RWKV_HEAD
DECAY_LORA = 64
AAA_LORA = 64
GATE_LORA = 160
RWKV_IN = 3 * RWKV_WIDTH + 2 * DECAY_LORA + 2 * AAA_LORA + GATE_LORA
RWKV_IN_PAD = 2048
GN_EPS = 64e-5
HYENA_BANDS = 16
HYENA_TARGET = 1e-2
HYENA_FAST_DECAY = 0.3
HYENA_SLOW_DECAY = 1.5
NORM_EPS = 1e-6

V7X_VMEM_BYTES = 64 * 1024 * 1024
VMEM_LIMIT = V7X_VMEM_BYTES - 8 * 1024 * 1024
FREQ_TILE = 512
SCAN_BLOCK = 32
FFN_ROWS = 512
SUBLANES = 8
LANES = 128
MXU_COLS = 256
ROW_TILE = 256
COL_TILE = 512
ADA_COLS = 1536


def _params(*sem):
    return pltpu.CompilerParams(dimension_semantics=sem, vmem_limit_bytes=VMEM_LIMIT)


def _const_spec(shape):
    zeros = (0,) * len(shape)
    return pl.BlockSpec(shape, lambda *_: zeros, pipeline_mode=pl.Buffered(1))


def _dot(a, b):
    return jnp.dot(a, b, preferred_element_type=F32)


def _split(x):
    hi = x.astype(BF16)
    return hi, (x - hi.astype(F32)).astype(BF16)


def _dot_split(a, b):
    ah, al = _split(a)
    bh, bl = _split(b)
    return _dot(ah, bh) + (_dot(ah, bl) + _dot(al, bh))


def _seg_sum(a, seg):
    ah, al = _split(a)
    return _dot(ah, seg) + _dot(al, seg)


def _dot_hi(a, b):
    return jnp.dot(a, b, precision=HIGHEST, preferred_element_type=F32)


def _rms_mod(x, gain, scale, shift):
    ms = jnp.mean(x * x, axis=-1, keepdims=True)
    return (x * lax.rsqrt(ms + NORM_EPS) * gain) * (1.0 + scale) + shift


def _ada_kernel(c_ref, w_ref, b_ref, o_ref):
    c = c_ref[...]
    cond = c * jax.nn.sigmoid(c)
    o_ref[...] = _dot_hi(cond, w_ref[...]) + b_ref[...]


def ada_mod(c, ada_w, ada_b):
    depth, d, n = ada_w.shape
    b = c.shape[0]
    tn = min(ADA_COLS, n)
    return pl.pallas_call(
        _ada_kernel,
        grid=(depth, n // tn),
        in_specs=[pl.BlockSpec((b, d), lambda l, j: (0, 0)),
                  pl.BlockSpec((None, d, tn), lambda l, j: (l, 0, j)),
                  pl.BlockSpec((None, 1, tn), lambda l, j: (l, 0, j))],
        out_specs=pl.BlockSpec((None, b, tn), lambda l, j: (l, 0, j)),
        out_shape=jax.ShapeDtypeStruct((depth, b, n), F32),
        compiler_params=_params("parallel", "parallel"),
        name="ada_mod",
    )(c, ada_w, ada_b.reshape(depth, 1, n))


def _mod_spec(d, layer, chunk):
    return pl.BlockSpec((None, None, 1, d), lambda b, *_: (layer, b, 0, chunk))


def _proj_kernel(n_out, x_ref, sh_ref, sc_ref, g_ref, *refs):
    h = _rms_mod(x_ref[0], g_ref[...], sc_ref[...], sh_ref[...]).astype(BF16)
    for w_ref, o_ref in zip(refs[:n_out], refs[n_out:]):
        o_ref[0] = _dot(h, w_ref[...])


def proj(x, mod4, layer, gain, weights, tm):
    b, l, d = x.shape
    n_out = len(weights)
    return pl.pallas_call(
        functools.partial(_proj_kernel, n_out),
        grid=(b, l // tm),
        in_specs=[pl.BlockSpec((1, tm, d), lambda bi, i: (bi, i, 0)),
                  _mod_spec(d, layer, 0), _mod_spec(d, layer, 1),
                  _const_spec((1, d))] + [_const_spec(w.shape) for w in weights],
        out_specs=[pl.BlockSpec((1, tm, w.shape[1]), lambda bi, i: (bi, i, 0)) for w in weights],
        out_shape=[jax.ShapeDtypeStruct((b, l, w.shape[1]), F32) for w in weights],
        compiler_params=_params("parallel", "parallel"),
        name="proj",
    )(x, mod4, mod4, gain.reshape(1, d), *weights)


def _headnorm_rope(x, gain, cos, sin):
    ms = jnp.mean(x * x, axis=-1, keepdims=True)
    y = x * lax.rsqrt(ms + NORM_EPS) * gain
    half = HEAD_DIM // 2
    rot = jnp.concatenate([-y[:, half:], y[:, :half]], axis=-1)
    return y * cos + rot * sin


def _attn_kernel(q_ref, kv_ref, cq_ref, sq_ref, ck_ref, sk_ref, qn_ref, kn_ref, *refs):
    o_ref, kt_s, v_s = refs[-3:]
    @pl.when(pl.program_id(1) == 0)
    def _():
        kv = kv_ref[0]
        ones_col = (lax.broadcasted_iota(jnp.int32, (kv.shape[0], HEAD_DIM), 1) == 0).astype(F32)
        for hk in range(ATTN_KV_HEADS):
            k = kv[:, HEAD_DIM * hk:HEAD_DIM * (hk + 1)]
            kt_s[hk] = _headnorm_rope(k, kn_ref[...], ck_ref[...], sk_ref[...]).T.astype(BF16)
            v = kv[:, KV_WIDTH + HEAD_DIM * hk:KV_WIDTH + HEAD_DIM * (hk + 1)]
            v_s[hk] = jnp.concatenate([v, ones_col], axis=-1).astype(BF16)

    q = q_ref[0]
    group = ATTN_HEADS // ATTN_KV_HEADS
    outs = []
    for h in range(ATTN_HEADS):
        qh = _headnorm_rope(q[:, HEAD_DIM * h:HEAD_DIM * (h + 1)], qn_ref[...], cq_ref[...], sq_ref[...])
        qh = (qh * (HEAD_DIM ** -0.5 * math.log2(math.e))).astype(BF16)
        s = _dot(qh, kt_s[h // group])
        p = jnp.exp2(s - jnp.max(s, axis=-1, keepdims=True)).astype(BF16)
        o = _dot(p, v_s[h // group])
        outs.append(o[:, :HEAD_DIM] / o[:, HEAD_DIM:HEAD_DIM + 1])
    o_ref[0] = jnp.concatenate(outs, axis=-1)


def _anchor_specs(after):
    return [pl.BlockSpec((1,) * (a.ndim - 2) + (SUBLANES, LANES), lambda *_, nd=a.ndim: (0,) * nd) for a in after]


def attn(q, kv, cos, sin, qn, kn, tq, b0, nb, after=()):
    _, l, _ = q.shape
    b = nb
    return pl.pallas_call(
        _attn_kernel,
        grid=(nb, l // tq),
        in_specs=[pl.BlockSpec((1, tq, ATTN_WIDTH), lambda bi, i: (b0 + bi, i, 0)),
                  pl.BlockSpec((1, l, 2 * KV_WIDTH), lambda bi, i: (b0 + bi, 0, 0)),
                  pl.BlockSpec((tq, HEAD_DIM), lambda bi, i: (i, 0)),
                  pl.BlockSpec((tq, HEAD_DIM), lambda bi, i: (i, 0)),
                  _const_spec((l, HEAD_DIM)), _const_spec((l, HEAD_DIM)),
                  _const_spec((1, HEAD_DIM)), _const_spec((1, HEAD_DIM))] + _anchor_specs(after),
        out_specs=pl.BlockSpec((1, tq, ATTN_WIDTH), lambda bi, i: (bi, i, 0)),
        out_shape=jax.ShapeDtypeStruct((b, l, ATTN_WIDTH), F32),
        scratch_shapes=[pltpu.VMEM((ATTN_KV_HEADS, HEAD_DIM, l), BF16),
                        pltpu.VMEM((ATTN_KV_HEADS, l, 2 * HEAD_DIM), BF16)],
        compiler_params=_params("parallel", "arbitrary"),
        name="attn",
    )(q, kv, cos, sin, cos, sin, qn, kn, *after)


def _halo_specs(tt, w, l):
    nb = l // SUBLANES
    per = tt // SUBLANES
    main = pl.BlockSpec((1, tt, w), lambda bi, i: (bi, i, 0))
    prev = pl.BlockSpec((1, SUBLANES, w), lambda bi, i: (bi, jnp.maximum(i * per - 1, 0), 0))
    nxt = pl.BlockSpec((1, SUBLANES, w), lambda bi, i: (bi, jnp.minimum((i + 1) * per, nb - 1), 0))
    return [main, prev, nxt]


def _neighbours(cur, prow, nrow):
    tt = cur.shape[0]
    row = lax.broadcasted_iota(jnp.int32, cur.shape, 0)
    prev = jnp.where(row == 0, prow, pltpu.roll(cur, 1, 0))
    nxt = jnp.where(row == tt - 1, nrow, pltpu.roll(cur, tt - 1, 0))
    return prev, nxt


def _halo_rows(xp_ref, xn_ref, axis):
    i = pl.program_id(axis)
    last = pl.num_programs(axis) - 1
    prow = jnp.where(i > 0, xp_ref[0, SUBLANES - 1:SUBLANES, :], 0.0)
    nrow = jnp.where(i < last, xn_ref[0, 0:1, :], 0.0)
    return prow, nrow


def _rwkv_prep_kernel(x_ref, xp_ref, xn_ref, mu_ref, w0_ref, wup_ref, a0_ref, aup_ref, gup_ref,
                      kk_ref, ka_ref, rk_ref, seg_ref,
                      r_o, z_o, v_o, w_o, kd_o, b_o, bonus_o, g_o):
    cur = x_ref[0]
    prev, nxt = _neighbours(cur, *_halo_rows(xp_ref, xn_ref, 1))
    ps = cur + (0.5 * (prev + nxt) - cur) * mu_ref[...]
    c = RWKV_WIDTH
    r = ps[:, 0:c]
    k = ps[:, c:2 * c]
    v = ps[:, 2 * c:3 * c]
    o_a = 3 * c + 2 * DECAY_LORA
    o_g = o_a + 2 * AAA_LORA
    kk = k * kk_ref[...]
    kk = kk * lax.rsqrt(jnp.maximum(_seg_sum(kk * kk, seg_ref[...]), 1e-24))
    r_t, v_t, z_t = r.T, v.T, (-kk).T
    for d in range(2):
        rows = slice(c * d, c * (d + 1))
        r_o[rows, :] = r_t
        v_o[rows, :] = v_t
        z_o[rows, :] = z_t
        w_lo = ps[:, 3 * c + DECAY_LORA * d:3 * c + DECAY_LORA * (d + 1)]
        a_lo = ps[:, o_a + AAA_LORA * d:o_a + AAA_LORA * (d + 1)]
        decay = jnp.exp(-math.exp(-0.5) * jax.nn.sigmoid(w0_ref[d] + _dot_split(jnp.tanh(w_lo), wup_ref[d])))
        a = jax.nn.sigmoid(a0_ref[d] + _dot_split(a_lo, aup_ref[d]))
        w_o[rows, :] = decay.T
        kd_o[rows, :] = (k * (1.0 + (a - 1.0) * ka_ref[...])).T
        b_o[rows, :] = (kk * a).T
    bonus_o[0] = _seg_sum(r * k * rk_ref[...], seg_ref[...]) * v
    g_o[0] = _dot_split(jax.nn.sigmoid(ps[:, o_g:RWKV_IN_PAD]), gup_ref[...])


def rwkv_prep(rw, mu, w0, w_up, a0, a_up, g_up, k_k, k_a, r_k, seg, tt):
    b, l, w = rw.shape
    c = RWKV_WIDTH
    natural = jax.ShapeDtypeStruct((b, l, c), F32)
    time_major = jax.ShapeDtypeStruct((b * 2 * c, l), F32)
    spec_n = pl.BlockSpec((1, tt, c), lambda bi, i: (bi, i, 0))
    spec_t = pl.BlockSpec((2 * c, tt), lambda bi, i: (bi, i))
    consts = [mu, w0, w_up, a0, a_up, g_up, k_k, k_a, r_k, seg]
    return pl.pallas_call(
        _rwkv_prep_kernel,
        grid=(b, l // tt),
        in_specs=_halo_specs(tt, w, l) + [_const_spec(a.shape) for a in consts],
        out_specs=[spec_t] * 6 + [spec_n] * 2,
        out_shape=[time_major] * 6 + [natural] * 2,
        compiler_params=_params("parallel", "parallel"),
        name="rwkv_prep",
    )(rw, rw, rw, *consts)


SCAN_ROWS = 64
_R, _W, _KD, _V, _Z, _B = range(6)


def _rwkv_scan_kernel(*refs):
    ins, (yf_ref, yb_ref, s_ref, m_ref, sz_ref) = refs[:12], refs[-5:]
    tb, n, lanes = ins[0].shape
    fwd_lane = (lax.broadcasted_iota(jnp.int32, (n, lanes), 1) // RWKV_HEADS) % 2 == 0

    def stage(which, t):
        m_ref[which] = jnp.where(fwd_lane, ins[2 * which][t], ins[2 * which + 1][tb - 1 - t])

    @pl.when(pl.program_id(0) == 0)
    def _():
        s_ref[...] = jnp.zeros_like(s_ref)

    halves = [slice(h * SCAN_ROWS, (h + 1) * SCAN_ROWS) for h in range(n // SCAN_ROWS)]

    stage(_Z, 0)
    for rows in halves:
        acc = s_ref[0, rows, :] * m_ref[_Z, 0:1, :]
        for k in range(1, n):
            acc = acc + s_ref[k, rows, :] * m_ref[_Z, k:k + 1, :]
        sz_ref[rows, :] = acc

    def step(t, carry):
        for which in (_R, _W, _KD, _V, _B):
            stage(which, t)
        stage(_Z, jnp.minimum(t + 1, tb - 1))
        for rows in halves:
            sz = sz_ref[rows, :]
            vt = m_ref[_V, rows, :]
            y = None
            sz_next = None
            for k in range(n):
                sk = (s_ref[k, rows, :] * m_ref[_W, k:k + 1, :] + sz * m_ref[_B, k:k + 1, :]
                      + vt * m_ref[_KD, k:k + 1, :])
                s_ref[k, rows, :] = sk
                yk = sk * m_ref[_R, k:k + 1, :]
                zk = sk * m_ref[_Z, k:k + 1, :]
                y = yk if y is None else y + yk
                sz_next = zk if sz_next is None else sz_next + zk
            sz_ref[rows, :] = sz_next
            yf_ref[t, rows, :] = y
            yb_ref[tb - 1 - t, rows, :] = y
        return carry

    lax.fori_loop(0, tb, step, 0)


def rwkv_scan(arrays, tb, after=()):
    l, n, lanes = arrays[0].shape
    nt = l // tb
    fwd = pl.BlockSpec((tb, n, lanes), lambda i: (i, 0, 0))
    rev = pl.BlockSpec((tb, n, lanes), lambda i: (nt - 1 - i, 0, 0))
    out = jax.ShapeDtypeStruct((l, n, lanes), F32)
    return pl.pallas_call(
        _rwkv_scan_kernel,
        grid=(nt,),
        in_specs=[fwd, rev] * 6 + _anchor_specs(after),
        out_specs=[fwd, rev],
        out_shape=[out, out],
        scratch_shapes=[pltpu.VMEM((n, n, lanes), F32), pltpu.VMEM((6, n, lanes), F32),
                        pltpu.VMEM((n, lanes), F32)],
        compiler_params=_params("arbitrary"),
        name="rwkv_scan",
    )(*[a for a in arrays for _ in range(2)], *after)


def _mix_out_kernel(rwkv, *refs):
    if rwkv:
        (ya0_ref, ya1_ref, yf_ref, yb_ref, bonus_ref, g_ref, lng_ref, lnb_ref, seg_ref,
         x_ref, gate_ref, w_ref, o_ref) = refs
        first_half = pl.program_id(0) < pl.num_programs(0) // 2
        y_attn = jnp.where(first_half, ya0_ref[0], ya1_ref[0])
        y = yf_ref[...] + yb_ref[...]
        inv_n = 1.0 / RWKV_HEAD
        mean = _seg_sum(y, seg_ref[...]) * inv_n
        cen = y - mean
        var = _seg_sum(cen * cen, seg_ref[...]) * inv_n
        yn = cen * lax.rsqrt(var + GN_EPS) * lng_ref[...] + lnb_ref[...]
        yr = (yn + bonus_ref[0]) * g_ref[0]
        a = jnp.concatenate([y_attn, yr], axis=-1)
    else:
        a_ref, x_ref, gate_ref, w_ref, o_ref = refs
        a = a_ref[0]
    o_ref[0] = x_ref[0] + gate_ref[...] * _dot(a.astype(BF16), w_ref[...])


def mix_out(acts, consts, x, mod4, layer, w, tm, rwkv):
    b, l, d = x.shape
    in_specs = []
    for i_act, a in enumerate(acts):
        if not isinstance(a, tuple) and a.shape[0] == b // 2:
            in_specs.append(pl.BlockSpec((1, tm, a.shape[2]), lambda bi, i: (bi % (b // 2), i, 0)))
        elif isinstance(a, tuple):
            arr, direction = a
            width = arr.shape[1] // (2 * b)
            in_specs.append(pl.BlockSpec((tm, width), lambda bi, i, dr=direction: (i, 2 * bi + dr)))
            acts[i_act] = arr
        else:
            in_specs.append(pl.BlockSpec((1, tm, a.shape[2]), lambda bi, i: (bi, i, 0)))
    in_specs += [_const_spec(a.shape) for a in consts]
    in_specs += [pl.BlockSpec((1, tm, d), lambda bi, i: (bi, i, 0)), _mod_spec(d, layer, 2), _const_spec(w.shape)]
    return pl.pallas_call(
        functools.partial(_mix_out_kernel, rwkv),
        grid=(b, l // tm),
        in_specs=in_specs,
        out_specs=pl.BlockSpec((1, tm, d), lambda bi, i: (bi, i, 0)),
        out_shape=jax.ShapeDtypeStruct((b, l, d), F32),
        compiler_params=_params("parallel", "parallel"),
        name="mix_out",
    )(*acts, *consts, x, mod4, w)


def _ffn_kernel(x_ref, sh_ref, sc_ref, gate_ref, g_ref, w1_ref, w3_ref, w2_ref, *refs):
    o_ref = refs[-1]
    x = x_ref[0]
    h = _rms_mod(x, g_ref[...], sc_ref[...], sh_ref[...]).astype(BF16)
    a = _dot(h, w1_ref[...])
    u = (a * jax.nn.sigmoid(a)) * _dot(h, w3_ref[...])
    out = x + gate_ref[...] * _dot(u.astype(BF16), w2_ref[...])
    if len(refs) == 2:
        ms = jnp.mean(out * out, axis=-1, keepdims=True)
        out = out * lax.rsqrt(ms + NORM_EPS) * refs[0][...]
    o_ref[0] = out


def ffn(x, mod4, layer, gain, w1, w3, w2, tm, final_gain=None):
    b, l, d = x.shape
    closing = [] if final_gain is None else [final_gain.reshape(1, d)]
    return pl.pallas_call(
        _ffn_kernel,
        grid=(b, l // tm),
        in_specs=[pl.BlockSpec((1, tm, d), lambda bi, i: (bi, i, 0)),
                  _mod_spec(d, layer, 3), _mod_spec(d, layer, 4), _mod_spec(d, layer, 5),
                  _const_spec((1, d)), _const_spec(w1.shape), _const_spec(w3.shape), _const_spec(w2.shape)]
        + [_const_spec((1, d)) for _ in closing],
        out_specs=pl.BlockSpec((1, tm, d), lambda bi, i: (bi, i, 0)),
        out_shape=jax.ShapeDtypeStruct((b, l, d), F32),
        compiler_params=_params("parallel", "parallel"),
        name="ffn",
    )(x, mod4, mod4, mod4, gain.reshape(1, d), w1, w3, w2, *closing)


def _conv3(cur, prow, nrow, w_ref, b_ref):
    prev, nxt = _neighbours(cur, prow, nrow)
    return prev * w_ref[0:1, :] + cur * w_ref[1:2, :] + nxt * w_ref[2:3, :] + b_ref[...]


def _hy_filter_kernel(d_tiles, feats_ref, w1_ref, b1_ref, w2_ref, b2_ref, w3_ref, b3_ref, fr_ref,
                      wout_ref, win_ref, o_ref, hid_s):
    j = pl.program_id(0)

    @pl.when(j == 0)
    def _():
        fr = fr_ref[...]
        hid = jnp.sin(fr * (_dot_hi(feats_ref[...], w1_ref[...]) + b1_ref[...]))
        hid = jnp.sin(fr * (_dot_hi(hid, w2_ref[...]) + b2_ref[...]))
        hid_s[...] = jnp.sin(fr * (_dot_hi(hid, w3_ref[...]) + b3_ref[...]))

    filt = _dot_split(hid_s[...], wout_ref[...]) * win_ref[...]
    backward = (j // d_tiles) % 2 == 1
    row = lax.broadcasted_iota(jnp.int32, filt.shape, 0)
    o_ref[...] = jnp.where(jnp.logical_and(backward, row == 0), 0.0, filt)


def hy_filter(feats, w1, b1, w2, b2, w3, b3, fr, w_out, window, tn):
    l = feats.shape[0]
    hf = w2.shape[0]
    n = w_out.shape[1]
    d = window.shape[1]
    d_tiles = d // tn
    consts = [feats, w1, b1.reshape(1, hf), w2, b2.reshape(1, hf), w3, b3.reshape(1, hf), fr.reshape(1, hf)]
    return pl.pallas_call(
        functools.partial(_hy_filter_kernel, d_tiles),
        grid=(n // tn,),
        in_specs=[_const_spec(a.shape) for a in consts]
        + [pl.BlockSpec((hf, tn), lambda j: (0, j)), pl.BlockSpec((l, tn), lambda j: (0, j % d_tiles))],
        out_specs=pl.BlockSpec((l, tn), lambda j: (0, j)),
        out_shape=jax.ShapeDtypeStruct((l, n), F32),
        scratch_shapes=[pltpu.VMEM((l, hf), F32)],
        compiler_params=_params("arbitrary"),
        name="hy_filter",
    )(*consts, w_out, window)


def _dft_matrices(l):
    n = 2 * l
    lane = min(LANES, l)
    k = jnp.arange(l, dtype=jnp.int32)[:, None]
    t = jnp.arange(l, dtype=jnp.int32)
    a = ((k * jnp.arange(lane, dtype=jnp.int32)[None, :]) % n).astype(F32) * (2.0 * math.pi / n)
    c = ((k * (lane * jnp.arange(l // lane, dtype=jnp.int32))[None, :]) % n).astype(F32) * (2.0 * math.pi / n)
    ca, sa, cc, sc = jnp.cos(a)[:, None, :], jnp.sin(a)[:, None, :], jnp.cos(c)[:, :, None], jnp.sin(c)[:, :, None]
    cos = (ca * cc - sa * sc).reshape(l, l)
    sin = (sa * cc + ca * sc).reshape(l, l)
    alt = jnp.where(t % 2 == 0, 1.0, -1.0).astype(F32)
    f_re = cos.astype(BF16)
    f_im = jnp.where(k == 0, alt[None, :], -sin).astype(BF16)
    col0 = (t == 0)[None, :]
    g_re = jnp.where(col0, 1.0 / n, (2.0 / n) * cos).astype(BF16)
    g_im = jnp.where(col0, alt[:, None] / n, (-2.0 / n) * sin).astype(BF16)
    return (f_re, f_im), (g_re, g_im)


def _nyquist_slot(i, shape):
    row = lax.broadcasted_iota(jnp.int32, shape, 0)
    return jnp.logical_and(i == 0, row == 0)


def _dft_spec_kernel(fre_ref, fim_ref, ft_ref, fb_ref, bt_ref, bb_ref, o_ref):
    fre, fim = fre_ref[...], fim_ref[...]
    ft, fb, bt, bb = (ref[...].astype(BF16) for ref in (ft_ref, fb_ref, bt_ref, bb_ref))
    af_r, af_i = _dot(fre, ft), _dot(fim, ft)
    bf_r, bf_i = _dot(fre, fb), _dot(fim, fb)
    ab_r, ab_i = _dot(fre, bt), _dot(fim, bt)
    bb_r, bb_i = _dot(fre, bb), _dot(fim, bb)
    nyq = _nyquist_slot(pl.program_id(1), af_r.shape)
    row = lax.broadcasted_iota(jnp.int32, af_r.shape, 0)
    sgn = jnp.where(row % 2 == 0, 1.0, -1.0)
    lag0 = ft_ref[0:1, :]
    o_ref[0, 0] = (af_r + ab_r).astype(BF16)
    o_ref[0, 1] = jnp.where(nyq, af_i + ab_i, af_i - ab_i).astype(BF16)
    o_ref[1, 0] = (bf_r + sgn * (af_r - lag0)).astype(BF16)
    o_ref[1, 1] = (bf_i + jnp.where(nyq, af_i - lag0, sgn * af_i)).astype(BF16)
    o_ref[2, 0] = (sgn * ab_r + bb_r).astype(BF16)
    o_ref[2, 1] = jnp.where(nyq, ab_i + bb_i, -(sgn * ab_i + bb_i)).astype(BF16)


def dft_spec(fwd, taps, d, ft, tn):
    f_re, f_im = fwd
    m = f_re.shape[0]
    d_tiles = d // tn
    orders = taps.shape[1] // (2 * d)
    f_spec = pl.BlockSpec((ft, m), lambda j, i: (i, 0))

    def tap_spec(half, direction):
        return pl.BlockSpec((m, tn), lambda j, i: (half, (j // d_tiles) * 2 * d_tiles + direction * d_tiles
                                                    + j % d_tiles))

    return pl.pallas_call(
        _dft_spec_kernel,
        grid=(orders * d_tiles, m // ft),
        in_specs=[f_spec, f_spec, tap_spec(0, 0), tap_spec(1, 0), tap_spec(0, 1), tap_spec(1, 1)],
        out_specs=pl.BlockSpec((3, 2, ft, tn), lambda j, i: (0, 0, i, j)),
        out_shape=jax.ShapeDtypeStruct((3, 2, m, orders * d), BF16),
        compiler_params=_params("parallel", "parallel"),
        name="dft_spec",
    )(f_re, f_im, taps, taps, taps, taps)


def _dft_mul_kernel(has_conv, fre_ref, fim_ref, u_ref, h_ref, *refs):
    o_ref, u_s = refs[-2:]
    i = pl.program_id(2)

    @pl.when(i == 0)
    def _():
        u = u_ref[0]
        if has_conv:
            u = _conv3(u, 0.0, 0.0, *refs[:2])
        u_s[...] = u.astype(BF16)

    m = u_s.shape[0] // 2
    ft = h_ref.shape[2]
    rows = pl.ds(pl.multiple_of(i * ft, ft), ft)
    fre, fim = fre_ref[rows, :], fim_ref[rows, :]
    head = 2 * SUBLANES
    for c0 in range(0, o_ref.shape[-1], MXU_COLS):
        cols = slice(c0, c0 + MXU_COLS)
        tr, ti = _dot(fre, u_s[0:m, cols]), _dot(fim, u_s[0:m, cols])
        br, bi = _dot(fre, u_s[m:, cols]), _dot(fim, u_s[m:, cols])
        h0r, h0i, hpr, hpi, hmr, hmi = (h_ref[a, p, :, cols].astype(F32) for a in range(3) for p in range(2))
        nyq = _nyquist_slot(i, (head, MXU_COLS))

        def emit(half, plane, general, packed):
            o_ref[0, half, plane, 0:head, cols] = jnp.where(nyq, packed, general[0:head]).astype(BF16)
            o_ref[0, half, plane, head:, cols] = general[head:].astype(BF16)

        def top(x):
            return x[0:head]

        emit(0, 0, h0r * tr - h0i * ti + hmr * br - hmi * bi, top(h0r) * top(tr) + top(hmr) * top(br))
        emit(0, 1, h0r * ti + h0i * tr + hmr * bi + hmi * br, top(h0i) * top(ti) + top(hmi) * top(bi))
        emit(1, 0, hpr * tr - hpi * ti + h0r * br - h0i * bi, top(hpr) * top(tr) + top(h0r) * top(br))
        emit(1, 1, hpr * ti + hpi * tr + h0r * bi + h0i * br, top(hpi) * top(ti) + top(h0i) * top(bi))


def dft_mul(fwd, u, u_col, spec, spec_col, d, ft, tn, conv=None):
    f_re, f_im = fwd
    m = f_re.shape[0]
    b = u.shape[0]
    d_tiles = d // tn
    f_spec = _const_spec((m, m))
    conv_specs = [] if conv is None else [pl.BlockSpec((3, tn), lambda bi, j, i: (0, u_col * d_tiles + j)),
                                          pl.BlockSpec((1, tn), lambda bi, j, i: (0, u_col * d_tiles + j))]
    return pl.pallas_call(
        functools.partial(_dft_mul_kernel, conv is not None),
        grid=(b, d_tiles, m // ft),
        in_specs=[f_spec, f_spec,
                  pl.BlockSpec((1, 2 * m, tn), lambda bi, j, i: (bi, 0, u_col * d_tiles + j)),
                  pl.BlockSpec((3, 2, ft, tn), lambda bi, j, i: (0, 0, i, spec_col * d_tiles + j))] + conv_specs,
        out_specs=pl.BlockSpec((1, 2, 2, ft, tn), lambda bi, j, i: (bi, 0, 0, i, j)),
        out_shape=jax.ShapeDtypeStruct((b, 2, 2, m, d), BF16),
        scratch_shapes=[pltpu.VMEM((2 * m, tn), BF16)],
        compiler_params=_params("parallel", "parallel", "arbitrary"),
        name="dft_mul",
    )(f_re, f_im, u, spec, *(conv or ()))


def _dft_inv_kernel(conv_u, per_half, gre_ref, gim_ref, y_ref, *refs):
    u_refs, (g_ref, gp_ref, gn_ref, gw_ref, gb_ref, skip_ref, o_ref) = refs[:-7], refs[-7:]
    u = u_refs[0][0]
    if conv_u:
        u = _conv3(u, *_halo_rows(u_refs[1], u_refs[2], 2), u_refs[3], u_refs[4])
    gate = _conv3(g_ref[0], *_halo_rows(gp_ref, gn_ref, 2), gw_ref, gb_ref)
    tm = o_ref.shape[1]
    rows = pl.ds(pl.multiple_of((pl.program_id(2) % per_half) * tm, tm), tm)
    gre, gim = gre_ref[rows, :], gim_ref[rows, :]
    for c0 in range(0, o_ref.shape[-1], MXU_COLS):
        cols = slice(c0, c0 + MXU_COLS)
        conv = _dot(gre, y_ref[0, 0, 0, :, cols]) + _dot(gim, y_ref[0, 0, 1, :, cols])
        o_ref[0, :, cols] = gate[:, cols] * (conv + u[:, cols] * skip_ref[:, cols])


def dft_inv(inv, y, u, u_col, gate, gate_col, skip, tm, tn, conv_w, conv_b, conv_u):
    g_re, g_im = inv
    m = g_re.shape[0]
    b, _, _, _, d = y.shape
    d_tiles = d // tn
    per_half = m // tm
    rows8 = tm // SUBLANES
    last8 = 2 * m // SUBLANES - 1
    g_spec = _const_spec((m, m))

    def tile_specs(col, halo):
        specs = [pl.BlockSpec((1, tm, tn), lambda bi, j, i: (bi, i, col * d_tiles + j))]
        if halo:
            specs += [pl.BlockSpec((1, SUBLANES, tn),
                                   lambda bi, j, i: (bi, jnp.maximum(i * rows8 - 1, 0), col * d_tiles + j)),
                      pl.BlockSpec((1, SUBLANES, tn),
                                   lambda bi, j, i: (bi, jnp.minimum((i + 1) * rows8, last8), col * d_tiles + j)),
                      pl.BlockSpec((3, tn), lambda bi, j, i: (0, col * d_tiles + j)),
                      pl.BlockSpec((1, tn), lambda bi, j, i: (0, col * d_tiles + j))]
        return specs

    u_ops = [u, u, u, conv_w, conv_b] if conv_u else [u]
    return pl.pallas_call(
        functools.partial(_dft_inv_kernel, conv_u, per_half),
        grid=(b, d_tiles, 2 * per_half),
        in_specs=[g_spec, g_spec,
                  pl.BlockSpec((1, 1, 2, m, tn), lambda bi, j, i: (bi, i // per_half, 0, 0, j))]
        + tile_specs(u_col, conv_u) + tile_specs(gate_col, True)
        + [pl.BlockSpec((1, tn), lambda bi, j, i: (0, j))],
        out_specs=pl.BlockSpec((1, tm, tn), lambda bi, j, i: (bi, i, j)),
        out_shape=jax.ShapeDtypeStruct((b, 2 * m, d), F32),
        compiler_params=_params("parallel", "parallel", "parallel"),
        name="dft_inv",
    )(g_re, g_im, y, *u_ops, gate, gate, gate, conv_w, conv_b, skip.reshape(1, d))


def _rope_tables(l):
    rows = l // GRID_W
    row = jnp.repeat(jnp.arange(rows), GRID_W).astype(F32)
    col = jnp.tile(jnp.arange(GRID_W), rows).astype(F32)
    half = HEAD_DIM // 2
    inv_freq = ROPE_THETA ** (-jnp.arange(0, half, 2, dtype=F32) / half)
    ang = jnp.concatenate([row[:, None] * inv_freq, col[:, None] * inv_freq], axis=-1)
    cos, sin = jnp.cos(ang), jnp.sin(ang)
    return jnp.concatenate([cos, cos], axis=-1), jnp.concatenate([sin, sin], axis=-1)


def _hyena_tables(l, d):
    t = jnp.linspace(0.0, 1.0, l, dtype=F32)[:, None]
    omega = (2.0 * math.pi / l) * jnp.arange(l, dtype=F32)[:, None]
    bands = jnp.linspace(1e-4, HYENA_BANDS - 1, HYENA_BANDS, dtype=F32)[None, :]
    feats = jnp.concatenate([t, jnp.cos(bands * omega), -jnp.sin(bands * omega)], axis=-1)
    min_decay = math.log(HYENA_TARGET) / HYENA_SLOW_DECAY
    max_decay = math.log(HYENA_TARGET) / HYENA_FAST_DECAY
    deltas = jnp.abs(jnp.linspace(min_decay, max_decay, d, dtype=F32))
    window = jnp.exp(-t * deltas)
    return feats, window


def _from_scan_layout(y):
    return y.transpose(0, 2, 1).reshape(y.shape[0], -1)


def kernel(x, c, mix_w_in, mix_w_out, attn_q_norm, attn_k_norm, rwkv_mu, rwkv_w0, rwkv_w_up, rwkv_a0, rwkv_a_up, rwkv_g_up, rwkv_k_k, rwkv_k_a, rwkv_r_k, rwkv_ln_g, rwkv_ln_b, hy_w_in, hy_conv_w, hy_conv_b, hy_f_w1, hy_f_b1, hy_f_w2, hy_f_b2, hy_f_w3, hy_f_b3, hy_sin_freq, hy_f_out, hy_skip, hy_w_out, ada_w, ada_b, norm_mix, norm_ffn, ffn_w1, ffn_w3, ffn_w2, final_norm):
    b, l, d = x.shape
    depth = ada_w.shape[0]
    tm = min(ROW_TILE, l)
    tp = min(FFN_ROWS, l)
    c_w = RWKV_WIDTH

    mod4 = ada_mod(c, ada_w, ada_b).reshape(depth, b, 1, 6 * d)

    perm = jnp.concatenate([jnp.arange(0, HEAD_DIM, 2), jnp.arange(1, HEAD_DIM, 2)])
    w_in = mix_w_in[0]
    w_q = w_in[:, :ATTN_WIDTH].reshape(d, ATTN_HEADS, HEAD_DIM)[:, :, perm].reshape(d, ATTN_WIDTH)
    w_kv = w_in[:, ATTN_WIDTH:ATTN_WIDTH + 2 * KV_WIDTH]
    w_k = w_kv[:, :KV_WIDTH].reshape(d, ATTN_KV_HEADS, HEAD_DIM)[:, :, perm].reshape(d, KV_WIDTH)
    w_kv = jnp.concatenate([w_k, w_kv[:, KV_WIDTH:]], axis=1)
    w_rw = jnp.pad(w_in[:, ATTN_WIDTH + 2 * KV_WIDTH:], ((0, 0), (0, RWKV_IN_PAD - RWKV_IN)))
    q, kv, rw = proj(x, mod4, 0, norm_mix[0], [w_q.astype(BF16), w_kv.astype(BF16), w_rw.astype(BF16)], tp)

    seg = jnp.kron(jnp.eye(RWKV_HEADS, dtype=BF16), jnp.ones((RWKV_HEAD, RWKV_HEAD), BF16))
    r, z, v, w, kd, bb, bonus, g = rwkv_prep(
        rw, jnp.pad(rwkv_mu[0], (0, RWKV_IN_PAD - RWKV_IN)).reshape(1, RWKV_IN_PAD),
        rwkv_w0[0].reshape(2, 1, c_w), rwkv_w_up[0], rwkv_a0[0].reshape(2, 1, c_w), rwkv_a_up[0],
        jnp.pad(rwkv_g_up[0], ((0, RWKV_IN_PAD - RWKV_IN), (0, 0))),
        rwkv_k_k[0].reshape(1, c_w), rwkv_k_a[0].reshape(1, c_w), rwkv_r_k[0].reshape(1, c_w), seg, tm)
    stage1 = [r, w, kd, v, z, bb]
    cos, sin = _rope_tables(l)
    qn, kn = attn_q_norm[0][perm].reshape(1, HEAD_DIM), attn_k_norm[0][perm].reshape(1, HEAD_DIM)
    half = b // 2
    y_attn0 = attn(q, kv, cos, sin, qn, kn, tm, 0, half)
    tn = min(COL_TILE, d)
    feats, window = _hyena_tables(l, d)
    feats = jnp.pad(feats, ((0, 0), (0, LANES - feats.shape[1])))
    f_w1 = jnp.pad(hy_f_w1[0], ((0, LANES - hy_f_w1.shape[1]), (0, 0)))
    taps = hy_filter(feats, f_w1, hy_f_b1[0], hy_f_w2[0], hy_f_b2[0], hy_f_w3[0], hy_f_b3[0],
                     hy_sin_freq[0], hy_f_out[0], window, tn)
    scan_operands = [a.reshape(-1, RWKV_HEAD, l).transpose(2, 1, 0) for a in stage1]
    y_attn1 = attn(q, kv, cos, sin, qn, kn, tm, half, b - half)
    fwd, inv = _dft_matrices(l // 2)
    ft = min(FREQ_TILE, l // 2)
    spec = dft_spec(fwd, taps, d, ft, tn)

    y_f, y_b = rwkv_scan(scan_operands, min(SCAN_BLOCK, l), after=(y_attn0, y_attn1, spec))
    fwd_lane = (jnp.arange(y_f.shape[2]) // RWKV_HEADS) % 2 == 0
    y_t = _from_scan_layout(jnp.where(fwd_lane[None, None, :], y_f, y_b))
    x = mix_out([y_attn0, y_attn1, (y_t, 0), (y_t, 1), bonus, g],
                [rwkv_ln_g[0].reshape(1, c_w), rwkv_ln_b[0].reshape(1, c_w), seg],
                x, mod4, 0, mix_w_out[0].astype(BF16), tp, True)
    x = ffn(x, mod4, 0, norm_ffn[0], ffn_w1[0].astype(BF16), ffn_w3[0].astype(BF16), ffn_w2[0].astype(BF16),
            tp)

    (p3,) = proj(x, mod4, 1, norm_mix[1], [hy_w_in[0].astype(BF16)], tp)
    conv_w, conv_b = hy_conv_w[0], hy_conv_b[0].reshape(1, 3 * d)
    y1 = dft_mul(fwd, p3, 0, spec, 0, d, ft, tn, conv=(conv_w, conv_b))
    z1 = dft_inv(inv, y1, p3, 0, p3, 1, hy_skip[0, 0], ft, tn, conv_w, conv_b, True)
    y2 = dft_mul(fwd, z1, 0, spec, 1, d, ft, tn)
    z2 = dft_inv(inv, y2, z1, 0, p3, 2, hy_skip[0, 1], ft, tn, conv_w, conv_b, False)
    x = mix_out([z2], [], x, mod4, 1, hy_w_out[0].astype(BF16), tp, False)
    x = ffn(x, mod4, 1, norm_ffn[1], ffn_w1[1].astype(BF16), ffn_w3[1].astype(BF16), ffn_w2[1].astype(BF16),
            tp, final_gain=final_norm)
    return x
```
